```python
import jax, jax.numpy as jnp
from jax import lax
import numpy as np

D_MODEL = 2048
BATCH = 4
SEQ = 2048
DEPTH = 1
DEC_BATCH = 128
DEC_SEQ = 4
PAST_LEN = 2048
PAGE_SIZE = 128

SSM_EXPAND = 2
D_INNER = SSM_EXPAND * D_MODEL
SSM_HEAD_DIM = 64
SSM_HEADS = D_INNER // SSM_HEAD_DIM
SSM_GROUPS = 8
SSM_STATE = 128
CONV_WIDTH = 4
CONV_DIM = D_INNER + 2 * SSM_GROUPS * SSM_STATE
SSD_CHUNK = 128
ATT_HEAD_DIM = 128
ATT_HEADS_PER_GROUP = 8
ATT_PATTERNS = ((128, 1), (512, 4), (2048, 16))
ATT_N_GROUPS = len(ATT_PATTERNS)
ATT_HEADS = ATT_N_GROUPS * ATT_HEADS_PER_GROUP
ATT_WIDTH = ATT_HEADS * ATT_HEAD_DIM
ATT_OUT_WIDTH = ATT_HEADS_PER_GROUP * ATT_HEAD_DIM
N_BUCKETS = 32
MAX_DISTANCE = max(w for w, _ in ATT_PATTERNS)
D_FF = 4 * D_MODEL
EPS = 1e-6
IN_SPLITS = (D_INNER, CONV_DIM, SSM_HEADS, ATT_WIDTH, ATT_WIDTH, ATT_WIDTH, D_MODEL, D_MODEL)
IN_TOTAL = sum(IN_SPLITS)

kernel_name = "hybrid_ssd_dilated_swa_decode_step"


def rmsnorm(x, g):
    xf = x.astype(jnp.float32)
    y = xf * lax.rsqrt(jnp.mean(xf * xf, axis=-1, keepdims=True) + EPS)
    return (y * g.astype(jnp.float32)).astype(x.dtype)


def t5_bucket(dist):
    max_exact = N_BUCKETS // 2
    d = np.maximum(dist, max_exact).astype(np.float32)
    large = max_exact + (np.log(d / max_exact) / np.log(MAX_DISTANCE / max_exact) * (N_BUCKETS - max_exact)).astype(np.int32)
    large = np.minimum(large, N_BUCKETS - 1)
    return np.where(dist < max_exact, dist, large).astype(np.int32)


def group_rel_biases(rel_bias):
    out = []
    for g, (win, dil) in enumerate(ATT_PATTERNS):
        n_keys = win // dil
        buckets = t5_bucket(np.arange(n_keys + 1) * dil)
        out.append(rel_bias[buckets][:, g * ATT_HEADS_PER_GROUP:(g + 1) * ATT_HEADS_PER_GROUP])
    return out


def dilated_attn_prompt(q, k, v, bias, dil, n_keys):
    bsz, S, H, E = q.shape
    M = S // dil
    KB = n_keys
    nb = -(-M // KB)
    pad_r = nb * KB - M
    def to_res(t):
        return t.reshape(bsz, M, dil, H, E)
    qr = jnp.pad(to_res(q), ((0, 0), (0, pad_r), (0, 0), (0, 0), (0, 0))).reshape(bsz, nb, KB, dil, H, E)
    def kv_blocks(t):
        tp = jnp.pad(to_res(t), ((0, 0), (KB, pad_r), (0, 0), (0, 0), (0, 0))).reshape(bsz, nb + 1, KB, dil, H, E)
        return jnp.concatenate([tp[:, :-1], tp[:, 1:]], axis=2)
    kb, vb = kv_blocks(k), kv_blocks(v)
    logits = jnp.einsum('bnqrhe,bnkrhe->bnrhqk', qr, kb, preferred_element_type=jnp.float32) * (ATT_HEAD_DIM ** -0.5)
    rel = np.arange(KB)[:, None] + KB - np.arange(2 * KB)[None, :]
    key_m = np.arange(nb)[:, None] * KB - KB + np.arange(2 * KB)[None, :]
    valid = ((rel >= 0) & (rel <= n_keys))[None] & (key_m >= 0)[:, None, :]
    bias_qk = jnp.transpose(bias[np.clip(rel, 0, n_keys)], (2, 0, 1)).astype(jnp.float32)
    logits = jnp.where(valid[None, :, None, None], logits + bias_qk[None, None, None], -jnp.inf)
    lse = jax.nn.logsumexp(logits, axis=-1)
    p = jnp.exp(logits - lse[..., None])
    o = jnp.einsum('bnrhqk,bnkrhe->bnqrhe', p.astype(v.dtype), vb, preferred_element_type=jnp.float32)
    o = o.reshape(bsz, nb * KB, dil, H, E)[:, :M].reshape(bsz, S, H, E)
    lse = jnp.transpose(lse, (0, 1, 4, 2, 3)).reshape(bsz, nb * KB, dil, H)[:, :M].reshape(bsz, S, H)
    return o, lse


def dilated_attn_sample(q, k_new, v_new, k_buf, v_buf, bias, dil, n_keys):
    bsz, T, H, E = q.shape
    L = k_buf.shape[1]
    kc = jnp.concatenate([k_buf.astype(k_new.dtype), k_new], axis=1)
    vc = jnp.concatenate([v_buf.astype(v_new.dtype), v_new], axis=1)
    idx = L + np.arange(T)[:, None] - np.arange(n_keys + 1)[None, :] * dil
    valid = idx >= 0
    flat = np.clip(idx, 0, None).reshape(-1)
    kg = jnp.take(kc, flat, axis=1).reshape(bsz, T, n_keys + 1, H, E)
    vg = jnp.take(vc, flat, axis=1).reshape(bsz, T, n_keys + 1, H, E)
    logits = jnp.einsum('bthe,btjhe->bthj', q, kg, preferred_element_type=jnp.float32) * (ATT_HEAD_DIM ** -0.5)
    logits = jnp.where(valid[None, :, None, :], logits + bias.T.astype(jnp.float32)[None, None], -jnp.inf)
    lse = jax.nn.logsumexp(logits, axis=-1)
    p = jnp.exp(logits - lse[..., None])
    o = jnp.einsum('bthj,btjhe->bthe', p.astype(vg.dtype), vg, preferred_element_type=jnp.float32)
    return o, lse


def attention_branch(q, k, v, win_bufs, biases):
    bsz, L, _ = q.shape
    shp = (bsz, L, ATT_N_GROUPS, ATT_HEADS_PER_GROUP, ATT_HEAD_DIM)
    q, k, v = q.reshape(shp), k.reshape(shp), v.reshape(shp)
    outs, lses, new_rows = [], [], []
    for g, (win, dil) in enumerate(ATT_PATTERNS):
        n_keys = win // dil
        qg, kg, vg = q[:, :, g], k[:, :, g], v[:, :, g]
        if win_bufs is None:
            o, lse = dilated_attn_prompt(qg, kg, vg, biases[g], dil, n_keys)
            keep = min(win, L)
            new_rows.append(jnp.stack([kg[:, L - keep:], vg[:, L - keep:]], axis=2))
        else:
            buf = win_bufs[g]
            o, lse = dilated_attn_sample(qg, kg, vg, buf[:, :, 0], buf[:, :, 1], biases[g], dil, n_keys)
            new_rows.append(jnp.stack([kg, vg], axis=2))
        outs.append(o)
        lses.append(lse)
    alpha = jax.nn.softmax(jnp.stack(lses), axis=0)
    o = jnp.sum(alpha[..., None] * jnp.stack(outs), axis=0)
    return o.reshape(bsz, L, ATT_OUT_WIDTH), new_rows


def ssd_chunked(x, dt, A, b, c, h0, chunk):
    bsz, L, H, P = x.shape
    G, N = b.shape[2], b.shape[3]
    Hg = H // G
    nc = L // chunk
    f32 = jnp.float32
    xc = x.astype(f32).reshape(bsz, nc, chunk, G, Hg, P)
    dtc = dt.reshape(bsz, nc, chunk, G, Hg)
    bc = b.astype(f32).reshape(bsz, nc, chunk, G, N)
    cc = c.astype(f32).reshape(bsz, nc, chunk, G, N)
    acum = jnp.cumsum(dtc * A.reshape(G, Hg), axis=2)
    causal = np.tril(np.ones((chunk, chunk), dtype=bool))
    seg = acum[:, :, :, None] - acum[:, :, None, :]
    decay = jnp.exp(jnp.where(causal[:, :, None, None], seg, -jnp.inf))
    cb = jnp.einsum('bcqgn,bckgn->bcqkg', cc, bc)
    w = cb[..., None] * decay * dtc[:, :, None]
    y_diag = jnp.einsum('bcqkgh,bckghp->bcqghp', w, xc)
    decay_end = jnp.exp(acum[:, :, -1:] - acum) * dtc
    states = jnp.einsum('bckgn,bckgh,bckghp->bcghpn', bc, decay_end, xc)
    chunk_decay = jnp.exp(acum[:, :, -1])
    def step(h, inp):
        s, d = inp
        return d[..., None, None] * h + s, h
    h_T, h_prev = lax.scan(step, h0.astype(f32).reshape(bsz, G, Hg, P, N),
                           (jnp.moveaxis(states, 1, 0), jnp.moveaxis(chunk_decay, 1, 0)))
    h_prev = jnp.moveaxis(h_prev, 0, 1)
    y_off = jnp.einsum('bcqgn,bcghpn->bcqghp', cc, h_prev) * jnp.exp(acum)[..., None]
    y = (y_diag + y_off).reshape(bsz, L, H, P)
    return y, h_T.reshape(bsz, H, P, N)


def ssd_branch(z, xbc, dt_raw, conv_state, ssm_state, conv_w, conv_b, dt_bias, a_log, d_skip, ssm_norm):
    bsz, L, _ = xbc.shape
    xpad = jnp.concatenate([conv_state.astype(xbc.dtype), xbc], axis=1)
    conv = conv_b
    for i in range(CONV_WIDTH):
        conv = conv + xpad[:, i:i + L] * conv_w[i]
    new_conv_state = xpad[:, L:]
    u = jax.nn.silu(conv)
    xs, b_in, c_in = jnp.split(u, [D_INNER, D_INNER + SSM_GROUPS * SSM_STATE], axis=-1)
    xs = xs.reshape(bsz, L, SSM_HEADS, SSM_HEAD_DIM)
    b_in = b_in.reshape(bsz, L, SSM_GROUPS, SSM_STATE)
    c_in = c_in.reshape(bsz, L, SSM_GROUPS, SSM_STATE)
    dt = jax.nn.softplus(dt_raw.astype(jnp.float32) + dt_bias.astype(jnp.float32))
    A = -jnp.exp(a_log.astype(jnp.float32))
    chunk = SSD_CHUNK if L % SSD_CHUNK == 0 else L
    y, h_T = ssd_chunked(xs, dt, A, b_in, c_in, ssm_state, chunk)
    y = y + d_skip.astype(jnp.float32)[:, None] * xs.astype(jnp.float32)
    y = y.reshape(bsz, L, D_INNER) * jax.nn.silu(z.astype(jnp.float32))
    yg = y.reshape(bsz, L, SSM_GROUPS, D_INNER // SSM_GROUPS)
    yg = yg * lax.rsqrt(jnp.mean(yg * yg, axis=-1, keepdims=True) + EPS)
    y = yg.reshape(bsz, L, D_INNER) * ssm_norm.astype(jnp.float32)
    return y.astype(z.dtype), new_conv_state, h_T.astype(z.dtype)


def hybrid_layer(x, win_bufs, conv_state, ssm_state, lw, biases):
    (w_in, conv_w, conv_b, dt_bias, a_log, d_skip, ssm_norm, w_ssm_proj, w_att_proj,
     w_out, norm_mix, w_up, w_down, norm_mlp) = lw
    h = rmsnorm(x, norm_mix)
    offsets = np.cumsum(IN_SPLITS)[:-1].tolist()
    z, xbc, dt_raw, q, k, v, g_ssm, g_att = jnp.split(h @ w_in, offsets, axis=-1)
    y_ssm, new_conv, new_ssm = ssd_branch(z, xbc, dt_raw, conv_state, ssm_state, conv_w, conv_b,
                                          dt_bias, a_log, d_skip, ssm_norm)
    o_att, new_rows = attention_branch(q, k, v, win_bufs, biases)
    merged = (jax.nn.sigmoid(g_ssm) * (y_ssm @ w_ssm_proj)
              + jax.nn.sigmoid(g_att) * (o_att.astype(x.dtype) @ w_att_proj))
    x = x + merged @ w_out
    u = jax.nn.relu(rmsnorm(x, norm_mlp) @ w_up)
    x = x + (u * u) @ w_down
    return x, new_rows, new_conv, new_ssm


def setup_inputs(seed: int = 0) -> dict:
    key = jax.random.key(seed)
    ks = iter(jax.random.split(key, 32))
    def nrm(shape, scale):
        return jax.random.normal(next(ks), shape, jnp.float32) * scale
    def win_shape(w):
        return (DEPTH, DEC_BATCH, min(w, PAST_LEN), 2, ATT_HEADS_PER_GROUP, ATT_HEAD_DIM)
    dt0 = jnp.exp(jax.random.uniform(next(ks), (DEPTH, SSM_HEADS), jnp.float32, np.log(1e-3), np.log(1e-1)))
    return {
        'x_prompt': nrm((BATCH, SEQ, D_MODEL), 1.0),
        'x_sample': nrm((DEC_BATCH, DEC_SEQ, D_MODEL), 1.0),
        'cache_win128': nrm(win_shape(ATT_PATTERNS[0][0]), 1.0),
        'cache_win512': nrm(win_shape(ATT_PATTERNS[1][0]), 1.0),
        'cache_win2048': nrm(win_shape(ATT_PATTERNS[2][0]), 1.0),
        'state_conv': nrm((DEPTH, DEC_BATCH, CONV_WIDTH - 1, CONV_DIM), 1.0),
        'state_ssm': nrm((DEPTH, DEC_BATCH, SSM_HEADS, SSM_HEAD_DIM, SSM_STATE), 0.1),
        'w_in': nrm((DEPTH, D_MODEL, IN_TOTAL), D_MODEL ** -0.5),
        'conv_w': nrm((DEPTH, CONV_WIDTH, CONV_DIM), CONV_WIDTH ** -0.5),
        'conv_b': nrm((DEPTH, CONV_DIM), 0.01),
        'dt_bias': dt0 + jnp.log(-jnp.expm1(-dt0)),
        'a_log': jnp.log(jax.random.uniform(next(ks), (DEPTH, SSM_HEADS), jnp.float32, 1.0, 16.0)),
        'd_skip': 1.0 + nrm((DEPTH, SSM_HEADS), 0.01),
        'ssm_norm': 1.0 + nrm((DEPTH, D_INNER), 0.01),
        'w_ssm_proj': nrm((DEPTH, D_INNER, D_MODEL), D_INNER ** -0.5),
        'w_att_proj': nrm((DEPTH, ATT_OUT_WIDTH, D_MODEL), ATT_OUT_WIDTH ** -0.5),
        'w_out': nrm((DEPTH, D_MODEL, D_MODEL), D_MODEL ** -0.5),
        'norm_mix': 1.0 + nrm((DEPTH, D_MODEL), 0.01),
        'w_up': nrm((DEPTH, D_MODEL, D_FF), D_MODEL ** -0.5),
        'w_down': nrm((DEPTH, D_FF, D_MODEL), D_FF ** -0.5),
        'norm_mlp': 1.0 + nrm((DEPTH, D_MODEL), 0.01),
        'rel_bias': nrm((N_BUCKETS, ATT_HEADS), 0.5),
        'norm_final': 1.0 + nrm((D_MODEL,), 0.01),
    }


def reference(x_prompt, x_sample, cache_win128, cache_win512, cache_win2048, state_conv, state_ssm,
              w_in, conv_w, conv_b, dt_bias, a_log, d_skip, ssm_norm, w_ssm_proj, w_att_proj,
              w_out, norm_mix, w_up, w_down, norm_mlp, rel_bias, norm_final):
    biases = group_rel_biases(rel_bias)
    bp = x_prompt.shape[0]
    xp, xs = x_prompt, x_sample
    p_win, s_win = ([], [], []), ([], [], [])
    p_conv, p_ssm, s_conv, s_ssm = [], [], [], []
    for l in range(DEPTH):
        lw = (w_in[l], conv_w[l], conv_b[l], dt_bias[l], a_log[l], d_skip[l], ssm_norm[l],
              w_ssm_proj[l], w_att_proj[l], w_out[l], norm_mix[l], w_up[l], w_down[l], norm_mlp[l])
        conv0 = jnp.zeros((bp, CONV_WIDTH - 1, CONV_DIM), x_prompt.dtype)
        ssm0 = jnp.zeros((bp, SSM_HEADS, SSM_HEAD_DIM, SSM_STATE), jnp.float32)
        xp, rows_p, conv_p, ssm_p = hybrid_layer(xp, None, conv0, ssm0, lw, biases)
        bufs = (cache_win128[l], cache_win512[l], cache_win2048[l])
        xs, rows_s, conv_s, ssm_s = hybrid_layer(xs, bufs, state_conv[l], state_ssm[l], lw, biases)
        for g in range(ATT_N_GROUPS):
            p_win[g].append(rows_p[g])
            s_win[g].append(rows_s[g])
        p_conv.append(conv_p)
        p_ssm.append(ssm_p)
        s_conv.append(conv_s)
        s_ssm.append(ssm_s)
    y_prompt = rmsnorm(xp, norm_final)
    y_sample = rmsnorm(xs, norm_final)
    return (y_prompt, y_sample,
            jnp.stack(p_win[0]), jnp.stack(p_win[1]), jnp.stack(p_win[2]), jnp.stack(p_conv), jnp.stack(p_ssm),
            jnp.stack(s_win[0]), jnp.stack(s_win[1]), jnp.stack(s_win[2]), jnp.stack(s_conv), jnp.stack(s_ssm))
```

```python
import functools

import numpy as np
import jax
import jax.numpy as jnp
from jax import lax
from jax.experimental import pallas as pl
from jax.experimental.pallas import tpu as pltpu

D_MODEL = 2048
D_INNER = 2 * D_MODEL
SSM_HEAD_DIM = 64
SSM_HEADS = D_INNER // SSM_HEAD_DIM
SSM_GROUPS = 8
HEADS_PER_SSM_GROUP = SSM_HEADS // SSM_GROUPS
SSM_STATE = 128
SSM_GROUP_WIDTH = D_INNER // SSM_GROUPS
CONV_WIDTH = 4
CONV_DIM = D_INNER + 2 * SSM_GROUPS * SSM_STATE
SSD_CHUNK = 128
ATT_HEAD_DIM = 128
ATT_HEADS_PER_GROUP = 8
ATT_PATTERNS = ((128, 1), (512, 4), (2048, 16))
ATT_N_GROUPS = len(ATT_PATTERNS)
ATT_GROUP_WIDTH = ATT_HEADS_PER_GROUP * ATT_HEAD_DIM
ATT_WIDTH = ATT_N_GROUPS * ATT_GROUP_WIDTH
N_BUCKETS = 32
MAX_DISTANCE = max(w for w, _ in ATT_PATTERNS)
D_FF = 4 * D_MODEL
EPS = 1e-6
PAST_LEN = 2048

LANES = 128
SUBLANES = 8
VMEM_LIMIT_BYTES = 56 * 1024 * 1024

F32 = jnp.float32
BF16 = jnp.bfloat16
NEG_INF = float("-inf")


def _params(n_grid_dims):
    return pltpu.CompilerParams(
        dimension_semantics=("arbitrary",) * n_grid_dims,
        vmem_limit_bytes=VMEM_LIMIT_BYTES)


def _sigmoid(x):
    return 1.0 / (1.0 + jnp.exp(-x))


def _silu(x):
    return x * _sigmoid(x)


def _softplus(x):
    return jnp.maximum(x, 0.0) + jnp.log1p(jnp.exp(-jnp.abs(x)))


def _dot_nt(a, b):
    return lax.dot_general(a, b, (((1,), (1,)), ((), ())), preferred_element_type=F32)


def _dot_tn(a, b):
    return lax.dot_general(a, b, (((0,), (0,)), ((), ())), preferred_element_type=F32)


def _rmsnorm_kernel(x_ref, g_ref, o_ref):
    x = x_ref[...]
    ms = jnp.mean(x * x, axis=-1, keepdims=True)
    o_ref[...] = (x * lax.rsqrt(ms + EPS) * g_ref[...]).astype(o_ref.dtype)


def rmsnorm(x, g, out_dtype, tm=512):
    m, d = x.shape
    return pl.pallas_call(
        _rmsnorm_kernel,
        grid=(m // tm,),
        in_specs=[pl.BlockSpec((tm, d), lambda i: (i, 0)),
                  pl.BlockSpec((1, d), lambda i: (0, 0))],
        out_specs=pl.BlockSpec((tm, d), lambda i: (i, 0)),
        out_shape=jax.ShapeDtypeStruct((m, d), out_dtype),
        compiler_params=_params(1),
        name="rmsnorm",
    )(x, g.reshape(1, d))


def _mm_kernel(a_ref, w_ref, *rest, epilogue):
    acc = jnp.dot(a_ref[...], w_ref[...], preferred_element_type=F32)
    if epilogue == "residual":
        r_ref, o_ref = rest
        o_ref[...] = r_ref[...] + acc
    elif epilogue == "relu2":
        (o_ref,) = rest
        u = jnp.maximum(acc, 0.0)
        o_ref[...] = (u * u).astype(o_ref.dtype)
    else:
        (o_ref,) = rest
        o_ref[...] = acc.astype(o_ref.dtype)


def matmul(a, w, *, tm, tn, out_dtype=F32, epilogue="none", residual=None, name="matmul"):
    m, k = a.shape
    n = w.shape[1]
    tm = min(tm, m)
    tn = min(tn, n)
    in_specs = [pl.BlockSpec((tm, k), lambda i, j: (i, 0)),
                pl.BlockSpec((k, tn), lambda i, j: (0, j))]
    args = [a, w]
    if epilogue == "residual":
        in_specs.append(pl.BlockSpec((tm, tn), lambda i, j: (i, j)))
        args.append(residual)
    return pl.pallas_call(
        functools.partial(_mm_kernel, epilogue=epilogue),
        grid=(m // tm, n // tn),
        in_specs=in_specs,
        out_specs=pl.BlockSpec((tm, tn), lambda i, j: (i, j)),
        out_shape=jax.ShapeDtypeStruct((m, n), out_dtype),
        compiler_params=_params(2),
        name=name,
    )(*args)


def _merge_kernel(y_ref, o_ref, gs_ref, ga_ref, ws_ref, wa_ref, out_ref):
    ps = jnp.dot(y_ref[...], ws_ref[...], preferred_element_type=F32)
    pa = jnp.dot(o_ref[...], wa_ref[...], preferred_element_type=F32)
    out_ref[...] = (_sigmoid(gs_ref[...]) * ps + _sigmoid(ga_ref[...]) * pa).astype(out_ref.dtype)


def merge_proj(y_ssm, o_att, gates, w_ssm, w_att, *, tm, tn):
    m = y_ssm.shape[0]
    n = w_ssm.shape[1]
    tm = min(tm, m)
    n_col = n // tn
    return pl.pallas_call(
        _merge_kernel,
        grid=(m // tm, n_col),
        in_specs=[pl.BlockSpec((tm, y_ssm.shape[1]), lambda i, j: (i, 0)),
                  pl.BlockSpec((tm, o_att.shape[1]), lambda i, j: (i, 0)),
                  pl.BlockSpec((tm, tn), lambda i, j: (i, j)),
                  pl.BlockSpec((tm, tn), lambda i, j: (i, j + n_col)),
                  pl.BlockSpec((w_ssm.shape[0], tn), lambda i, j: (0, j)),
                  pl.BlockSpec((w_att.shape[0], tn), lambda i, j: (0, j))],
        out_specs=pl.BlockSpec((tm, tn), lambda i, j: (i, j)),
        out_shape=jax.ShapeDtypeStruct((m, n), BF16),
        compiler_params=_params(2),
        name="merge_proj",
    )(y_ssm, o_att, gates, gates, w_ssm, w_att)


def _conv_prompt_kernel(x_ref, w_ref, b_ref, o_ref, pad_ref, *, seq):
    pad_ref[0:SUBLANES, :] = jnp.zeros((SUBLANES, pad_ref.shape[1]), F32)
    pad_ref[SUBLANES:SUBLANES + seq, :] = x_ref[...]
    acc = b_ref[...] + pad_ref[SUBLANES - (CONV_WIDTH - 1):SUBLANES - (CONV_WIDTH - 1) + seq, :] * w_ref[0:1, :]
    for i in range(1, CONV_WIDTH):
        start = SUBLANES - (CONV_WIDTH - 1) + i
        acc = acc + pad_ref[start:start + seq, :] * w_ref[i:i + 1, :]
    o_ref[...] = _silu(acc)


def conv_prompt(xbc, conv_w, conv_b, *, batch, seq, tc=512):
    m, c = xbc.shape
    return pl.pallas_call(
        functools.partial(_conv_prompt_kernel, seq=seq),
        grid=(batch, c // tc),
        in_specs=[pl.BlockSpec((seq, tc), lambda b, j: (b, j)),
                  pl.BlockSpec((CONV_WIDTH, tc), lambda b, j: (0, j)),
                  pl.BlockSpec((1, tc), lambda b, j: (0, j))],
        out_specs=pl.BlockSpec((seq, tc), lambda b, j: (b, j)),
        out_shape=jax.ShapeDtypeStruct((m, c), F32),
        scratch_shapes=[pltpu.VMEM((seq + SUBLANES, tc), F32)],
        compiler_params=_params(2),
        name="conv_prompt",
    )(xbc, conv_w, conv_b.reshape(1, c))


def _conv_sample_kernel(x0_ref, x1_ref, x2_ref, x3_ref, w_ref, b_ref, o_ref):
    acc = b_ref[...] + x0_ref[...] * w_ref[0:1, :]
    for i, x_ref in enumerate((x1_ref, x2_ref, x3_ref), start=1):
        acc = acc + x_ref[...] * w_ref[i:i + 1, :]
    o_ref[...] = _silu(acc)


def conv_sample(xpad, conv_w, conv_b, *, steps, tc=512):
    bsz = xpad.shape[0]
    c = conv_w.shape[1]
    nct = c // tc
    x_specs = [pl.BlockSpec((bsz, tc), functools.partial(lambda t, j, i: (0, (t + i) * nct + j), i=i))
               for i in range(CONV_WIDTH)]
    return pl.pallas_call(
        _conv_sample_kernel,
        grid=(steps, nct),
        in_specs=x_specs + [pl.BlockSpec((CONV_WIDTH, tc), lambda t, j: (0, j)),
                            pl.BlockSpec((1, tc), lambda t, j: (0, j))],
        out_specs=pl.BlockSpec((bsz, tc), lambda t, j: (0, t * nct + j)),
        out_shape=jax.ShapeDtypeStruct((bsz, steps * c), F32),
        compiler_params=_params(2),
        name="conv_sample",
    )(xpad, xpad, xpad, xpad, conv_w, conv_b.reshape(1, c))


def _ssd_kernel(*refs, q_len, valid_len, has_h0):
    if has_h0:
        (x_ref, b_ref, c_ref, dtc_ref, dtr_ref, z_ref, pc_ref, pr_ref, nrm_ref, h0_ref,
         y_ref, h_ref) = refs
    else:
        (x_ref, b_ref, c_ref, dtc_ref, dtr_ref, z_ref, pc_ref, pr_ref, nrm_ref,
         y_ref, h_ref) = refs
    nh, hd = HEADS_PER_SSM_GROUP, SSM_HEAD_DIM
    pair_w = 2 * hd

    @pl.when(pl.program_id(2) == 0)
    def _():
        if has_h0:
            h_ref[...] = h0_ref[...]
        else:
            h_ref[...] = jnp.zeros(h_ref.shape, F32)

    pc = pc_ref[...]
    pr = pr_ref[...]
    dt_c = _softplus(dtc_ref[...] + pc[0:1, :])
    dt_r = _softplus(dtr_ref[...] + pr[:, 0:1])
    if valid_len < q_len:
        dt_c = jnp.where(lax.broadcasted_iota(jnp.int32, dt_c.shape, 0) < valid_len, dt_c, 0.0)
        dt_r = jnp.where(lax.broadcasted_iota(jnp.int32, dt_r.shape, 1) < valid_len, dt_r, 0.0)
    a_c = dt_c * (-jnp.exp(pc[1:2, :]))
    a_r = dt_r * (-jnp.exp(pr[:, 1:2]))
    ti = lax.broadcasted_iota(jnp.int32, (q_len, q_len), 0)
    tj = lax.broadcasted_iota(jnp.int32, (q_len, q_len), 1)
    causal = ti >= tj
    acum_c = jnp.dot(causal.astype(F32), a_c, precision=lax.Precision.HIGHEST,
                     preferred_element_type=F32)
    acum_r = jnp.dot(a_r, (ti <= tj).astype(F32), precision=lax.Precision.HIGHEST,
                     preferred_element_type=F32)
    end_c = acum_c[q_len - 1:q_len, :]
    dd_c = jnp.exp(end_c - acum_c) * dt_c

    x = x_ref[...]
    bmat = b_ref[...].astype(BF16)
    cmat = c_ref[...].astype(BF16)
    cb = _dot_nt(cmat, bmat)
    h_prev = h_ref[...]
    y_off_all = _dot_nt(cmat, h_prev.reshape(nh * hd, SSM_STATE).astype(BF16))

    lane = lax.broadcasted_iota(jnp.int32, (q_len, pair_w), 1)
    first_half = lane < hd

    def pair_bcast(cols, j0):
        return jnp.where(first_half,
                         jnp.broadcast_to(cols[:, j0:j0 + 1], (q_len, pair_w)),
                         jnp.broadcast_to(cols[:, j0 + 1:j0 + 2], (q_len, pair_w)))

    ys = []
    sumsq = jnp.zeros((q_len, 1), F32)
    for i in range(nh // 2):
        j0 = 2 * i
        x_pair = x[:, i * pair_w:(i + 1) * pair_w]
        y_diag = jnp.zeros((q_len, pair_w), F32)
        for j in (j0, j0 + 1):
            seg = jnp.broadcast_to(acum_c[:, j:j + 1], (q_len, q_len)) - acum_r[j:j + 1, :]
            decay = jnp.exp(jnp.where(causal, seg, NEG_INF))
            w = (cb * decay * dt_r[j:j + 1, :]).astype(BF16)
            own = first_half if j == j0 else jnp.logical_not(first_half)
            y_diag = y_diag + jnp.dot(w, jnp.where(own, x_pair, 0.0).astype(BF16),
                                      preferred_element_type=F32)
        e_pair = jnp.exp(pair_bcast(acum_c, j0))
        d_pair = jnp.where(first_half[0:1, :], pc[2:3, j0:j0 + 1], pc[2:3, j0 + 1:j0 + 2])
        y = y_diag + y_off_all[:, i * pair_w:(i + 1) * pair_w] * e_pair + d_pair * x_pair
        y = y * _silu(z_ref[:, i * pair_w:(i + 1) * pair_w])
        sumsq = sumsq + jnp.sum(y * y, axis=-1, keepdims=True)
        ys.append(y)

        xw = (x_pair * pair_bcast(dd_c, j0)).astype(BF16)
        s_pair = _dot_tn(xw, bmat)
        for jj, j in enumerate((j0, j0 + 1)):
            chunk_decay = jnp.exp(acum_r[j:j + 1, q_len - 1:q_len])
            h_ref[j] = h_prev[j] * chunk_decay + s_pair[jj * hd:(jj + 1) * hd, :]

    inv = lax.rsqrt(sumsq * (1.0 / (nh * hd)) + EPS)
    for i in range(nh // 2):
        sl = slice(i * pair_w, (i + 1) * pair_w)
        y_ref[:, sl] = (ys[i] * inv * nrm_ref[:, sl]).astype(y_ref.dtype)


def ssd(u, dt_raw, z, dt_bias, a_log, d_skip, ssm_norm, h0, *, batch, seq, q_len, valid_len):
    nh, g = HEADS_PER_SSM_GROUP, SSM_GROUPS
    nc = seq // q_len
    m = batch * seq
    dt4 = dt_raw[:, :SSM_HEADS].reshape(batch * nc, q_len, g, nh)
    dtc = jnp.transpose(dt4, (0, 2, 1, 3))
    dtr = jnp.transpose(dt4, (0, 2, 3, 1))
    pcol = jnp.stack([dt_bias, a_log, d_skip]).reshape(3, g, nh).transpose(1, 0, 2)
    prow = jnp.transpose(pcol, (0, 2, 1))
    xw_blocks = SSM_GROUP_WIDTH // SSM_GROUP_WIDTH
    b_off = D_INNER // SSM_STATE
    c_off = b_off + g
    in_specs = [
        pl.BlockSpec((q_len, SSM_GROUP_WIDTH), lambda b, gi, c: (b * nc + c, gi * xw_blocks)),
        pl.BlockSpec((q_len, SSM_STATE), lambda b, gi, c: (b * nc + c, b_off + gi)),
        pl.BlockSpec((q_len, SSM_STATE), lambda b, gi, c: (b * nc + c, c_off + gi)),
        pl.BlockSpec((None, None, q_len, nh), lambda b, gi, c: (b * nc + c, gi, 0, 0)),
        pl.BlockSpec((None, None, nh, q_len), lambda b, gi, c: (b * nc + c, gi, 0, 0)),
        pl.BlockSpec((q_len, SSM_GROUP_WIDTH), lambda b, gi, c: (b * nc + c, gi)),
        pl.BlockSpec((None, 3, nh), lambda b, gi, c: (gi, 0, 0)),
        pl.BlockSpec((None, nh, 3), lambda b, gi, c: (gi, 0, 0)),
        pl.BlockSpec((1, SSM_GROUP_WIDTH), lambda b, gi, c: (0, gi)),
    ]
    args = [u, u, u, dtc, dtr, z, pcol, prow, ssm_norm.reshape(1, D_INNER)]
    h_spec = pl.BlockSpec((None, nh, SSM_HEAD_DIM, SSM_STATE), lambda b, gi, c: (b, gi, 0, 0))
    if h0 is not None:
        in_specs.append(h_spec)
        args.append(h0)
    return pl.pallas_call(
        functools.partial(_ssd_kernel, q_len=q_len, valid_len=valid_len, has_h0=h0 is not None),
        grid=(batch, g, nc),
        in_specs=in_specs,
        out_specs=[pl.BlockSpec((q_len, SSM_GROUP_WIDTH), lambda b, gi, c: (b * nc + c, gi)), h_spec],
        out_shape=[jax.ShapeDtypeStruct((m, D_INNER), BF16),
                   jax.ShapeDtypeStruct((batch, SSM_HEADS, SSM_HEAD_DIM, SSM_STATE), F32)],
        compiler_params=_params(3),
        name="ssd",
    )(*args)


def _attn_prompt_kernel(*refs, kb, has_prev):
    if has_prev:
        q_ref, kp_ref, kc_ref, vp_ref, vc_ref, mb_ref, o_ref, l_ref = refs
    else:
        q_ref, kc_ref, vc_ref, mb_ref, o_ref, l_ref = refs
    not_first = pl.program_id(2) > 0
    scale = ATT_HEAD_DIM ** -0.5
    for h in range(ATT_HEADS_PER_GROUP):
        sl = slice(h * ATT_HEAD_DIM, (h + 1) * ATT_HEAD_DIM)
        q = (q_ref[:, sl] * scale).astype(BF16)
        s_c = _dot_nt(q, kc_ref[:, sl].astype(BF16)) + mb_ref[h, :, kb:]
        m = jnp.max(s_c, axis=-1, keepdims=True)
        if has_prev:
            s_p = _dot_nt(q, kp_ref[:, sl].astype(BF16)) + mb_ref[h, :, :kb]
            s_p = jnp.where(not_first, s_p, NEG_INF)
            m = jnp.maximum(m, jnp.max(s_p, axis=-1, keepdims=True))
        p_c = jnp.exp(s_c - m)
        den = jnp.sum(p_c, axis=-1, keepdims=True)
        o = jnp.dot(p_c.astype(BF16), vc_ref[:, sl].astype(BF16), preferred_element_type=F32)
        if has_prev:
            p_p = jnp.exp(s_p - m)
            den = den + jnp.sum(p_p, axis=-1, keepdims=True)
            o = o + jnp.dot(p_p.astype(BF16), vp_ref[:, sl].astype(BF16), preferred_element_type=F32)
        o_ref[:, sl] = o / den
        l_ref[:, sl] = jnp.broadcast_to(m + jnp.log(den), (kb, ATT_HEAD_DIM))


def attn_prompt(q, kv, mbias, *, group, batch, seq, dil, kb):
    m_res = seq // dil
    nb = m_res // kb
    has_prev = nb > 1
    gw = ATT_GROUP_WIDTH
    q2 = q.reshape(batch * m_res, dil * ATT_WIDTH)
    kv2 = kv.reshape(batch * m_res, dil * 2 * gw)
    q_blocks, kv_blocks = ATT_WIDTH // gw, 2

    def cur(col, blocks):
        return pl.BlockSpec((kb, gw), lambda b, r, n: (b * nb + n, r * blocks + col))

    def prev(col, blocks):
        return pl.BlockSpec((kb, gw), lambda b, r, n: (b * nb + jnp.maximum(n - 1, 0), r * blocks + col))

    if has_prev:
        in_specs = [cur(group, q_blocks), prev(0, kv_blocks), cur(0, kv_blocks),
                    prev(1, kv_blocks), cur(1, kv_blocks)]
        args = [q2, kv2, kv2, kv2, kv2]
    else:
        in_specs = [cur(group, q_blocks), cur(0, kv_blocks), cur(1, kv_blocks)]
        args = [q2, kv2, kv2]
    in_specs.append(pl.BlockSpec(mbias.shape, lambda b, r, n: (0, 0, 0)))
    args.append(mbias)
    out_spec = cur(0, 1)
    o, lse = pl.pallas_call(
        functools.partial(_attn_prompt_kernel, kb=kb, has_prev=has_prev),
        grid=(batch, dil, nb),
        in_specs=in_specs,
        out_specs=[out_spec, out_spec],
        out_shape=[jax.ShapeDtypeStruct((batch * m_res, dil * gw), F32)] * 2,
        compiler_params=_params(3),
        name=f"attn_prompt_g{group}",
    )(*args)
    return o.reshape(batch * seq, gw), lse.reshape(batch * seq, gw)


def _attn_sample_kernel(*refs, n_cache, steps, q_rows):
    q_ref, new_ref = refs[0], refs[1]
    cache_refs = refs[2:2 + n_cache]
    bias_ref = refs[2 + n_cache]
    o_ref, l_ref = refs[3 + n_cache:]
    gw = ATT_GROUP_WIDTH
    nh = ATT_HEADS_PER_GROUP
    scale = ATT_HEAD_DIM ** -0.5
    per_cache = steps // n_cache
    sub = lax.broadcasted_iota(jnp.int32, (q_rows, gw), 0)
    head_of_lane = lax.broadcasted_iota(jnp.int32, (q_rows, gw), 1) // ATT_HEAD_DIM
    new_k = new_ref[:, :gw].astype(BF16)
    new_v = new_ref[:, gw:].astype(BF16)
    for ci in range(n_cache):
        qbd = jnp.zeros((q_rows, gw), F32)
        for tt in range(per_cache):
            t = ci * per_cache + tt
            q_row = jnp.broadcast_to(q_ref[t:t + 1, :], (q_rows, gw))
            qbd = jnp.where(sub - tt * nh == head_of_lane, q_row, qbd)
        qbd = (qbd * scale).astype(BF16)
        k_all = jnp.concatenate([cache_refs[ci][:, :gw].astype(BF16), new_k], axis=0)
        v_all = jnp.concatenate([cache_refs[ci][:, gw:].astype(BF16), new_v], axis=0)
        s = _dot_nt(qbd, k_all) + bias_ref[ci]
        m = jnp.max(s, axis=-1, keepdims=True)
        p = jnp.exp(s - m)
        den = jnp.sum(p, axis=-1, keepdims=True)
        o_all = jnp.dot(p.astype(BF16), v_all, preferred_element_type=F32) / den
        lse = jnp.broadcast_to(m + jnp.log(den), (q_rows, gw))
        for tt in range(per_cache):
            t = ci * per_cache + tt
            own = sub - tt * nh == head_of_lane
            o_ref[t:t + 1, :] = jnp.sum(jnp.where(own, o_all, 0.0), axis=0, keepdims=True)
            l_ref[t:t + 1, :] = jnp.sum(jnp.where(own, lse, 0.0), axis=0, keepdims=True)


def attn_sample(q, kv_new_pad, cache, bias, *, group, batch, steps, dil):
    gw = ATT_GROUP_WIDTH
    n_cache, q_rows, n_keys_tot = bias.shape
    new_rows = kv_new_pad.shape[1]
    rows_per_cache = n_keys_tot - new_rows
    q3 = q.reshape(batch, steps, ATT_WIDTH)
    if n_cache == 1:
        cache_view = cache
        cache_specs = [pl.BlockSpec((None, rows_per_cache, 2 * gw), lambda b: (b, 0, 0))]
    else:
        cache_view = cache.reshape(batch, rows_per_cache, dil * 2 * gw)
        cache_specs = [pl.BlockSpec((None, rows_per_cache, 2 * gw),
                                    functools.partial(lambda b, t: (b, 0, t), t=t))
                       for t in range(n_cache)]
    out_spec = pl.BlockSpec((None, steps, gw), lambda b: (b, 0, 0))
    o, lse = pl.pallas_call(
        functools.partial(_attn_sample_kernel, n_cache=n_cache, steps=steps, q_rows=q_rows),
        grid=(batch,),
        in_specs=[pl.BlockSpec((None, steps, gw), lambda b: (b, 0, group)),
                  pl.BlockSpec((None, new_rows, 2 * gw), lambda b: (b, 0, 0))]
                 + cache_specs
                 + [pl.BlockSpec(bias.shape, lambda b: (0, 0, 0))],
        out_specs=[out_spec, out_spec],
        out_shape=[jax.ShapeDtypeStruct((batch, steps, gw), F32)] * 2,
        compiler_params=_params(1),
        name=f"attn_sample_g{group}",
    )(q3, kv_new_pad, *([cache_view] * n_cache), bias)
    return o.reshape(batch * steps, gw), lse.reshape(batch * steps, gw)


def _attn_combine_kernel(o0, o1, o2, l0, l1, l2, out_ref):
    ls = (l0[...], l1[...], l2[...])
    m = jnp.maximum(jnp.maximum(ls[0], ls[1]), ls[2])
    ws = [jnp.exp(l - m) for l in ls]
    num = ws[0] * o0[...] + ws[1] * o1[...] + ws[2] * o2[...]
    out_ref[...] = (num / (ws[0] + ws[1] + ws[2])).astype(out_ref.dtype)


def attn_combine(outs, lses, tm=512):
    m, w = outs[0].shape
    tm = min(tm, m)
    spec = pl.BlockSpec((tm, w), lambda i: (i, 0))
    return pl.pallas_call(
        _attn_combine_kernel,
        grid=(m // tm,),
        in_specs=[spec] * 6,
        out_specs=spec,
        out_shape=jax.ShapeDtypeStruct((m, w), BF16),
        compiler_params=_params(1),
        name="attn_combine",
    )(*outs, *lses)


def _t5_bucket(dist):
    max_exact = N_BUCKETS // 2
    d = np.maximum(dist, max_exact).astype(np.float32)
    large = max_exact + (np.log(d / max_exact) / np.log(MAX_DISTANCE / max_exact)
                         * (N_BUCKETS - max_exact)).astype(np.int32)
    large = np.minimum(large, N_BUCKETS - 1)
    return np.where(dist < max_exact, dist, large).astype(np.int32)


def _group_bias(rel_bias, g):
    win, dil = ATT_PATTERNS[g]
    buckets = _t5_bucket(np.arange(win // dil + 1) * dil)
    return rel_bias[buckets][:, g * ATT_HEADS_PER_GROUP:(g + 1) * ATT_HEADS_PER_GROUP]


def _prompt_mask_bias(bias_g, n_keys, kb):
    rel = np.arange(kb)[:, None] + kb - np.arange(2 * kb)[None, :]
    valid = (rel >= 0) & (rel <= n_keys)
    tab = jnp.transpose(bias_g[np.clip(rel, 0, n_keys)], (2, 0, 1)).astype(F32)
    return jnp.where(valid[None], tab, NEG_INF)


def _sample_mask_bias(bias_g, *, cache_len, dil, n_keys, steps, new_rows, shared_cache, q_rows):
    nh = ATT_HEADS_PER_GROUP
    n_cache = 1 if shared_cache else steps
    per_cache = steps // n_cache
    rows_per_cache = cache_len if shared_cache else cache_len // dil
    blocks = []
    for ci in range(n_cache):
        if shared_cache:
            cache_pos = np.arange(rows_per_cache)
        else:
            cache_pos = ci + dil * np.arange(rows_per_cache)
        key_pos = np.concatenate([cache_pos, cache_len + np.arange(new_rows)])
        key_live = np.concatenate([np.ones(rows_per_cache, bool), np.arange(new_rows) < steps])
        rows = []
        for tt in range(per_cache):
            t = ci * per_cache + tt
            diff = cache_len + t - key_pos
            valid = key_live & (diff >= 0) & (diff % dil == 0) & (diff // dil <= n_keys)
            j = np.clip(diff // dil, 0, n_keys)
            rows.append(jnp.where(valid[None, :], bias_g[j].T.astype(F32), NEG_INF))
        blk = jnp.concatenate(rows, axis=0)
        pad = q_rows - per_cache * nh
        if pad:
            blk = jnp.concatenate([blk, jnp.zeros((pad, blk.shape[1]), F32)], axis=0)
        blocks.append(blk)
    return jnp.stack(blocks)


SAMPLE_NEW_ROWS = 16
SAMPLE_Q_ROWS_MIN = 16
SSD_SAMPLE_Q = 16


def _split_w_in(w_in):
    sizes = (D_INNER, CONV_DIM, SSM_HEADS, ATT_WIDTH, ATT_WIDTH, ATT_WIDTH, D_MODEL, D_MODEL)
    offs = np.concatenate([[0], np.cumsum(sizes)])
    seg = lambda i: w_in[:, offs[i]:offs[i + 1]]
    gw = ATT_GROUP_WIDTH
    w_kv = [jnp.concatenate([seg(4)[:, g * gw:(g + 1) * gw], seg(5)[:, g * gw:(g + 1) * gw]], axis=1).astype(BF16)
            for g in range(ATT_N_GROUPS)]
    w_dt = jnp.pad(seg(2), ((0, 0), (0, LANES - SSM_HEADS))).astype(BF16)
    w_gates = jnp.concatenate([seg(6), seg(7)], axis=1).astype(BF16)
    return dict(z=seg(0).astype(BF16), xbc=seg(1).astype(BF16), dt=w_dt, q=seg(3).astype(BF16),
                kv=w_kv, gates=w_gates)


def _layer(x, w, *, batch, seq, is_prompt, caches, conv_state, ssm_state, rel_bias):
    m = batch * seq
    tm = 1024 if m % 1024 == 0 else 512
    h = rmsnorm(x, w["norm_mix"], BF16)
    proj = lambda wseg, name: matmul(h, wseg, tm=tm, tn=1024, name=name)
    z = proj(w["in"]["z"], "in_z")
    xbc = proj(w["in"]["xbc"], "in_xbc")
    dt_raw = proj(w["in"]["dt"], "in_dt")
    q = proj(w["in"]["q"], "in_q")
    kvs = [proj(w["in"]["kv"][g], f"in_kv{g}") for g in range(ATT_N_GROUPS)]
    gates = proj(w["in"]["gates"], "in_gates")

    if is_prompt:
        u = conv_prompt(xbc, w["conv_w"], w["conv_b"], batch=batch, seq=seq)
        new_conv = xbc.reshape(batch, seq, CONV_DIM)[:, seq - (CONV_WIDTH - 1):]
        y_ssm, new_ssm = ssd(u, dt_raw, z, w["dt_bias"], w["a_log"], w["d_skip"], w["ssm_norm"], None,
                             batch=batch, seq=seq, q_len=SSD_CHUNK, valid_len=SSD_CHUNK)
    else:
        xpad = jnp.concatenate([conv_state.reshape(batch, (CONV_WIDTH - 1) * CONV_DIM),
                                xbc.reshape(batch, seq * CONV_DIM)], axis=1)
        u = conv_sample(xpad, w["conv_w"], w["conv_b"], steps=seq)
        new_conv = xpad.reshape(batch, CONV_WIDTH - 1 + seq, CONV_DIM)[:, seq:]
        qp = SSD_SAMPLE_Q
        pad_rows = lambda a: jnp.pad(a.reshape(batch, seq, a.shape[-1]),
                                     ((0, 0), (0, qp - seq), (0, 0))).reshape(batch * qp, a.shape[-1])
        y_pad, new_ssm = ssd(pad_rows(u.reshape(m, CONV_DIM)), pad_rows(dt_raw), pad_rows(z),
                             w["dt_bias"], w["a_log"], w["d_skip"], w["ssm_norm"], ssm_state,
                             batch=batch, seq=qp, q_len=qp, valid_len=seq)
        y_ssm = y_pad.reshape(batch, qp, D_INNER)[:, :seq].reshape(m, D_INNER)

    outs, lses, new_rows = [], [], []
    for g, (win, dil) in enumerate(ATT_PATTERNS):
        n_keys = win // dil
        bias_g = _group_bias(rel_bias, g)
        if is_prompt:
            mb = _prompt_mask_bias(bias_g, n_keys, n_keys)
            o, lse = attn_prompt(q, kvs[g], mb, group=g, batch=batch, seq=seq, dil=dil, kb=n_keys)
            keep = min(win, seq)
            rows = kvs[g].reshape(batch, seq, 2, ATT_HEADS_PER_GROUP, ATT_HEAD_DIM)[:, seq - keep:]
        else:
            cache = caches[g]
            cache_len = cache.shape[1]
            shared = dil == 1
            q_rows = max(SAMPLE_Q_ROWS_MIN, (seq if shared else 1) * ATT_HEADS_PER_GROUP)
            sb = _sample_mask_bias(bias_g, cache_len=cache_len, dil=dil, n_keys=n_keys, steps=seq,
                                   new_rows=SAMPLE_NEW_ROWS, shared_cache=shared, q_rows=q_rows)
            kv_new = jnp.pad(kvs[g].reshape(batch, seq, 2 * ATT_GROUP_WIDTH),
                             ((0, 0), (0, SAMPLE_NEW_ROWS - seq), (0, 0)))
            o, lse = attn_sample(q, kv_new, cache.reshape(batch, cache_len, 2 * ATT_GROUP_WIDTH), sb,
                                 group=g, batch=batch, steps=seq, dil=dil)
            rows = kvs[g].reshape(batch, seq, 2, ATT_HEADS_PER_GROUP, ATT_HEAD_DIM)
        outs.append(o)
        lses.append(lse)
        new_rows.append(rows)
    o_att = attn_combine(outs, lses)

    merged = merge_proj(y_ssm, o_att, gates, w["ssm_proj"], w["att_proj"], tm=tm, tn=512)
    x1 = matmul(merged, w["out"], tm=tm, tn=1024, epilogue="residual", residual=x, name="out_proj")
    h2 = rmsnorm(x1, w["norm_mlp"], BF16)
    up = matmul(h2, w["up"], tm=tm, tn=1024, out_dtype=BF16, epilogue="relu2", name="mlp_up")
    x2 = matmul(up, w["down"], tm=512, tn=512, epilogue="residual", residual=x1, name="mlp_down")
    y = rmsnorm(x2, w["norm_final"], F32)
    return y, new_rows, new_conv, new_ssm


def kernel(x_prompt, x_sample, cache_win128, cache_win512, cache_win2048, state_conv, state_ssm, w_in, conv_w, conv_b, dt_bias, a_log, d_skip, ssm_norm, w_ssm_proj, w_att_proj, w_out, norm_mix, w_up, w_down, norm_mlp, rel_bias, norm_final):
    assert w_in.shape[0] == 1, "single-layer model"
    bp, sp, _ = x_prompt.shape
    bs, ss, _ = x_sample.shape
    w = dict(
        **{"in": _split_w_in(w_in[0])},
        conv_w=conv_w[0], conv_b=conv_b[0], dt_bias=dt_bias[0], a_log=a_log[0], d_skip=d_skip[0],
        ssm_norm=ssm_norm[0], ssm_proj=w_ssm_proj[0].astype(BF16), att_proj=w_att_proj[0].astype(BF16),
        out=w_out[0].astype(BF16), norm_mix=norm_mix[0], up=w_up[0].astype(BF16),
        down=w_down[0].astype(BF16), norm_mlp=norm_mlp[0], norm_final=norm_final)

    yp, rows_p, conv_p, ssm_p = _layer(
        x_prompt.reshape(bp * sp, D_MODEL), w, batch=bp, seq=sp, is_prompt=True,
        caches=None, conv_state=None, ssm_state=None, rel_bias=rel_bias)
    ys, rows_s, conv_s, ssm_s = _layer(
        x_sample.reshape(bs * ss, D_MODEL), w, batch=bs, seq=ss, is_prompt=False,
        caches=(cache_win128[0], cache_win512[0], cache_win2048[0]),
        conv_state=state_conv[0], ssm_state=state_ssm[0], rel_bias=rel_bias)

    return (yp.reshape(bp, sp, D_MODEL), ys.reshape(bs, ss, D_MODEL),
            rows_p[0][None], rows_p[1][None], rows_p[2][None], conv_p[None], ssm_p[None],
            rows_s[0][None], rows_s[1][None], rows_s[2][None], conv_s[None], ssm_s[None])
```

```python
import functools

import numpy as np
import jax
import jax.numpy as jnp
from jax import lax
from jax.experimental import pallas as pl
from jax.experimental.pallas import tpu as pltpu

D_MODEL = 2048
D_INNER = 2 * D_MODEL
SSM_HEAD_DIM = 64
SSM_HEADS = D_INNER // SSM_HEAD_DIM
SSM_GROUPS = 8
HEADS_PER_SSM_GROUP = SSM_HEADS // SSM_GROUPS
SSM_STATE = 128
SSM_GROUP_WIDTH = D_INNER // SSM_GROUPS
CONV_WIDTH = 4
CONV_DIM = D_INNER + 2 * SSM_GROUPS * SSM_STATE
SSD_CHUNK = 128
ATT_HEAD_DIM = 128
ATT_HEADS_PER_GROUP = 8
ATT_PATTERNS = ((128, 1), (512, 4), (2048, 16))
ATT_N_GROUPS = len(ATT_PATTERNS)
ATT_GROUP_WIDTH = ATT_HEADS_PER_GROUP * ATT_HEAD_DIM
ATT_WIDTH = ATT_N_GROUPS * ATT_GROUP_WIDTH
N_BUCKETS = 32
MAX_DISTANCE = max(w for w, _ in ATT_PATTERNS)
D_FF = 4 * D_MODEL
EPS = 1e-6

LANES = 128
SUBLANES = 8
VMEM_LIMIT_BYTES = 56 * 1024 * 1024

F32 = jnp.float32
BF16 = jnp.bfloat16
NEG_INF = float("-inf")


def _params(n_grid_dims):
    return pltpu.CompilerParams(
        dimension_semantics=("arbitrary",) * n_grid_dims,
        vmem_limit_bytes=VMEM_LIMIT_BYTES)


def _sigmoid(x):
    return 1.0 / (1.0 + jnp.exp(-x))


def _silu(x):
    return x * _sigmoid(x)


def _softplus(x):
    return jnp.maximum(x, 0.0) + jnp.log1p(jnp.exp(-jnp.abs(x)))


def _dot_nt(a, b):
    return lax.dot_general(a, b, (((1,), (1,)), ((), ())), preferred_element_type=F32)


def _dot_tn(a, b):
    return lax.dot_general(a, b, (((0,), (0,)), ((), ())), preferred_element_type=F32)


def _load_head(ref, h):
    rows, heads, width = ref.shape
    return ref.reshape(rows * heads, width)[pl.ds(h, rows, stride=heads), :]


def _store_head(ref, h, value):
    rows, heads, width = ref.shape
    ref.reshape(rows * heads, width)[pl.ds(h, rows, stride=heads), :] = value


def _rmsnorm_kernel(x_ref, g_ref, o_ref):
    x = x_ref[...]
    ms = jnp.mean(x * x, axis=-1, keepdims=True)
    o_ref[...] = (x * lax.rsqrt(ms + EPS) * g_ref[...]).astype(o_ref.dtype)


def rmsnorm(x, g, out_dtype, tm=512):
    m, d = x.shape
    tm = min(tm, m)
    return pl.pallas_call(
        _rmsnorm_kernel,
        grid=(m // tm,),
        in_specs=[pl.BlockSpec((tm, d), lambda i: (i, 0)),
                  pl.BlockSpec((1, d), lambda i: (0, 0))],
        out_specs=pl.BlockSpec((tm, d), lambda i: (i, 0)),
        out_shape=jax.ShapeDtypeStruct((m, d), out_dtype),
        compiler_params=_params(1),
        name="rmsnorm",
    )(x, g.reshape(1, d))


def _mm_kernel(a_ref, w_ref, *rest, epilogue):
    acc = jnp.dot(a_ref[...], w_ref[...], preferred_element_type=F32)
    if epilogue == "residual":
        r_ref, o_ref = rest
        o_ref[...] = r_ref[...] + acc
    elif epilogue == "relu2":
        (o_ref,) = rest
        u = jnp.maximum(acc, 0.0)
        o_ref[...] = (u * u).astype(o_ref.dtype)
    elif epilogue == "heads":
        (o_ref,) = rest
        for j in range(o_ref.shape[1]):
            _store_head(o_ref, j, acc[:, j * LANES:(j + 1) * LANES])
    else:
        (o_ref,) = rest
        o_ref[...] = acc.astype(o_ref.dtype)


def matmul(a, w, *, tm, tn, out_dtype=F32, epilogue="none", residual=None, name="matmul"):
    m, k = a.shape
    n = w.shape[1]
    tm = min(tm, m)
    tn = min(tn, n)
    in_specs = [pl.BlockSpec((tm, k), lambda i, j: (i, 0)),
                pl.BlockSpec((k, tn), lambda i, j: (0, j))]
    args = [a, w]
    if epilogue == "residual":
        in_specs.append(pl.BlockSpec((tm, tn), lambda i, j: (i, j)))
        args.append(residual)
    if epilogue == "heads":
        out_spec = pl.BlockSpec((tm, tn // LANES, LANES), lambda i, j: (i, j, 0))
        out_shape = jax.ShapeDtypeStruct((m, n // LANES, LANES), out_dtype)
    else:
        out_spec = pl.BlockSpec((tm, tn), lambda i, j: (i, j))
        out_shape = jax.ShapeDtypeStruct((m, n), out_dtype)
    return pl.pallas_call(
        functools.partial(_mm_kernel, epilogue=epilogue),
        grid=(m // tm, n // tn),
        in_specs=in_specs,
        out_specs=out_spec,
        out_shape=out_shape,
        compiler_params=_params(2),
        name=name,
    )(*args)


def _merge_kernel(y_ref, o_ref, gs_ref, ga_ref, ws_ref, wa_ref, out_ref):
    ps = jnp.dot(y_ref[...], ws_ref[...], preferred_element_type=F32)
    pa = jnp.dot(o_ref[...], wa_ref[...], preferred_element_type=F32)
    out_ref[...] = (_sigmoid(gs_ref[...]) * ps + _sigmoid(ga_ref[...]) * pa).astype(out_ref.dtype)


def merge_proj(y_ssm, o_att, gates, w_ssm, w_att, *, tm, tn):
    m = y_ssm.shape[0]
    n = w_ssm.shape[1]
    tm = min(tm, m)
    n_col = n // tn
    return pl.pallas_call(
        _merge_kernel,
        grid=(m // tm, n_col),
        in_specs=[pl.BlockSpec((tm, y_ssm.shape[1]), lambda i, j: (i, 0)),
                  pl.BlockSpec((tm, o_att.shape[1]), lambda i, j: (i, 0)),
                  pl.BlockSpec((tm, tn), lambda i, j: (i, j)),
                  pl.BlockSpec((tm, tn), lambda i, j: (i, j + n_col)),
                  pl.BlockSpec((w_ssm.shape[0], tn), lambda i, j: (0, j)),
                  pl.BlockSpec((w_att.shape[0], tn), lambda i, j: (0, j))],
        out_specs=pl.BlockSpec((tm, tn), lambda i, j: (i, j)),
        out_shape=jax.ShapeDtypeStruct((m, n), BF16),
        compiler_params=_params(2),
        name="merge_proj",
    )(y_ssm, o_att, gates, gates, w_ssm, w_att)


def _conv_prompt_kernel(x_ref, w_ref, b_ref, o_ref, pad_ref, *, seq):
    pad_ref[0:SUBLANES, :] = jnp.zeros((SUBLANES, pad_ref.shape[1]), F32)
    pad_ref[SUBLANES:SUBLANES + seq, :] = x_ref[...]
    first = SUBLANES - (CONV_WIDTH - 1)
    acc = b_ref[...] + pad_ref[first:first + seq, :] * w_ref[0:1, :]
    for i in range(1, CONV_WIDTH):
        acc = acc + pad_ref[first + i:first + i + seq, :] * w_ref[i:i + 1, :]
    o_ref[...] = _silu(acc)


def conv_prompt(xbc, conv_w, conv_b, *, batch, seq, tc=512):
    m, c = xbc.shape
    return pl.pallas_call(
        functools.partial(_conv_prompt_kernel, seq=seq),
        grid=(batch, c // tc),
        in_specs=[pl.BlockSpec((seq, tc), lambda b, j: (b, j)),
                  pl.BlockSpec((CONV_WIDTH, tc), lambda b, j: (0, j)),
                  pl.BlockSpec((1, tc), lambda b, j: (0, j))],
        out_specs=pl.BlockSpec((seq, tc), lambda b, j: (b, j)),
        out_shape=jax.ShapeDtypeStruct((m, c), F32),
        scratch_shapes=[pltpu.VMEM((seq + SUBLANES, tc), F32)],
        compiler_params=_params(2),
        name="conv_prompt",
    )(xbc, conv_w, conv_b.reshape(1, c))


def _conv_sample_kernel(x0_ref, x1_ref, x2_ref, x3_ref, w_ref, b_ref, o_ref):
    acc = b_ref[...] + x0_ref[...] * w_ref[0:1, :]
    for i, x_ref in enumerate((x1_ref, x2_ref, x3_ref), start=1):
        acc = acc + x_ref[...] * w_ref[i:i + 1, :]
    o_ref[...] = _silu(acc)


def conv_sample(xpad, conv_w, conv_b, *, steps, tc=512):
    bsz = xpad.shape[0]
    c = conv_w.shape[1]
    nct = c // tc
    x_specs = [pl.BlockSpec((bsz, tc), functools.partial(lambda t, j, i: (0, (t + i) * nct + j), i=i))
               for i in range(CONV_WIDTH)]
    return pl.pallas_call(
        _conv_sample_kernel,
        grid=(steps, nct),
        in_specs=x_specs + [pl.BlockSpec((CONV_WIDTH, tc), lambda t, j: (0, j)),
                            pl.BlockSpec((1, tc), lambda t, j: (0, j))],
        out_specs=pl.BlockSpec((bsz, tc), lambda t, j: (0, t * nct + j)),
        out_shape=jax.ShapeDtypeStruct((bsz, steps * c), F32),
        compiler_params=_params(2),
        name="conv_sample",
    )(xpad, xpad, xpad, xpad, conv_w, conv_b.reshape(1, c))


def _ssd_group(gl, x_ref, b_ref, c_ref, dtc_ref, dtr_ref, z_ref, pc_ref, pr_ref, nrm_ref, y_ref, h_ref,
               *, q_len, valid_len):
    nh, hd = HEADS_PER_SSM_GROUP, SSM_HEAD_DIM
    pair_w = 2 * hd
    gw = SSM_GROUP_WIDTH
    x_cols = slice(gl * gw, (gl + 1) * gw)
    n_cols = slice(gl * SSM_STATE, (gl + 1) * SSM_STATE)

    pc = pc_ref[gl]
    pr = pr_ref[gl]
    dt_c = _softplus(dtc_ref[gl] + pc[0:1, :])
    dt_r = _softplus(dtr_ref[gl] + pr[:, 0:1])
    if valid_len < q_len:
        dt_c = jnp.where(lax.broadcasted_iota(jnp.int32, dt_c.shape, 0) < valid_len, dt_c, 0.0)
        dt_r = jnp.where(lax.broadcasted_iota(jnp.int32, dt_r.shape, 1) < valid_len, dt_r, 0.0)
    a_c = dt_c * (-jnp.exp(pc[1:2, :]))
    a_r = dt_r * (-jnp.exp(pr[:, 1:2]))
    ti = lax.broadcasted_iota(jnp.int32, (q_len, q_len), 0)
    tj = lax.broadcasted_iota(jnp.int32, (q_len, q_len), 1)
    causal = ti >= tj
    acum_c = jnp.dot(causal.astype(F32), a_c, precision=lax.Precision.HIGHEST,
                     preferred_element_type=F32)
    acum_r = jnp.dot(a_r, (ti <= tj).astype(F32), precision=lax.Precision.HIGHEST,
                     preferred_element_type=F32)
    end_c = acum_c[q_len - 1:q_len, :]
    dd_c = jnp.exp(end_c - acum_c) * dt_c
    src_r = acum_r - jnp.log(dt_r)

    x = x_ref[:, x_cols]
    bmat = b_ref[:, n_cols].astype(BF16)
    cmat = c_ref[:, n_cols].astype(BF16)
    cb = _dot_nt(cmat, bmat)
    h_prev = h_ref[gl * nh:(gl + 1) * nh]
    y_off_all = _dot_nt(cmat, h_prev.reshape(nh * hd, SSM_STATE).astype(BF16))

    lane = lax.broadcasted_iota(jnp.int32, (q_len, pair_w), 1)
    first_half = lane < hd

    def pair_bcast(cols, j0):
        return jnp.where(first_half,
                         jnp.broadcast_to(cols[:, j0:j0 + 1], (q_len, pair_w)),
                         jnp.broadcast_to(cols[:, j0 + 1:j0 + 2], (q_len, pair_w)))

    ys = []
    sumsq = jnp.zeros((q_len, 1), F32)
    for i in range(nh // 2):
        j0 = 2 * i
        x_pair = x[:, i * pair_w:(i + 1) * pair_w]
        y_diag = jnp.zeros((q_len, pair_w), F32)
        for j in (j0, j0 + 1):
            seg = jnp.broadcast_to(acum_c[:, j:j + 1], (q_len, q_len)) - src_r[j:j + 1, :]
            w = (cb * jnp.exp(jnp.where(causal, seg, NEG_INF))).astype(BF16)
            own = first_half if j == j0 else jnp.logical_not(first_half)
            y_diag = y_diag + jnp.dot(w, jnp.where(own, x_pair, 0.0).astype(BF16),
                                      preferred_element_type=F32)
        e_pair = jnp.exp(pair_bcast(acum_c, j0))
        d_pair = jnp.where(first_half[0:1, :], pc[2:3, j0:j0 + 1], pc[2:3, j0 + 1:j0 + 2])
        y = y_diag + y_off_all[:, i * pair_w:(i + 1) * pair_w] * e_pair + d_pair * x_pair
        y = y * _silu(z_ref[:, gl * gw + i * pair_w:gl * gw + (i + 1) * pair_w])
        sumsq = sumsq + jnp.sum(y * y, axis=-1, keepdims=True)
        ys.append(y)

        xw = (x_pair * pair_bcast(dd_c, j0)).astype(BF16)
        s_pair = _dot_tn(xw, bmat)
        for jj, j in enumerate((j0, j0 + 1)):
            chunk_decay = jnp.exp(acum_r[j:j + 1, q_len - 1:q_len])
            h_ref[gl * nh + j] = h_prev[j] * chunk_decay + s_pair[jj * hd:(jj + 1) * hd, :]

    inv = lax.rsqrt(sumsq * (1.0 / (nh * hd)) + EPS)
    for i in range(nh // 2):
        sl = slice(gl * gw + i * pair_w, gl * gw + (i + 1) * pair_w)
        y_ref[:, sl] = (ys[i] * inv * nrm_ref[:, sl]).astype(y_ref.dtype)


def _ssd_kernel(*refs, q_len, valid_len, has_h0, groups_per_step):
    if has_h0:
        h0_ref = refs[9]
        refs = refs[:9] + refs[10:]
    h_ref = refs[-1]

    @pl.when(pl.program_id(2) == 0)
    def _():
        if has_h0:
            h_ref[...] = h0_ref[...]
        else:
            h_ref[...] = jnp.zeros(h_ref.shape, F32)

    for gl in range(groups_per_step):
        _ssd_group(gl, *refs, q_len=q_len, valid_len=valid_len)


def ssd(u, dt_raw, z, dt_bias, a_log, d_skip, ssm_norm, h0, *, batch, seq, q_len, valid_len, groups_per_step):
    nh, g, gps = HEADS_PER_SSM_GROUP, SSM_GROUPS, groups_per_step
    nc = seq // q_len
    m = batch * seq
    dt4 = dt_raw[:, :SSM_HEADS].reshape(batch * nc, q_len, g, nh)
    dtc = jnp.transpose(dt4, (0, 2, 1, 3))
    dtr = jnp.transpose(dt4, (0, 2, 3, 1))
    pcol = jnp.stack([dt_bias, a_log, d_skip]).reshape(3, g, nh).transpose(1, 0, 2)
    prow = jnp.transpose(pcol, (0, 2, 1))
    b_off = D_INNER // (gps * SSM_STATE)
    c_off = b_off + g // gps
    row = lambda b, gi, c: b * nc + c
    in_specs = [
        pl.BlockSpec((q_len, gps * SSM_GROUP_WIDTH), lambda b, gi, c: (row(b, gi, c), gi)),
        pl.BlockSpec((q_len, gps * SSM_STATE), lambda b, gi, c: (row(b, gi, c), b_off + gi)),
        pl.BlockSpec((q_len, gps * SSM_STATE), lambda b, gi, c: (row(b, gi, c), c_off + gi)),
        pl.BlockSpec((None, gps, q_len, nh), lambda b, gi, c: (row(b, gi, c), gi, 0, 0)),
        pl.BlockSpec((None, gps, nh, q_len), lambda b, gi, c: (row(b, gi, c), gi, 0, 0)),
        pl.BlockSpec((q_len, gps * SSM_GROUP_WIDTH), lambda b, gi, c: (row(b, gi, c), gi)),
        pl.BlockSpec((gps, 3, nh), lambda b, gi, c: (gi, 0, 0)),
        pl.BlockSpec((gps, nh, 3), lambda b, gi, c: (gi, 0, 0)),
        pl.BlockSpec((1, gps * SSM_GROUP_WIDTH), lambda b, gi, c: (0, gi)),
    ]
    args = [u, u, u, dtc, dtr, z, pcol, prow, ssm_norm.reshape(1, D_INNER)]
    h_spec = pl.BlockSpec((None, gps * nh, SSM_HEAD_DIM, SSM_STATE), lambda b, gi, c: (b, gi, 0, 0))
    if h0 is not None:
        in_specs.append(h_spec)
        args.append(h0)
    return pl.pallas_call(
        functools.partial(_ssd_kernel, q_len=q_len, valid_len=valid_len, has_h0=h0 is not None,
                          groups_per_step=gps),
        grid=(batch, g // gps, nc),
        in_specs=in_specs,
        out_specs=[pl.BlockSpec((q_len, gps * SSM_GROUP_WIDTH), lambda b, gi, c: (row(b, gi, c), gi)), h_spec],
        out_shape=[jax.ShapeDtypeStruct((m, D_INNER), BF16),
                   jax.ShapeDtypeStruct((batch, SSM_HEADS, SSM_HEAD_DIM, SSM_STATE), F32)],
        compiler_params=_params(3),
        name="ssd",
    )(*args)


def _attn_prompt_kernel(*refs, kb, has_prev):
    if has_prev:
        q_ref, kp_ref, kc_ref, vp_ref, vc_ref, mb_ref, o_ref, l_ref = refs
    else:
        q_ref, kc_ref, vc_ref, mb_ref, o_ref, l_ref = refs
    not_first = pl.program_id(2) > 0
    scale = ATT_HEAD_DIM ** -0.5
    for h in range(ATT_HEADS_PER_GROUP):
        q = (_load_head(q_ref, h) * scale).astype(BF16)
        s_c = _dot_nt(q, _load_head(kc_ref, h).astype(BF16)) + mb_ref[h, :, kb:]
        m = jnp.max(s_c, axis=-1, keepdims=True)
        if has_prev:
            s_p = _dot_nt(q, _load_head(kp_ref, h).astype(BF16)) + mb_ref[h, :, :kb]
            s_p = jnp.where(not_first, s_p, NEG_INF)
            m = jnp.maximum(m, jnp.max(s_p, axis=-1, keepdims=True))
        p_c = jnp.exp(s_c - m)
        den = jnp.sum(p_c, axis=-1, keepdims=True)
        o = jnp.dot(p_c.astype(BF16), _load_head(vc_ref, h).astype(BF16), preferred_element_type=F32)
        if has_prev:
            p_p = jnp.exp(s_p - m)
            den = den + jnp.sum(p_p, axis=-1, keepdims=True)
            o = o + jnp.dot(p_p.astype(BF16), _load_head(vp_ref, h).astype(BF16), preferred_element_type=F32)
        _store_head(o_ref, h, o / den)
        _store_head(l_ref, h, jnp.broadcast_to(m + jnp.log(den), (kb, ATT_HEAD_DIM)))


def attn_prompt(q, kv, mbias, *, group, batch, seq, dil, kb):
    m_res = seq // dil
    nb = m_res // kb
    has_prev = nb > 1
    nh = ATT_HEADS_PER_GROUP
    q4 = q.reshape(batch * m_res, dil, q.shape[1], ATT_HEAD_DIM)
    kv4 = kv.reshape(batch * m_res, dil, 2 * nh, ATT_HEAD_DIM)

    def cur(head_block):
        return pl.BlockSpec((kb, None, nh, ATT_HEAD_DIM), lambda b, r, n: (b * nb + n, r, head_block, 0))

    def prev(head_block):
        return pl.BlockSpec((kb, None, nh, ATT_HEAD_DIM),
                            lambda b, r, n: (b * nb + jnp.maximum(n - 1, 0), r, head_block, 0))

    if has_prev:
        in_specs = [cur(group), prev(0), cur(0), prev(1), cur(1)]
        args = [q4, kv4, kv4, kv4, kv4]
    else:
        in_specs = [cur(group), cur(0), cur(1)]
        args = [q4, kv4, kv4]
    in_specs.append(pl.BlockSpec(mbias.shape, lambda b, r, n: (0, 0, 0)))
    args.append(mbias)
    o, lse = pl.pallas_call(
        functools.partial(_attn_prompt_kernel, kb=kb, has_prev=has_prev),
        grid=(batch, dil, nb),
        in_specs=in_specs,
        out_specs=[cur(0), cur(0)],
        out_shape=[jax.ShapeDtypeStruct((batch * m_res, dil, nh, ATT_HEAD_DIM), F32)] * 2,
        compiler_params=_params(3),
        name=f"attn_prompt_g{group}",
    )(*args)
    return o.reshape(batch * seq, nh, ATT_HEAD_DIM), lse.reshape(batch * seq, nh, ATT_HEAD_DIM)


def _attn_sample_kernel(*refs, n_cache, steps, q_rows):
    q_ref, new_ref = refs[0], refs[1]
    cache_refs = refs[2:2 + n_cache]
    bias_ref = refs[2 + n_cache]
    o_ref, l_ref = refs[3 + n_cache:]
    gw = ATT_GROUP_WIDTH
    nh = ATT_HEADS_PER_GROUP
    hd = ATT_HEAD_DIM
    scale = ATT_HEAD_DIM ** -0.5
    per_cache = steps // n_cache
    sub = lax.broadcasted_iota(jnp.int32, (nh, gw), 0)
    own = sub == lax.broadcasted_iota(jnp.int32, (nh, gw), 1) // hd
    for ci in range(n_cache):
        q_blocks = []
        for tt in range(per_cache):
            q_t = q_ref[ci * per_cache + tt] * scale
            q_blocks.append(jnp.where(own, jnp.concatenate([q_t] * nh, axis=1), 0.0))
        if q_rows > per_cache * nh:
            q_blocks.append(jnp.zeros((q_rows - per_cache * nh, gw), F32))
        qbd = jnp.concatenate(q_blocks, axis=0).astype(BF16)
        k_all = jnp.concatenate(
            [jnp.concatenate([_load_head(cache_refs[ci], h), _load_head(new_ref, h)], axis=0)
             for h in range(nh)], axis=1).astype(BF16)
        v_all = jnp.concatenate(
            [jnp.concatenate([_load_head(cache_refs[ci], nh + h), _load_head(new_ref, nh + h)], axis=0)
             for h in range(nh)], axis=1).astype(BF16)
        s = _dot_nt(qbd, k_all) + bias_ref[ci]
        m = jnp.max(s, axis=-1, keepdims=True)
        p = jnp.exp(s - m)
        den = jnp.sum(p, axis=-1, keepdims=True)
        o_all = jnp.dot(p.astype(BF16), v_all, preferred_element_type=F32) / den
        lse = m + jnp.log(den)
        for tt in range(per_cache):
            t = ci * per_cache + tt
            rows = slice(tt * nh, (tt + 1) * nh)
            o_t = jnp.zeros((nh, hd), F32)
            for h in range(nh):
                o_t = o_t + jnp.where(sub[:, :hd] == h, o_all[rows, h * hd:(h + 1) * hd], 0.0)
            o_ref[t] = o_t
            l_ref[t] = jnp.broadcast_to(lse[rows, :], (nh, hd))


def attn_sample(q, kv_new_pad, cache, bias, *, group, batch, steps, dil):
    nh, hd = ATT_HEADS_PER_GROUP, ATT_HEAD_DIM
    n_cache, q_rows, n_keys_tot = bias.shape
    new_rows = kv_new_pad.shape[1]
    rows_per_cache = n_keys_tot - new_rows
    q4 = q.reshape(batch, steps, q.shape[1], hd)
    if n_cache == 1:
        cache_view = cache.reshape(batch, rows_per_cache, 2 * nh, hd)
        cache_specs = [pl.BlockSpec((None, rows_per_cache, 2 * nh, hd), lambda b: (b, 0, 0, 0))]
    else:
        cache_view = cache.reshape(batch, rows_per_cache, dil, 2 * nh, hd)
        cache_specs = [pl.BlockSpec((None, rows_per_cache, None, 2 * nh, hd),
                                    functools.partial(lambda b, t: (b, 0, t, 0, 0), t=t))
                       for t in range(n_cache)]
    out_spec = pl.BlockSpec((None, steps, nh, hd), lambda b: (b, 0, 0, 0))
    o, lse = pl.pallas_call(
        functools.partial(_attn_sample_kernel, n_cache=n_cache, steps=steps, q_rows=q_rows),
        grid=(batch,),
        in_specs=[pl.BlockSpec((None, steps, nh, hd), lambda b: (b, 0, group, 0)),
                  pl.BlockSpec((None, new_rows, 2 * nh, hd), lambda b: (b, 0, 0, 0))]
                 + cache_specs
                 + [pl.BlockSpec(bias.shape, lambda b: (0, 0, 0))],
        out_specs=[out_spec, out_spec],
        out_shape=[jax.ShapeDtypeStruct((batch, steps, nh, hd), F32)] * 2,
        compiler_params=_params(1),
        name=f"attn_sample_g{group}",
    )(q4, kv_new_pad, *([cache_view] * n_cache), bias)
    return o.reshape(batch * steps, nh, hd), lse.reshape(batch * steps, nh, hd)


def _attn_combine_kernel(o0, o1, o2, l0, l1, l2, out_ref, mix_ref):
    ls = (l0[...], l1[...], l2[...])
    m = jnp.maximum(jnp.maximum(ls[0], ls[1]), ls[2])
    ws = [jnp.exp(l - m) for l in ls]
    num = ws[0] * o0[...] + ws[1] * o1[...] + ws[2] * o2[...]
    mix_ref[...] = num / (ws[0] + ws[1] + ws[2])
    for h in range(ATT_HEADS_PER_GROUP):
        out_ref[:, h * ATT_HEAD_DIM:(h + 1) * ATT_HEAD_DIM] = _load_head(mix_ref, h).astype(out_ref.dtype)


def attn_combine(outs, lses, tm=512):
    m, nh, hd = outs[0].shape
    tm = min(tm, m)
    spec = pl.BlockSpec((tm, nh, hd), lambda i: (i, 0, 0))
    return pl.pallas_call(
        _attn_combine_kernel,
        grid=(m // tm,),
        in_specs=[spec] * 6,
        out_specs=pl.BlockSpec((tm, nh * hd), lambda i: (i, 0)),
        out_shape=jax.ShapeDtypeStruct((m, nh * hd), BF16),
        scratch_shapes=[pltpu.VMEM((tm, nh, hd), F32)],
        compiler_params=_params(1),
        name="attn_combine",
    )(*outs, *lses)


def _t5_bucket(dist):
    max_exact = N_BUCKETS // 2
    d = np.maximum(dist, max_exact).astype(np.float32)
    large = max_exact + (np.log(d / max_exact) / np.log(MAX_DISTANCE / max_exact)
                         * (N_BUCKETS - max_exact)).astype(np.int32)
    large = np.minimum(large, N_BUCKETS - 1)
    return np.where(dist < max_exact, dist, large).astype(np.int32)


def _group_bias(rel_bias, g):
    win, dil = ATT_PATTERNS[g]
    buckets = _t5_bucket(np.arange(win // dil + 1) * dil)
    return rel_bias[buckets][:, g * ATT_HEADS_PER_GROUP:(g + 1) * ATT_HEADS_PER_GROUP]


def _prompt_mask_bias(bias_g, n_keys, kb):
    period = 3 * kb
    offs = np.arange(period)
    offs = np.where(offs >= 2 * kb, offs - period, offs)
    j_of = np.clip(kb - offs, 0, n_keys)
    a = jnp.transpose(bias_g[j_of]).astype(F32)
    nh = a.shape[0]
    tiled = jnp.tile(a, (1, kb))[:, :kb * (period - 1)].reshape(nh, kb, period - 1)
    rel = np.arange(kb)[:, None] + kb - np.arange(2 * kb)[None, :]
    valid = (rel >= 0) & (rel <= n_keys)
    return jnp.where(valid[None], tiled[:, :, :2 * kb], NEG_INF)


def _sample_mask_bias(bias_g, *, cache_len, dil, n_keys, steps, new_rows, shared_cache, q_rows):
    nh = ATT_HEADS_PER_GROUP
    n_cache = 1 if shared_cache else steps
    per_cache = steps // n_cache
    rows_per_cache = cache_len if shared_cache else cache_len // dil
    blocks = []
    for ci in range(n_cache):
        if shared_cache:
            cache_pos = np.arange(rows_per_cache)
        else:
            cache_pos = ci + dil * np.arange(rows_per_cache)
        key_pos = np.concatenate([cache_pos, cache_len + np.arange(new_rows)])
        key_live = np.concatenate([np.ones(rows_per_cache, bool), np.arange(new_rows) < steps])
        rows = []
        for tt in range(per_cache):
            t = ci * per_cache + tt
            diff = cache_len + t - key_pos
            valid = key_live & (diff >= 0) & (diff % dil == 0) & (diff // dil <= n_keys)
            j = np.clip(diff // dil, 0, n_keys)
            rows.append(jnp.where(valid[None, :], bias_g[j].T.astype(F32), NEG_INF))
        blk = jnp.concatenate(rows, axis=0)
        pad = q_rows - per_cache * nh
        if pad:
            blk = jnp.concatenate([blk, jnp.zeros((pad, blk.shape[1]), F32)], axis=0)
        blocks.append(blk)
    return jnp.stack(blocks)


SAMPLE_NEW_ROWS = 16
SAMPLE_Q_ROWS_MIN = 16
SSD_SAMPLE_Q = 16


def _split_w_in(w_in):
    sizes = (D_INNER, CONV_DIM, SSM_HEADS, ATT_WIDTH, ATT_WIDTH, ATT_WIDTH, D_MODEL, D_MODEL)
    offs = np.concatenate([[0], np.cumsum(sizes)])
    seg = lambda i: w_in[:, offs[i]:offs[i + 1]]
    gw = ATT_GROUP_WIDTH
    w_kv = [jnp.concatenate([seg(4)[:, g * gw:(g + 1) * gw], seg(5)[:, g * gw:(g + 1) * gw]], axis=1).astype(BF16)
            for g in range(ATT_N_GROUPS)]
    w_dt = jnp.pad(seg(2), ((0, 0), (0, LANES - SSM_HEADS))).astype(BF16)
    w_gates = jnp.concatenate([seg(6), seg(7)], axis=1).astype(BF16)
    return dict(z=seg(0).astype(BF16), xbc=seg(1).astype(BF16), dt=w_dt, q=seg(3).astype(BF16),
                kv=w_kv, gates=w_gates)


def _layer(x, w, *, batch, seq, is_prompt, caches, conv_state, ssm_state, rel_bias):
    m = batch * seq
    nh, hd = ATT_HEADS_PER_GROUP, ATT_HEAD_DIM
    tm = 1024 if m % 1024 == 0 else 512
    h = rmsnorm(x, w["norm_mix"], BF16)
    proj = lambda wseg, name, **kw: matmul(h, wseg, tm=tm, tn=1024, name=name, **kw)
    z = proj(w["in"]["z"], "in_z")
    xbc = proj(w["in"]["xbc"], "in_xbc")
    dt_raw = proj(w["in"]["dt"], "in_dt")
    q = proj(w["in"]["q"], "in_q", epilogue="heads")
    kvs = [proj(w["in"]["kv"][g], f"in_kv{g}", epilogue="heads")
           for g in range(ATT_N_GROUPS)]
    gates = proj(w["in"]["gates"], "in_gates")

    if is_prompt:
        u = conv_prompt(xbc, w["conv_w"], w["conv_b"], batch=batch, seq=seq)
        new_conv = xbc.reshape(batch, seq, CONV_DIM)[:, seq - (CONV_WIDTH - 1):]
        y_ssm, new_ssm = ssd(u, dt_raw, z, w["dt_bias"], w["a_log"], w["d_skip"], w["ssm_norm"], None,
                             batch=batch, seq=seq, q_len=SSD_CHUNK, valid_len=SSD_CHUNK, groups_per_step=1)
    else:
        xpad = jnp.concatenate([conv_state.reshape(batch, (CONV_WIDTH - 1) * CONV_DIM),
                                xbc.reshape(batch, seq * CONV_DIM)], axis=1)
        u = conv_sample(xpad, w["conv_w"], w["conv_b"], steps=seq)
        new_conv = xpad.reshape(batch, CONV_WIDTH - 1 + seq, CONV_DIM)[:, seq:]
        qp = SSD_SAMPLE_Q
        pad_rows = lambda a: jnp.pad(a.reshape(batch, seq, a.shape[-1]),
                                     ((0, 0), (0, qp - seq), (0, 0))).reshape(batch * qp, a.shape[-1])
        y_pad, new_ssm = ssd(pad_rows(u.reshape(m, CONV_DIM)), pad_rows(dt_raw), pad_rows(z),
                             w["dt_bias"], w["a_log"], w["d_skip"], w["ssm_norm"], ssm_state,
                             batch=batch, seq=qp, q_len=qp, valid_len=seq, groups_per_step=SSM_GROUPS)
        y_ssm = y_pad.reshape(batch, qp, D_INNER)[:, :seq].reshape(m, D_INNER)

    outs, lses, new_rows = [], [], []
    for g, (win, dil) in enumerate(ATT_PATTERNS):
        n_keys = win // dil
        bias_g = _group_bias(rel_bias, g)
        rows = kvs[g].reshape(batch, seq, 2, nh, hd)
        if is_prompt:
            mb = _prompt_mask_bias(bias_g, n_keys, n_keys)
            o, lse = attn_prompt(q, kvs[g], mb, group=g, batch=batch, seq=seq, dil=dil, kb=n_keys)
            keep = min(win, seq)
            rows = rows[:, seq - keep:]
        else:
            cache = caches[g]
            cache_len = cache.shape[1]
            shared = dil == 1
            q_rows = max(SAMPLE_Q_ROWS_MIN, (seq if shared else 1) * nh)
            sb = _sample_mask_bias(bias_g, cache_len=cache_len, dil=dil, n_keys=n_keys, steps=seq,
                                   new_rows=SAMPLE_NEW_ROWS, shared_cache=shared, q_rows=q_rows)
            kv_new = jnp.pad(kvs[g].reshape(batch, seq, 2 * nh, hd),
                             ((0, 0), (0, SAMPLE_NEW_ROWS - seq), (0, 0), (0, 0)))
            o, lse = attn_sample(q, kv_new, cache, sb, group=g, batch=batch, steps=seq, dil=dil)
        outs.append(o)
        lses.append(lse)
        new_rows.append(rows)
    o_att = attn_combine(outs, lses)

    merged = merge_proj(y_ssm, o_att, gates, w["ssm_proj"], w["att_proj"], tm=tm, tn=512)
    x1 = matmul(merged, w["out"], tm=tm, tn=1024, epilogue="residual", residual=x, name="out_proj")
    h2 = rmsnorm(x1, w["norm_mlp"], BF16)
    up = matmul(h2, w["up"], tm=tm, tn=1024, out_dtype=BF16, epilogue="relu2", name="mlp_up")
    x2 = matmul(up, w["down"], tm=512, tn=512, epilogue="residual", residual=x1, name="mlp_down")
    y = rmsnorm(x2, w["norm_final"], F32)
    return y, new_rows, new_conv, new_ssm


def kernel(x_prompt, x_sample, cache_win128, cache_win512, cache_win2048, state_conv, state_ssm, w_in, conv_w, conv_b, dt_bias, a_log, d_skip, ssm_norm, w_ssm_proj, w_att_proj, w_out, norm_mix, w_up, w_down, norm_mlp, rel_bias, norm_final):
    assert w_in.shape[0] == 1, "single-layer model"
    bp, sp, _ = x_prompt.shape
    bs, ss, _ = x_sample.shape
    w = dict(
        **{"in": _split_w_in(w_in[0])},
        conv_w=conv_w[0], conv_b=conv_b[0], dt_bias=dt_bias[0], a_log=a_log[0], d_skip=d_skip[0],
        ssm_norm=ssm_norm[0], ssm_proj=w_ssm_proj[0].astype(BF16), att_proj=w_att_proj[0].astype(BF16),
        out=w_out[0].astype(BF16), norm_mix=norm_mix[0], up=w_up[0].astype(BF16),
        down=w_down[0].astype(BF16), norm_mlp=norm_mlp[0], norm_final=norm_final)

    yp, rows_p, conv_p, ssm_p = _layer(
        x_prompt.reshape(bp * sp, D_MODEL), w, batch=bp, seq=sp, is_prompt=True,
        caches=None, conv_state=None, ssm_state=None, rel_bias=rel_bias)
    ys, rows_s, conv_s, ssm_s = _layer(
        x_sample.reshape(bs * ss, D_MODEL), w, batch=bs, seq=ss, is_prompt=False,
        caches=(cache_win128[0], cache_win512[0], cache_win2048[0]),
        conv_state=state_conv[0], ssm_state=state_ssm[0], rel_bias=rel_bias)

    return (yp.reshape(bp, sp, D_MODEL), ys.reshape(bs, ss, D_MODEL),
            rows_p[0][None], rows_p[1][None], rows_p[2][None], conv_p[None], ssm_p[None],
            rows_s[0][None], rows_s[1][None], rows_s[2][None], conv_s[None], ssm_s[None])
```

```python
import functools
import math

import numpy as np
import jax
import jax.numpy as jnp
from jax import lax
from jax.experimental import pallas as pl
from jax.experimental.pallas import tpu as pltpu

D_MODEL = 2048
D_INNER = 2 * D_MODEL
SSM_HEAD_DIM = 64
SSM_HEADS = D_INNER // SSM_HEAD_DIM
SSM_GROUPS = 8
HEADS_PER_SSM_GROUP = SSM_HEADS // SSM_GROUPS
SSM_STATE = 128
SSM_GROUP_WIDTH = D_INNER // SSM_GROUPS
CONV_WIDTH = 4
CONV_DIM = D_INNER + 2 * SSM_GROUPS * SSM_STATE
SSD_CHUNK = 128
ATT_HEAD_DIM = 128
ATT_HEADS_PER_GROUP = 8
ATT_PATTERNS = ((128, 1), (512, 4), (2048, 16))
ATT_N_GROUPS = len(ATT_PATTERNS)
ATT_GROUP_WIDTH = ATT_HEADS_PER_GROUP * ATT_HEAD_DIM
ATT_WIDTH = ATT_N_GROUPS * ATT_GROUP_WIDTH
N_BUCKETS = 32
MAX_DISTANCE = max(w for w, _ in ATT_PATTERNS)
D_FF = 4 * D_MODEL
EPS = 1e-6

LANES = 128
SUBLANES = 8
VMEM_LIMIT_BYTES = 56 * 1024 * 1024

F32 = jnp.float32
BF16 = jnp.bfloat16
NEG_INF = float("-inf")


def _params(n_grid_dims):
    return pltpu.CompilerParams(
        dimension_semantics=("arbitrary",) * n_grid_dims,
        vmem_limit_bytes=VMEM_LIMIT_BYTES)


def _sigmoid(x):
    return 0.5 * jnp.tanh(0.5 * x) + 0.5


def _silu(x):
    return x * _sigmoid(x)


def _softplus(x):
    return jnp.maximum(x, 0.0) + jnp.log1p(jnp.exp(-jnp.abs(x)))


def _dot_nt(a, b):
    return lax.dot_general(a, b, (((1,), (1,)), ((), ())), preferred_element_type=F32)


def _dot_tn(a, b):
    return lax.dot_general(a, b, (((0,), (0,)), ((), ())), preferred_element_type=F32)


def _load_rows(ref, offset, rows, stride):
    flat = ref.reshape(math.prod(ref.shape[:-1]), ref.shape[-1])
    return flat[pl.ds(offset, rows, stride=stride), :]


def _store_rows(ref, offset, rows, stride, value):
    flat = ref.reshape(math.prod(ref.shape[:-1]), ref.shape[-1])
    flat[pl.ds(offset, rows, stride=stride), :] = value


def _load_head(ref, h):
    return _load_rows(ref, h, ref.shape[0], ref.shape[1])


def _store_head(ref, h, value):
    _store_rows(ref, h, ref.shape[0], ref.shape[1], value)


def _rmsnorm_kernel(x_ref, g_ref, o_ref):
    x = x_ref[...]
    ms = jnp.mean(x * x, axis=-1, keepdims=True)
    o_ref[...] = (x * lax.rsqrt(ms + EPS) * g_ref[...]).astype(o_ref.dtype)


def rmsnorm(x, g, out_dtype, tm=512):
    m, d = x.shape
    tm = min(tm, m)
    return pl.pallas_call(
        _rmsnorm_kernel,
        grid=(m // tm,),
        in_specs=[pl.BlockSpec((tm, d), lambda i: (i, 0)),
                  pl.BlockSpec((1, d), lambda i: (0, 0))],
        out_specs=pl.BlockSpec((tm, d), lambda i: (i, 0)),
        out_shape=jax.ShapeDtypeStruct((m, d), out_dtype),
        compiler_params=_params(1),
        name="rmsnorm",
    )(x, g.reshape(1, d))


def _bf16_weights(w_ref, wbf_ref):
    if wbf_ref is None:
        return w_ref

    @pl.when(pl.program_id(1) == 0)
    def _():
        wbf_ref[...] = w_ref[...].astype(BF16)

    return wbf_ref


CONV_COL_CHUNK = 256


def _conv_silu_tiles(a_ref, w_ref, cw_ref, cb_ref, o_ref, tail_ref, pad_ref, *, seq_tiles):
    tm = a_ref.shape[0]
    first = SUBLANES - (CONV_WIDTH - 1)

    @pl.when(pl.program_id(1) % seq_tiles == 0)
    def _():
        pad_ref[0:SUBLANES, :] = jnp.zeros((SUBLANES, pad_ref.shape[1]), F32)

    a = a_ref[...]
    for c0 in range(0, w_ref.shape[1], CONV_COL_CHUNK):
        cols = slice(c0, c0 + CONV_COL_CHUNK)
        acc = jnp.dot(a, w_ref[:, cols], preferred_element_type=F32)
        pad_ref[SUBLANES:SUBLANES + tm, cols] = acc
        conv = cb_ref[:, cols] + pad_ref[first:first + tm, cols] * cw_ref[0:1, cols]
        for i in range(1, CONV_WIDTH):
            conv = conv + pad_ref[first + i:first + i + tm, cols] * cw_ref[i:i + 1, cols]
        o_ref[:, cols] = _silu(conv)
        last_rows = acc[tm - SUBLANES:, :]
        pad_ref[0:SUBLANES, cols] = last_rows
        tail_ref[:, cols] = last_rows


def _mm_kernel(*refs, epilogue, cast_w, seq_tiles):
    refs = list(refs)
    a_ref, w_ref = refs[:2]
    wbf_ref = None
    if epilogue == "conv_silu":
        pad_ref = refs.pop()
    if cast_w:
        wbf_ref = refs.pop()
    rest = refs[2:]
    if epilogue == "conv_silu":
        _conv_silu_tiles(a_ref, _bf16_weights(w_ref, wbf_ref), *rest, pad_ref, seq_tiles=seq_tiles)
        return
    acc = jnp.dot(a_ref[...], _bf16_weights(w_ref, wbf_ref)[...], preferred_element_type=F32)
    if epilogue == "residual":
        r_ref, o_ref = rest
        o_ref[...] = r_ref[...] + acc
    elif epilogue == "relu2":
        (o_ref,) = rest
        u = jnp.maximum(acc, 0.0)
        o_ref[...] = (u * u).astype(o_ref.dtype)
    elif epilogue == "heads":
        (o_ref,) = rest
        for j in range(o_ref.shape[1]):
            _store_head(o_ref, j, acc[:, j * LANES:(j + 1) * LANES])
    else:
        (o_ref,) = rest
        o_ref[...] = acc.astype(o_ref.dtype)


def matmul(a, w, *, tm, tn, n=None, col0=0, out_dtype=F32, epilogue="none", residual=None,
           conv=None, seq=None, name="matmul"):
    m, k = a.shape
    n = w.shape[1] if n is None else n
    tm = min(tm, m)
    tn = min(tn, n)
    assert m % tm == 0 and n % tn == 0 and col0 % tn == 0
    cb0 = col0 // tn
    cast_w = w.dtype != BF16
    in_specs = [pl.BlockSpec((tm, k), lambda j, i: (i, 0)),
                pl.BlockSpec((k, tn), lambda j, i: (0, cb0 + j))]
    args = [a, w]
    scratch = [pltpu.VMEM((k, tn), BF16)] if cast_w else []
    out_spec = pl.BlockSpec((tm, tn), lambda j, i: (i, j))
    out_shape = jax.ShapeDtypeStruct((m, n), out_dtype)
    seq_tiles = None
    if epilogue == "residual":
        in_specs.append(pl.BlockSpec((tm, tn), lambda j, i: (i, j)))
        args.append(residual)
    elif epilogue == "heads":
        out_spec = pl.BlockSpec((tm, tn // LANES, LANES), lambda j, i: (i, j, 0))
        out_shape = jax.ShapeDtypeStruct((m, n // LANES, LANES), out_dtype)
    elif epilogue == "conv_silu":
        assert seq % tm == 0
        seq_tiles = seq // tm
        conv_w, conv_b = conv
        in_specs += [pl.BlockSpec((CONV_WIDTH, tn), lambda j, i: (0, j)),
                     pl.BlockSpec((1, tn), lambda j, i: (0, j))]
        args += [conv_w, conv_b.reshape(1, n)]
        out_spec = [out_spec, pl.BlockSpec((None, SUBLANES, tn), lambda j, i: (i // seq_tiles, 0, j))]
        out_shape = [out_shape, jax.ShapeDtypeStruct((m // seq, SUBLANES, n), F32)]
        scratch.append(pltpu.VMEM((tm + SUBLANES, tn), F32))
    return pl.pallas_call(
        functools.partial(_mm_kernel, epilogue=epilogue, cast_w=cast_w, seq_tiles=seq_tiles),
        grid=(n // tn, m // tm),
        in_specs=in_specs,
        out_specs=out_spec,
        out_shape=out_shape,
        scratch_shapes=scratch,
        compiler_params=_params(2),
        name=name,
    )(*args)


def _merge_kernel(y_ref, o_ref, gs_ref, ga_ref, ws_ref, wa_ref, out_ref, wsbf_ref, wabf_ref):
    ps = jnp.dot(y_ref[...], _bf16_weights(ws_ref, wsbf_ref)[...], preferred_element_type=F32)
    pa = jnp.dot(o_ref[...], _bf16_weights(wa_ref, wabf_ref)[...], preferred_element_type=F32)
    out_ref[...] = (_sigmoid(gs_ref[...]) * ps + _sigmoid(ga_ref[...]) * pa).astype(out_ref.dtype)


def merge_proj(y_ssm, o_att, gates, w_ssm, w_att, *, tm, tn):
    m = y_ssm.shape[0]
    n = w_ssm.shape[1]
    tm = min(tm, m)
    n_col = n // tn
    return pl.pallas_call(
        _merge_kernel,
        grid=(n_col, m // tm),
        in_specs=[pl.BlockSpec((tm, y_ssm.shape[1]), lambda j, i: (i, 0)),
                  pl.BlockSpec((tm, o_att.shape[1]), lambda j, i: (i, 0)),
                  pl.BlockSpec((tm, tn), lambda j, i: (i, j)),
                  pl.BlockSpec((tm, tn), lambda j, i: (i, j + n_col)),
                  pl.BlockSpec((w_ssm.shape[0], tn), lambda j, i: (0, j)),
                  pl.BlockSpec((w_att.shape[0], tn), lambda j, i: (0, j))],
        out_specs=pl.BlockSpec((tm, tn), lambda j, i: (i, j)),
        out_shape=jax.ShapeDtypeStruct((m, n), BF16),
        scratch_shapes=[pltpu.VMEM((w_ssm.shape[0], tn), BF16), pltpu.VMEM((w_att.shape[0], tn), BF16)],
        compiler_params=_params(2),
        name="merge_proj",
    )(y_ssm, o_att, gates, gates, w_ssm, w_att)


def _conv_sample_kernel(x0_ref, x1_ref, x2_ref, x3_ref, w_ref, b_ref, o_ref):
    acc = b_ref[...] + x0_ref[...] * w_ref[0:1, :]
    for i, x_ref in enumerate((x1_ref, x2_ref, x3_ref), start=1):
        acc = acc + x_ref[...] * w_ref[i:i + 1, :]
    o_ref[...] = _silu(acc)


def conv_sample(xpad, conv_w, conv_b, *, steps, tc=512):
    bsz = xpad.shape[0]
    c = conv_w.shape[1]
    nct = c // tc
    x_specs = [pl.BlockSpec((bsz, tc), functools.partial(lambda t, j, i: (0, (t + i) * nct + j), i=i))
               for i in range(CONV_WIDTH)]
    return pl.pallas_call(
        _conv_sample_kernel,
        grid=(steps, nct),
        in_specs=x_specs + [pl.BlockSpec((CONV_WIDTH, tc), lambda t, j: (0, j)),
                            pl.BlockSpec((1, tc), lambda t, j: (0, j))],
        out_specs=pl.BlockSpec((bsz, tc), lambda t, j: (0, t * nct + j)),
        out_shape=jax.ShapeDtypeStruct((bsz, steps * c), F32),
        compiler_params=_params(2),
        name="conv_sample",
    )(xpad, xpad, xpad, xpad, conv_w, conv_b.reshape(1, c))


def _split3(a):
    hi = a.astype(BF16)
    rest = a - hi.astype(F32)
    mid = rest.astype(BF16)
    return hi, mid, (rest - mid.astype(F32)).astype(BF16)


def _ssd_group(gl, x_ref, b_ref, c_ref, dtc_ref, dtr_ref, z_ref, pc_ref, pr_ref, nrm_ref, y_ref, h_ref,
               ht_ref, *, q_len, valid_len, last_chunk):
    nh, hd = HEADS_PER_SSM_GROUP, SSM_HEAD_DIM
    pair_w = 2 * hd
    gw = SSM_GROUP_WIDTH
    x_cols = slice(gl * gw, (gl + 1) * gw)
    n_cols = slice(gl * SSM_STATE, (gl + 1) * SSM_STATE)

    pc = pc_ref[gl]
    pr = pr_ref[gl]
    dt_c = _softplus(dtc_ref[gl] + pc[0:1, :])
    dt_r = _softplus(dtr_ref[gl] + pr[:, 0:1])
    if valid_len < q_len:
        dt_c = jnp.where(lax.broadcasted_iota(jnp.int32, dt_c.shape, 0) < valid_len, dt_c, 0.0)
        dt_r = jnp.where(lax.broadcasted_iota(jnp.int32, dt_r.shape, 1) < valid_len, dt_r, 0.0)
    a_c = dt_c * (-jnp.exp(pc[1:2, :]))
    a_r = dt_r * (-jnp.exp(pr[:, 1:2]))
    ti = lax.broadcasted_iota(jnp.int32, (q_len, q_len), 0)
    tj = lax.broadcasted_iota(jnp.int32, (q_len, q_len), 1)
    causal = ti >= tj
    tril = causal.astype(BF16)
    triu = (ti <= tj).astype(BF16)
    acum_c = sum(jnp.dot(tril, part, preferred_element_type=F32) for part in _split3(a_c))
    acum_r = sum(jnp.dot(part, triu, preferred_element_type=F32) for part in _split3(a_r))
    end_c = acum_c[q_len - 1:q_len, :]
    dd_c = jnp.exp(end_c - acum_c) * dt_c
    src_r = acum_r - jnp.log(dt_r)

    x = x_ref[:, x_cols]
    cmat = c_ref[:, n_cols].astype(BF16)
    if ht_ref is None:
        bmat = b_ref[:, n_cols].astype(BF16)
        cb = _dot_nt(cmat, bmat)
        h_prev = h_ref[gl * nh:(gl + 1) * nh]
        y_off_all = _dot_nt(cmat, h_prev.reshape(nh * hd, SSM_STATE).astype(BF16))
    else:
        b_t = jnp.transpose(b_ref[:, n_cols]).astype(BF16)
        cb = jnp.dot(cmat, b_t, preferred_element_type=F32)
        ht_prev = ht_ref[gl]
        y_off_all = jnp.dot(cmat, ht_prev.astype(BF16), preferred_element_type=F32)

    lane = lax.broadcasted_iota(jnp.int32, (q_len, pair_w), 1)
    first_half = lane < hd

    def pair_bcast(cols, j0):
        return jnp.where(first_half,
                         jnp.broadcast_to(cols[:, j0:j0 + 1], (q_len, pair_w)),
                         jnp.broadcast_to(cols[:, j0 + 1:j0 + 2], (q_len, pair_w)))

    ys = []
    xws = []
    sumsq = jnp.zeros((q_len, 1), F32)
    for i in range(nh // 2):
        j0 = 2 * i
        x_pair = x[:, i * pair_w:(i + 1) * pair_w]
        y_diag = jnp.zeros((q_len, pair_w), F32)
        for j in (j0, j0 + 1):
            seg = jnp.broadcast_to(acum_c[:, j:j + 1], (q_len, q_len)) - src_r[j:j + 1, :]
            w = (cb * jnp.exp(jnp.where(causal, seg, NEG_INF))).astype(BF16)
            own = first_half if j == j0 else jnp.logical_not(first_half)
            y_diag = y_diag + jnp.dot(w, jnp.where(own, x_pair, 0.0).astype(BF16),
                                      preferred_element_type=F32)
        e_pair = jnp.exp(pair_bcast(acum_c, j0))
        d_pair = jnp.where(first_half[0:1, :], pc[2:3, j0:j0 + 1], pc[2:3, j0 + 1:j0 + 2])
        y = y_diag + y_off_all[:, i * pair_w:(i + 1) * pair_w] * e_pair + d_pair * x_pair
        y = y * _silu(z_ref[:, gl * gw + i * pair_w:gl * gw + (i + 1) * pair_w])
        sumsq = sumsq + jnp.sum(y * y, axis=-1, keepdims=True)
        ys.append(y)

        xw = (x_pair * pair_bcast(dd_c, j0)).astype(BF16)
        if ht_ref is None:
            s_pair = _dot_tn(xw, bmat)
            for jj, j in enumerate((j0, j0 + 1)):
                chunk_decay = jnp.exp(acum_r[j:j + 1, q_len - 1:q_len])
                h_ref[gl * nh + j] = h_prev[j] * chunk_decay + s_pair[jj * hd:(jj + 1) * hd, :]
        else:
            xws.append(xw)

    if ht_ref is not None:
        end_decay = jnp.exp(end_c)
        decay_row = jnp.concatenate(
            [jnp.where(first_half[0:1, :], end_decay[:, 2 * i:2 * i + 1], end_decay[:, 2 * i + 1:2 * i + 2])
             for i in range(nh // 2)], axis=1)
        ht_new = ht_prev * decay_row + jnp.dot(b_t, jnp.concatenate(xws, axis=1), preferred_element_type=F32)
        ht_ref[gl] = ht_new

        @pl.when(last_chunk)
        def _():
            for i in range(nh // 2):
                both = jnp.transpose(ht_new[:, i * pair_w:(i + 1) * pair_w])
                h_ref[gl * nh + 2 * i] = both[:hd, :]
                h_ref[gl * nh + 2 * i + 1] = both[hd:, :]

    inv = lax.rsqrt(sumsq * (1.0 / (nh * hd)) + EPS)
    for i in range(nh // 2):
        sl = slice(gl * gw + i * pair_w, gl * gw + (i + 1) * pair_w)
        y_ref[:, sl] = (ys[i] * inv * nrm_ref[:, sl]).astype(y_ref.dtype)


def _ssd_kernel(*refs, q_len, valid_len, has_h0, groups_per_step):
    if has_h0:
        h0_ref = refs[9]
        refs = refs[:9] + refs[10:] + (None,)
    h_ref, ht_ref = refs[-2:]
    chunk = pl.program_id(2)

    @pl.when(chunk == 0)
    def _():
        if has_h0:
            h_ref[...] = h0_ref[...]
        else:
            ht_ref[...] = jnp.zeros(ht_ref.shape, F32)

    for gl in range(groups_per_step):
        _ssd_group(gl, *refs, q_len=q_len, valid_len=valid_len,
                   last_chunk=chunk == pl.num_programs(2) - 1)


def ssd(u, dt_raw, z, dt_bias, a_log, d_skip, ssm_norm, h0, *, batch, seq, q_len, valid_len, groups_per_step):
    nh, g, gps = HEADS_PER_SSM_GROUP, SSM_GROUPS, groups_per_step
    nc = seq // q_len
    m = batch * seq
    dt4 = dt_raw[:, :SSM_HEADS].reshape(batch * nc, q_len, g, nh)
    dtc = jnp.transpose(dt4, (0, 2, 1, 3))
    dtr = jnp.transpose(dt4, (0, 2, 3, 1))
    pcol = jnp.stack([dt_bias, a_log, d_skip]).reshape(3, g, nh).transpose(1, 0, 2)
    prow = jnp.transpose(pcol, (0, 2, 1))
    b_off = D_INNER // (gps * SSM_STATE)
    c_off = b_off + g // gps
    row = lambda b, gi, c: b * nc + c
    in_specs = [
        pl.BlockSpec((q_len, gps * SSM_GROUP_WIDTH), lambda b, gi, c: (row(b, gi, c), gi)),
        pl.BlockSpec((q_len, gps * SSM_STATE), lambda b, gi, c: (row(b, gi, c), b_off + gi)),
        pl.BlockSpec((q_len, gps * SSM_STATE), lambda b, gi, c: (row(b, gi, c), c_off + gi)),
        pl.BlockSpec((None, gps, q_len, nh), lambda b, gi, c: (row(b, gi, c), gi, 0, 0)),
        pl.BlockSpec((None, gps, nh, q_len), lambda b, gi, c: (row(b, gi, c), gi, 0, 0)),
        pl.BlockSpec((q_len, gps * SSM_GROUP_WIDTH), lambda b, gi, c: (row(b, gi, c), gi)),
        pl.BlockSpec((gps, 3, nh), lambda b, gi, c: (gi, 0, 0)),
        pl.BlockSpec((gps, nh, 3), lambda b, gi, c: (gi, 0, 0)),
        pl.BlockSpec((1, gps * SSM_GROUP_WIDTH), lambda b, gi, c: (0, gi)),
    ]
    args = [u, u, u, dtc, dtr, z, pcol, prow, ssm_norm.reshape(1, D_INNER)]
    h_spec = pl.BlockSpec((None, gps * nh, SSM_HEAD_DIM, SSM_STATE), lambda b, gi, c: (b, gi, 0, 0))
    scratch = []
    if h0 is not None:
        in_specs.append(h_spec)
        args.append(h0)
    else:
        scratch.append(pltpu.VMEM((gps, SSM_STATE, SSM_GROUP_WIDTH), F32))
    return pl.pallas_call(
        functools.partial(_ssd_kernel, q_len=q_len, valid_len=valid_len, has_h0=h0 is not None,
                          groups_per_step=gps),
        grid=(batch, g // gps, nc),
        in_specs=in_specs,
        out_specs=[pl.BlockSpec((q_len, gps * SSM_GROUP_WIDTH), lambda b, gi, c: (row(b, gi, c), gi)), h_spec],
        out_shape=[jax.ShapeDtypeStruct((m, D_INNER), BF16),
                   jax.ShapeDtypeStruct((batch, SSM_HEADS, SSM_HEAD_DIM, SSM_STATE), F32)],
        scratch_shapes=scratch,
        compiler_params=_params(3),
        name="ssd",
    )(*args)


def _attn_prompt_kernel(*refs, kb, has_prev):
    if has_prev:
        q_ref, kp_ref, kc_ref, vp_ref, vc_ref, mb_ref, o_ref, l_ref = refs
    else:
        q_ref, kc_ref, vc_ref, mb_ref, o_ref, l_ref = refs
    not_first = pl.program_id(2) > 0
    scale = ATT_HEAD_DIM ** -0.5
    for h in range(ATT_HEADS_PER_GROUP):
        q = (_load_head(q_ref, h) * scale).astype(BF16)
        s_c = _dot_nt(q, _load_head(kc_ref, h).astype(BF16)) + mb_ref[h, :, kb:]
        m = jnp.max(s_c, axis=-1, keepdims=True)
        if has_prev:
            s_p = _dot_nt(q, _load_head(kp_ref, h).astype(BF16)) + mb_ref[h, :, :kb]
            s_p = jnp.where(not_first, s_p, NEG_INF)
            m = jnp.maximum(m, jnp.max(s_p, axis=-1, keepdims=True))
        p_c = jnp.exp(s_c - m)
        den = jnp.sum(p_c, axis=-1, keepdims=True)
        o = jnp.dot(p_c.astype(BF16), _load_head(vc_ref, h).astype(BF16), preferred_element_type=F32)
        if has_prev:
            p_p = jnp.exp(s_p - m)
            den = den + jnp.sum(p_p, axis=-1, keepdims=True)
            o = o + jnp.dot(p_p.astype(BF16), _load_head(vp_ref, h).astype(BF16), preferred_element_type=F32)
        _store_head(o_ref, h, o / den)
        _store_head(l_ref, h, jnp.broadcast_to(m + jnp.log(den), (kb, ATT_HEAD_DIM)))


def attn_prompt(q, kv, mbias, *, group, batch, seq, dil, kb):
    m_res = seq // dil
    nb = m_res // kb
    has_prev = nb > 1
    nh = ATT_HEADS_PER_GROUP
    q4 = q.reshape(batch * m_res, dil, q.shape[1], ATT_HEAD_DIM)
    kv4 = kv.reshape(batch * m_res, dil, 2 * nh, ATT_HEAD_DIM)

    def cur(head_block):
        return pl.BlockSpec((kb, None, nh, ATT_HEAD_DIM), lambda b, r, n: (b * nb + n, r, head_block, 0))

    def prev(head_block):
        return pl.BlockSpec((kb, None, nh, ATT_HEAD_DIM),
                            lambda b, r, n: (b * nb + jnp.maximum(n - 1, 0), r, head_block, 0))

    if has_prev:
        in_specs = [cur(group), prev(0), cur(0), prev(1), cur(1)]
        args = [q4, kv4, kv4, kv4, kv4]
    else:
        in_specs = [cur(group), cur(0), cur(1)]
        args = [q4, kv4, kv4]
    in_specs.append(pl.BlockSpec(mbias.shape, lambda b, r, n: (0, 0, 0)))
    args.append(mbias)
    o, lse = pl.pallas_call(
        functools.partial(_attn_prompt_kernel, kb=kb, has_prev=has_prev),
        grid=(batch, dil, nb),
        in_specs=in_specs,
        out_specs=[cur(0), cur(0)],
        out_shape=[jax.ShapeDtypeStruct((batch * m_res, dil, nh, ATT_HEAD_DIM), F32)] * 2,
        compiler_params=_params(3),
        name=f"attn_prompt_g{group}",
    )(*args)
    return o.reshape(batch * seq, nh, ATT_HEAD_DIM), lse.reshape(batch * seq, nh, ATT_HEAD_DIM)


def _attn_sample_kernel(*refs, n_sets, steps, q_rows):
    q_ref, new_ref = refs[:2]
    cache_refs = refs[2:2 + n_sets]
    bias_ref, o_ref, l_ref = refs[2 + n_sets:]
    gw = ATT_GROUP_WIDTH
    nh = ATT_HEADS_PER_GROUP
    hd = ATT_HEAD_DIM
    scale = ATT_HEAD_DIM ** -0.5
    bb = q_ref.shape[0]
    cache_rows = cache_refs[0].shape[1]
    new_rows = new_ref.shape[1]
    row_stride = 2 * nh
    per_set = steps // n_sets
    sub = lax.broadcasted_iota(jnp.int32, (nh, gw), 0)
    own = sub == lax.broadcasted_iota(jnp.int32, (nh, gw), 1) // hd
    for bi in range(bb):
        new_heads = [_load_rows(new_ref, bi * new_rows * 2 * nh + h, new_rows, 2 * nh) for h in range(2 * nh)]
        for ci in range(n_sets):
            q_blocks = []
            for tt in range(per_set):
                q_t = q_ref[bi, ci * per_set + tt] * scale
                q_blocks.append(jnp.where(own, jnp.concatenate([q_t] * nh, axis=1), 0.0))
            if q_rows > per_set * nh:
                q_blocks.append(jnp.zeros((q_rows - per_set * nh, gw), F32))
            qbd = jnp.concatenate(q_blocks, axis=0).astype(BF16)
            base = bi * cache_rows * row_stride
            cache_ref = cache_refs[ci]
            k_all = jnp.concatenate(
                [jnp.concatenate([_load_rows(cache_ref, base + h, cache_rows, row_stride), new_heads[h]], axis=0)
                 for h in range(nh)], axis=1).astype(BF16)
            v_all = jnp.concatenate(
                [jnp.concatenate([_load_rows(cache_ref, base + nh + h, cache_rows, row_stride), new_heads[nh + h]],
                                 axis=0) for h in range(nh)], axis=1).astype(BF16)
            s = _dot_nt(qbd, k_all) + bias_ref[ci]
            m = jnp.max(s, axis=-1, keepdims=True)
            p = jnp.exp(s - m)
            den = jnp.sum(p, axis=-1, keepdims=True)
            o_all = jnp.dot(p.astype(BF16), v_all, preferred_element_type=F32) / den
            lse = m + jnp.log(den)
            for tt in range(per_set):
                t = ci * per_set + tt
                rows = slice(tt * nh, (tt + 1) * nh)
                o_t = jnp.zeros((nh, hd), F32)
                for h in range(nh):
                    o_t = o_t + jnp.where(sub[:, :hd] == h, o_all[rows, h * hd:(h + 1) * hd], 0.0)
                o_ref[bi, t] = o_t
                l_ref[bi, t] = jnp.broadcast_to(lse[rows, :], (nh, hd))


def attn_sample(q, kv_new_pad, cache, bias, *, group, batch, steps, dil, batch_block):
    nh, hd = ATT_HEADS_PER_GROUP, ATT_HEAD_DIM
    bb = batch_block
    n_sets, q_rows, n_keys_tot = bias.shape
    new_rows = kv_new_pad.shape[1]
    cache_rows = n_keys_tot - new_rows
    q4 = q.reshape(batch, steps, q.shape[1], hd)
    if n_sets == 1:
        cache_view = cache.reshape(batch, cache_rows, 2 * nh, hd)
        cache_specs = [pl.BlockSpec((bb, cache_rows, 2 * nh, hd), lambda b: (b, 0, 0, 0))]
    else:
        cache_view = cache.reshape(batch, cache_rows, dil, 2 * nh, hd)
        cache_specs = [pl.BlockSpec((bb, cache_rows, None, 2 * nh, hd),
                                    functools.partial(lambda b, t: (b, 0, t, 0, 0), t=t))
                       for t in range(n_sets)]
    out_spec = pl.BlockSpec((bb, steps, nh, hd), lambda b: (b, 0, 0, 0))
    o, lse = pl.pallas_call(
        functools.partial(_attn_sample_kernel, n_sets=n_sets, steps=steps, q_rows=q_rows),
        grid=(batch // bb,),
        in_specs=[pl.BlockSpec((bb, steps, nh, hd), lambda b: (b, 0, group, 0)),
                  pl.BlockSpec((bb, new_rows, 2 * nh, hd), lambda b: (b, 0, 0, 0))]
                 + cache_specs
                 + [pl.BlockSpec(bias.shape, lambda b: (0, 0, 0))],
        out_specs=[out_spec, out_spec],
        out_shape=[jax.ShapeDtypeStruct((batch, steps, nh, hd), F32)] * 2,
        compiler_params=_params(1),
        name=f"attn_sample_g{group}",
    )(q4, kv_new_pad, *([cache_view] * n_sets), bias)
    return o.reshape(batch * steps, nh, hd), lse.reshape(batch * steps, nh, hd)


def _attn_combine_kernel(o0, o1, o2, l0, l1, l2, out_ref, mix_ref):
    ls = (l0[...], l1[...], l2[...])
    m = jnp.maximum(jnp.maximum(ls[0], ls[1]), ls[2])
    ws = [jnp.exp(l - m) for l in ls]
    num = ws[0] * o0[...] + ws[1] * o1[...] + ws[2] * o2[...]
    mix_ref[...] = num / (ws[0] + ws[1] + ws[2])
    for h in range(ATT_HEADS_PER_GROUP):
        out_ref[:, h * ATT_HEAD_DIM:(h + 1) * ATT_HEAD_DIM] = _load_head(mix_ref, h).astype(out_ref.dtype)


def attn_combine(outs, lses, tm=512):
    m, nh, hd = outs[0].shape
    tm = min(tm, m)
    spec = pl.BlockSpec((tm, nh, hd), lambda i: (i, 0, 0))
    return pl.pallas_call(
        _attn_combine_kernel,
        grid=(m // tm,),
        in_specs=[spec] * 6,
        out_specs=pl.BlockSpec((tm, nh * hd), lambda i: (i, 0)),
        out_shape=jax.ShapeDtypeStruct((m, nh * hd), BF16),
        scratch_shapes=[pltpu.VMEM((tm, nh, hd), F32)],
        compiler_params=_params(1),
        name="attn_combine",
    )(*outs, *lses)


def _t5_bucket(dist):
    max_exact = N_BUCKETS // 2
    d = np.maximum(dist, max_exact).astype(np.float32)
    large = max_exact + (np.log(d / max_exact) / np.log(MAX_DISTANCE / max_exact)
                         * (N_BUCKETS - max_exact)).astype(np.int32)
    large = np.minimum(large, N_BUCKETS - 1)
    return np.where(dist < max_exact, dist, large).astype(np.int32)


def _group_bias(rel_bias, g):
    win, dil = ATT_PATTERNS[g]
    buckets = _t5_bucket(np.arange(win // dil + 1) * dil)
    return rel_bias[buckets][:, g * ATT_HEADS_PER_GROUP:(g + 1) * ATT_HEADS_PER_GROUP]


def _prompt_mask_bias(bias_g, n_keys, kb):
    period = 3 * kb
    offs = np.arange(period)
    offs = np.where(offs >= 2 * kb, offs - period, offs)
    j_of = np.clip(kb - offs, 0, n_keys)
    a = jnp.transpose(bias_g[j_of]).astype(F32)
    nh = a.shape[0]
    tiled = jnp.tile(a, (1, kb))[:, :kb * (period - 1)].reshape(nh, kb, period - 1)
    rel = np.arange(kb)[:, None] + kb - np.arange(2 * kb)[None, :]
    valid = (rel >= 0) & (rel <= n_keys)
    return jnp.where(valid[None], tiled[:, :, :2 * kb], NEG_INF)


def _sample_mask_bias(bias_g, *, cache_len, dil, n_keys, steps, new_rows, shared_cache, q_rows):
    nh = ATT_HEADS_PER_GROUP
    n_cache = 1 if shared_cache else steps
    per_cache = steps // n_cache
    rows_per_cache = cache_len if shared_cache else cache_len // dil
    blocks = []
    for ci in range(n_cache):
        if shared_cache:
            cache_pos = np.arange(rows_per_cache)
        else:
            cache_pos = ci + dil * np.arange(rows_per_cache)
        key_pos = np.concatenate([cache_pos, cache_len + np.arange(new_rows)])
        key_live = np.concatenate([np.ones(rows_per_cache, bool), np.arange(new_rows) < steps])
        rows = []
        for tt in range(per_cache):
            t = ci * per_cache + tt
            diff = cache_len + t - key_pos
            valid = key_live & (diff >= 0) & (diff % dil == 0) & (diff // dil <= n_keys)
            j = np.clip(diff // dil, 0, n_keys)
            rows.append(jnp.where(valid[None, :], bias_g[j].T.astype(F32), NEG_INF))
        blk = jnp.concatenate(rows, axis=0)
        pad = q_rows - per_cache * nh
        if pad:
            blk = jnp.concatenate([blk, jnp.zeros((pad, blk.shape[1]), F32)], axis=0)
        blocks.append(blk)
    return jnp.stack(blocks)


SAMPLE_NEW_ROWS = 16
SAMPLE_Q_ROWS_MIN = 16
SSD_SAMPLE_Q = 16
SAMPLE_SHARED_BATCH_BLOCK = 4


IN_SIZES = (D_INNER, CONV_DIM, SSM_HEADS, ATT_WIDTH, ATT_WIDTH, ATT_WIDTH, D_MODEL, D_MODEL)
IN_OFFSETS = tuple(int(o) for o in np.concatenate([[0], np.cumsum(IN_SIZES)]))


def _split_w_in(w_in):
    seg = lambda i: w_in[:, IN_OFFSETS[i]:IN_OFFSETS[i + 1]]
    gw = ATT_GROUP_WIDTH
    w_kv = [jnp.concatenate([seg(4)[:, g * gw:(g + 1) * gw], seg(5)[:, g * gw:(g + 1) * gw]], axis=1).astype(BF16)
            for g in range(ATT_N_GROUPS)]
    w_gates = jnp.concatenate([seg(6), seg(7)], axis=1).astype(BF16)
    return dict(q=seg(3).astype(BF16), kv=w_kv, gates=w_gates)


def _layer(x, w, *, batch, seq, is_prompt, caches, conv_state, ssm_state, rel_bias):
    m = batch * seq
    nh, hd = ATT_HEADS_PER_GROUP, ATT_HEAD_DIM
    tm = 1024 if m % 1024 == 0 else 512
    h = rmsnorm(x, w["norm_mix"], BF16)
    proj = lambda wseg, name, **kw: matmul(h, wseg, tm=tm, tn=1024, name=name, **kw)
    z = proj(w["in_full"], "in_z", n=IN_SIZES[0], col0=IN_OFFSETS[0])
    dt_raw = matmul(h, w["in_full"], tm=tm, tn=LANES, n=LANES, col0=IN_OFFSETS[2], name="in_dt")
    q = proj(w["in"]["q"], "in_q", epilogue="heads")
    kvs = [proj(w["in"]["kv"][g], f"in_kv{g}", epilogue="heads")
           for g in range(ATT_N_GROUPS)]
    gates = proj(w["in"]["gates"], "in_gates")

    if is_prompt:
        u, xbc_tail = proj(w["in_full"], "in_xbc_conv", n=IN_SIZES[1], col0=IN_OFFSETS[1],
                           epilogue="conv_silu", conv=(w["conv_w"], w["conv_b"]), seq=seq)
        new_conv = xbc_tail[:, SUBLANES - (CONV_WIDTH - 1):]
        y_ssm, new_ssm = ssd(u, dt_raw, z, w["dt_bias"], w["a_log"], w["d_skip"], w["ssm_norm"], None,
                             batch=batch, seq=seq, q_len=SSD_CHUNK, valid_len=SSD_CHUNK, groups_per_step=2)
    else:
        xbc = proj(w["in_full"], "in_xbc", n=IN_SIZES[1], col0=IN_OFFSETS[1])
        xpad = jnp.concatenate([conv_state.reshape(batch, (CONV_WIDTH - 1) * CONV_DIM),
                                xbc.reshape(batch, seq * CONV_DIM)], axis=1)
        u = conv_sample(xpad, w["conv_w"], w["conv_b"], steps=seq)
        new_conv = xpad.reshape(batch, CONV_WIDTH - 1 + seq, CONV_DIM)[:, seq:]
        qp = SSD_SAMPLE_Q
        pad_rows = lambda a: jnp.pad(a.reshape(batch, seq, a.shape[-1]),
                                     ((0, 0), (0, qp - seq), (0, 0))).reshape(batch * qp, a.shape[-1])
        y_pad, new_ssm = ssd(pad_rows(u.reshape(m, CONV_DIM)), pad_rows(dt_raw), pad_rows(z),
                             w["dt_bias"], w["a_log"], w["d_skip"], w["ssm_norm"], ssm_state,
                             batch=batch, seq=qp, q_len=qp, valid_len=seq, groups_per_step=SSM_GROUPS)
        y_ssm = y_pad.reshape(batch, qp, D_INNER)[:, :seq].reshape(m, D_INNER)

    outs, lses, new_rows = [], [], []
    for g, (win, dil) in enumerate(ATT_PATTERNS):
        n_keys = win // dil
        bias_g = _group_bias(rel_bias, g)
        rows = kvs[g].reshape(batch, seq, 2, nh, hd)
        if is_prompt:
            mb = _prompt_mask_bias(bias_g, n_keys, n_keys)
            o, lse = attn_prompt(q, kvs[g], mb, group=g, batch=batch, seq=seq, dil=dil, kb=n_keys)
            keep = min(win, seq)
            rows = rows[:, seq - keep:]
        else:
            cache = caches[g]
            cache_len = cache.shape[1]
            shared = dil == 1
            q_rows = max(SAMPLE_Q_ROWS_MIN, (seq if shared else 1) * nh)
            sb = _sample_mask_bias(bias_g, cache_len=cache_len, dil=dil, n_keys=n_keys, steps=seq,
                                   new_rows=SAMPLE_NEW_ROWS, shared_cache=shared, q_rows=q_rows)
            kv_new = jnp.pad(kvs[g].reshape(batch, seq, 2 * nh, hd),
                             ((0, 0), (0, SAMPLE_NEW_ROWS - seq), (0, 0), (0, 0)))
            o, lse = attn_sample(q, kv_new, cache, sb, group=g, batch=batch, steps=seq, dil=dil,
                                 batch_block=SAMPLE_SHARED_BATCH_BLOCK if shared else 1)
        outs.append(o)
        lses.append(lse)
        new_rows.append(rows)
    o_att = attn_combine(outs, lses)

    merged = merge_proj(y_ssm, o_att, gates, w["ssm_proj"], w["att_proj"], tm=512, tn=512)
    x1 = matmul(merged, w["out"], tm=tm, tn=1024, epilogue="residual", residual=x, name="out_proj")
    h2 = rmsnorm(x1, w["norm_mlp"], BF16)
    up = matmul(h2, w["up"], tm=tm, tn=1024, out_dtype=BF16, epilogue="relu2", name="mlp_up")
    x2 = matmul(up, w["down"], tm=512, tn=512, epilogue="residual", residual=x1, name="mlp_down")
    y = rmsnorm(x2, w["norm_final"], F32)
    return y, new_rows, new_conv, new_ssm


def kernel(x_prompt, x_sample, cache_win128, cache_win512, cache_win2048, state_conv, state_ssm, w_in, conv_w, conv_b, dt_bias, a_log, d_skip, ssm_norm, w_ssm_proj, w_att_proj, w_out, norm_mix, w_up, w_down, norm_mlp, rel_bias, norm_final):
    assert w_in.shape[0] == 1, "single-layer model"
    bp, sp, _ = x_prompt.shape
    bs, ss, _ = x_sample.shape
    w = dict(
        **{"in": _split_w_in(w_in[0])}, in_full=w_in[0],
        conv_w=conv_w[0], conv_b=conv_b[0], dt_bias=dt_bias[0], a_log=a_log[0], d_skip=d_skip[0],
        ssm_norm=ssm_norm[0], ssm_proj=w_ssm_proj[0], att_proj=w_att_proj[0],
        out=w_out[0], norm_mix=norm_mix[0], up=w_up[0],
        down=w_down[0].astype(BF16), norm_mlp=norm_mlp[0], norm_final=norm_final)

    yp, rows_p, conv_p, ssm_p = _layer(
        x_prompt.reshape(bp * sp, D_MODEL), w, batch=bp, seq=sp, is_prompt=True,
        caches=None, conv_state=None, ssm_state=None, rel_bias=rel_bias)
    ys, rows_s, conv_s, ssm_s = _layer(
        x_sample.reshape(bs * ss, D_MODEL), w, batch=bs, seq=ss, is_prompt=False,
        caches=(cache_win128[0], cache_win512[0], cache_win2048[0]),
        conv_state=state_conv[0], ssm_state=state_ssm[0], rel_bias=rel_bias)

    return (yp.reshape(bp, sp, D_MODEL), ys.reshape(bs, ss, D_MODEL),
            rows_p[0][None], rows_p[1][None], rows_p[2][None], conv_p[None], ssm_p[None],
            rows_s[0][None], rows_s[1][None], rows_s[2][None], conv_s[None], ssm_s[None])
```

```python
import functools
import math

import numpy as np
import jax
import jax.numpy as jnp
from jax import lax
from jax.experimental import pallas as pl
from jax.experimental.pallas import tpu as pltpu

D_MODEL = 2048
D_INNER = 2 * D_MODEL
SSM_HEAD_DIM = 64
SSM_HEADS = D_INNER // SSM_HEAD_DIM
SSM_GROUPS = 8
HEADS_PER_SSM_GROUP = SSM_HEADS // SSM_GROUPS
SSM_STATE = 128
SSM_GROUP_WIDTH = D_INNER // SSM_GROUPS
CONV_WIDTH = 4
CONV_DIM = D_INNER + 2 * SSM_GROUPS * SSM_STATE
SSD_CHUNK = 128
ATT_HEAD_DIM = 128
ATT_HEADS_PER_GROUP = 8
ATT_PATTERNS = ((128, 1), (512, 4), (2048, 16))
ATT_N_GROUPS = len(ATT_PATTERNS)
ATT_GROUP_WIDTH = ATT_HEADS_PER_GROUP * ATT_HEAD_DIM
ATT_WIDTH = ATT_N_GROUPS * ATT_GROUP_WIDTH
N_BUCKETS = 32
MAX_DISTANCE = max(w for w, _ in ATT_PATTERNS)
D_FF = 4 * D_MODEL
EPS = 1e-6

LANES = 128
SUBLANES = 8
VMEM_LIMIT_BYTES = 56 * 1024 * 1024

F32 = jnp.float32
BF16 = jnp.bfloat16
NEG_INF = float("-inf")


def _params(n_grid_dims):
    return pltpu.CompilerParams(
        dimension_semantics=("arbitrary",) * n_grid_dims,
        vmem_limit_bytes=VMEM_LIMIT_BYTES)


def _sigmoid(x):
    return 0.5 * jnp.tanh(0.5 * x) + 0.5


def _silu(x):
    return x * _sigmoid(x)


def _softplus(x):
    return jnp.maximum(x, 0.0) + jnp.log1p(jnp.exp(-jnp.abs(x)))


def _dot_nt(a, b):
    return lax.dot_general(a, b, (((1,), (1,)), ((), ())), preferred_element_type=F32)


def _dot_tn(a, b):
    return lax.dot_general(a, b, (((0,), (0,)), ((), ())), preferred_element_type=F32)


def _load_rows(ref, offset, rows, stride):
    flat = ref.reshape(math.prod(ref.shape[:-1]), ref.shape[-1])
    return flat[pl.ds(offset, rows, stride=stride), :]


def _store_rows(ref, offset, rows, stride, value):
    flat = ref.reshape(math.prod(ref.shape[:-1]), ref.shape[-1])
    flat[pl.ds(offset, rows, stride=stride), :] = value


def _load_head(ref, h):
    return _load_rows(ref, h, ref.shape[0], ref.shape[1])


def _store_head(ref, h, value):
    _store_rows(ref, h, ref.shape[0], ref.shape[1], value)


def _rmsnorm_kernel(x_ref, g_ref, o_ref):
    x = x_ref[...]
    ms = jnp.mean(x * x, axis=-1, keepdims=True)
    o_ref[...] = (x * lax.rsqrt(ms + EPS) * g_ref[...]).astype(o_ref.dtype)


def rmsnorm(x, g, out_dtype, tm=512):
    m, d = x.shape
    tm = min(tm, m)
    return pl.pallas_call(
        _rmsnorm_kernel,
        grid=(m // tm,),
        in_specs=[pl.BlockSpec((tm, d), lambda i: (i, 0)),
                  pl.BlockSpec((1, d), lambda i: (0, 0))],
        out_specs=pl.BlockSpec((tm, d), lambda i: (i, 0)),
        out_shape=jax.ShapeDtypeStruct((m, d), out_dtype),
        compiler_params=_params(1),
        name="rmsnorm",
    )(x, g.reshape(1, d))


def _bf16_weights(w_ref, wbf_ref):
    if wbf_ref is None:
        return w_ref

    @pl.when(pl.program_id(1) == 0)
    def _():
        wbf_ref[...] = w_ref[...].astype(BF16)

    return wbf_ref


CONV_COL_CHUNK = 256


def _tile_dot(a, w_ref, cols, w_transposed):
    if w_transposed:
        return _dot_nt(a, w_ref[cols, :])
    return jnp.dot(a, w_ref[:, cols], preferred_element_type=F32)


def _conv_silu_tiles(a_ref, w_ref, cw_ref, cb_ref, o_ref, tail_ref, pad_ref, *, seq_tiles, w_transposed):
    tm = a_ref.shape[0]
    first = SUBLANES - (CONV_WIDTH - 1)

    @pl.when(pl.program_id(1) % seq_tiles == 0)
    def _():
        pad_ref[0:SUBLANES, :] = jnp.zeros((SUBLANES, pad_ref.shape[1]), F32)

    a = a_ref[...]
    for c0 in range(0, o_ref.shape[1], CONV_COL_CHUNK):
        cols = slice(c0, c0 + CONV_COL_CHUNK)
        acc = _tile_dot(a, w_ref, cols, w_transposed)
        pad_ref[SUBLANES:SUBLANES + tm, cols] = acc
        conv = cb_ref[:, cols] + pad_ref[first:first + tm, cols] * cw_ref[0:1, cols]
        for i in range(1, CONV_WIDTH):
            conv = conv + pad_ref[first + i:first + i + tm, cols] * cw_ref[i:i + 1, cols]
        o_ref[:, cols] = _silu(conv)
        last_rows = acc[tm - SUBLANES:, :]
        pad_ref[0:SUBLANES, cols] = last_rows
        tail_ref[:, cols] = last_rows


def _mm_kernel(*refs, epilogue, cast_w, seq_tiles, w_transposed):
    refs = list(refs)
    a_ref, w_ref = refs[:2]
    wbf_ref = None
    if epilogue == "conv_silu":
        pad_ref = refs.pop()
    if cast_w:
        wbf_ref = refs.pop()
    rest = refs[2:]
    w_ref = _bf16_weights(w_ref, wbf_ref)
    if epilogue == "conv_silu":
        _conv_silu_tiles(a_ref, w_ref, *rest, pad_ref, seq_tiles=seq_tiles, w_transposed=w_transposed)
        return
    acc = _tile_dot(a_ref[...], w_ref, slice(None), w_transposed)
    if epilogue == "residual":
        r_ref, o_ref = rest
        o_ref[...] = r_ref[...] + acc
    elif epilogue == "relu2":
        (o_ref,) = rest
        u = jnp.maximum(acc, 0.0)
        o_ref[...] = (u * u).astype(o_ref.dtype)
    elif epilogue == "heads":
        (o_ref,) = rest
        for j in range(o_ref.shape[1]):
            _store_head(o_ref, j, acc[:, j * LANES:(j + 1) * LANES])
    else:
        (o_ref,) = rest
        o_ref[...] = acc.astype(o_ref.dtype)


W_ROW_ALIGN = 64


def matmul(a, w, *, tm, tn, n=None, w_rows=None, out_dtype=F32, epilogue="none", residual=None,
           conv=None, seq=None, name="matmul"):
    m, k = a.shape
    n = w.shape[1] if n is None else n
    tm = min(tm, m)
    tn = min(tn, n)
    assert m % tm == 0 and n % tn == 0
    cast_w = w.dtype != BF16
    if w_rows is None:
        w_spec = pl.BlockSpec((k, tn), lambda j, i: (0, j))
        w_tile = (k, tn)
    else:
        w_spec = pl.BlockSpec((pl.Element(tn), pl.Element(k)),
                              lambda j, i: (pl.multiple_of(w_rows(j), W_ROW_ALIGN), 0))
        w_tile = (tn, k)
    in_specs = [pl.BlockSpec((tm, k), lambda j, i: (i, 0)), w_spec]
    args = [a, w]
    scratch = [pltpu.VMEM(w_tile, BF16)] if cast_w else []
    out_spec = pl.BlockSpec((tm, tn), lambda j, i: (i, j))
    out_shape = jax.ShapeDtypeStruct((m, n), out_dtype)
    seq_tiles = None
    if epilogue == "residual":
        in_specs.append(pl.BlockSpec((tm, tn), lambda j, i: (i, j)))
        args.append(residual)
    elif epilogue == "heads":
        out_spec = pl.BlockSpec((tm, tn // LANES, LANES), lambda j, i: (i, j, 0))
        out_shape = jax.ShapeDtypeStruct((m, n // LANES, LANES), out_dtype)
    elif epilogue == "conv_silu":
        assert seq % tm == 0
        seq_tiles = seq // tm
        conv_w, conv_b = conv
        in_specs += [pl.BlockSpec((CONV_WIDTH, tn), lambda j, i: (0, j)),
                     pl.BlockSpec((1, tn), lambda j, i: (0, j))]
        args += [conv_w, conv_b.reshape(1, n)]
        out_spec = [out_spec, pl.BlockSpec((None, SUBLANES, tn), lambda j, i: (i // seq_tiles, 0, j))]
        out_shape = [out_shape, jax.ShapeDtypeStruct((m // seq, SUBLANES, n), F32)]
        scratch.append(pltpu.VMEM((tm + SUBLANES, tn), F32))
    return pl.pallas_call(
        functools.partial(_mm_kernel, epilogue=epilogue, cast_w=cast_w, seq_tiles=seq_tiles,
                          w_transposed=w_rows is not None),
        grid=(n // tn, m // tm),
        in_specs=in_specs,
        out_specs=out_spec,
        out_shape=out_shape,
        scratch_shapes=scratch,
        compiler_params=_params(2),
        name=name,
    )(*args)


def _merge_kernel(y_ref, o_ref, gs_ref, ga_ref, ws_ref, wa_ref, out_ref, wsbf_ref=None, wabf_ref=None):
    ps = jnp.dot(y_ref[...], _bf16_weights(ws_ref, wsbf_ref)[...], preferred_element_type=F32)
    pa = jnp.dot(o_ref[...], _bf16_weights(wa_ref, wabf_ref)[...], preferred_element_type=F32)
    out_ref[...] = (_sigmoid(gs_ref[...]) * ps + _sigmoid(ga_ref[...]) * pa).astype(out_ref.dtype)


def merge_proj(y_ssm, o_att, gates, w_ssm, w_att, *, tm, tn):
    m = y_ssm.shape[0]
    n = w_ssm.shape[1]
    tm = min(tm, m)
    n_col = n // tn
    scratch = []
    if w_ssm.dtype != BF16:
        scratch = [pltpu.VMEM((w_ssm.shape[0], tn), BF16), pltpu.VMEM((w_att.shape[0], tn), BF16)]
    return pl.pallas_call(
        _merge_kernel,
        grid=(n_col, m // tm),
        in_specs=[pl.BlockSpec((tm, y_ssm.shape[1]), lambda j, i: (i, 0)),
                  pl.BlockSpec((tm, o_att.shape[1]), lambda j, i: (i, 0)),
                  pl.BlockSpec((tm, tn), lambda j, i: (i, j)),
                  pl.BlockSpec((tm, tn), lambda j, i: (i, j + n_col)),
                  pl.BlockSpec((w_ssm.shape[0], tn), lambda j, i: (0, j)),
                  pl.BlockSpec((w_att.shape[0], tn), lambda j, i: (0, j))],
        out_specs=pl.BlockSpec((tm, tn), lambda j, i: (i, j)),
        out_shape=jax.ShapeDtypeStruct((m, n), BF16),
        scratch_shapes=scratch,
        compiler_params=_params(2),
        name="merge_proj",
    )(y_ssm, o_att, gates, gates, w_ssm, w_att)


def _conv_sample_kernel(x0_ref, x1_ref, x2_ref, x3_ref, w_ref, b_ref, o_ref):
    acc = b_ref[...] + x0_ref[...] * w_ref[0:1, :]
    for i, x_ref in enumerate((x1_ref, x2_ref, x3_ref), start=1):
        acc = acc + x_ref[...] * w_ref[i:i + 1, :]
    o_ref[...] = _silu(acc)


def conv_sample(xpad, conv_w, conv_b, *, steps, tc=512):
    bsz = xpad.shape[0]
    c = conv_w.shape[1]
    nct = c // tc
    x_specs = [pl.BlockSpec((bsz, tc), functools.partial(lambda t, j, i: (0, (t + i) * nct + j), i=i))
               for i in range(CONV_WIDTH)]
    return pl.pallas_call(
        _conv_sample_kernel,
        grid=(steps, nct),
        in_specs=x_specs + [pl.BlockSpec((CONV_WIDTH, tc), lambda t, j: (0, j)),
                            pl.BlockSpec((1, tc), lambda t, j: (0, j))],
        out_specs=pl.BlockSpec((bsz, tc), lambda t, j: (0, t * nct + j)),
        out_shape=jax.ShapeDtypeStruct((bsz, steps * c), F32),
        compiler_params=_params(2),
        name="conv_sample",
    )(xpad, xpad, xpad, xpad, conv_w, conv_b.reshape(1, c))


def _split3(a):
    hi = a.astype(BF16)
    rest = a - hi.astype(F32)
    mid = rest.astype(BF16)
    return hi, mid, (rest - mid.astype(F32)).astype(BF16)


def _ssd_group(gl, x_ref, b_ref, c_ref, dtc_ref, dtr_ref, z_ref, pc_ref, pr_ref, nrm_ref, y_ref, h_ref,
               ht_ref, *, q_len, valid_len, last_chunk):
    nh, hd = HEADS_PER_SSM_GROUP, SSM_HEAD_DIM
    pair_w = 2 * hd
    gw = SSM_GROUP_WIDTH
    x_cols = slice(gl * gw, (gl + 1) * gw)
    n_cols = slice(gl * SSM_STATE, (gl + 1) * SSM_STATE)

    pc = pc_ref[gl]
    pr = pr_ref[gl]
    dt_c = _softplus(dtc_ref[gl] + pc[0:1, :])
    dt_r = _softplus(dtr_ref[gl] + pr[:, 0:1])
    if valid_len < q_len:
        dt_c = jnp.where(lax.broadcasted_iota(jnp.int32, dt_c.shape, 0) < valid_len, dt_c, 0.0)
        dt_r = jnp.where(lax.broadcasted_iota(jnp.int32, dt_r.shape, 1) < valid_len, dt_r, 0.0)
    a_c = dt_c * (-jnp.exp(pc[1:2, :]))
    a_r = dt_r * (-jnp.exp(pr[:, 1:2]))
    ti = lax.broadcasted_iota(jnp.int32, (q_len, q_len), 0)
    tj = lax.broadcasted_iota(jnp.int32, (q_len, q_len), 1)
    causal = ti >= tj
    tril = causal.astype(BF16)
    triu = (ti <= tj).astype(BF16)
    acum_c = sum(jnp.dot(tril, part, preferred_element_type=F32) for part in _split3(a_c))
    acum_r = sum(jnp.dot(part, triu, preferred_element_type=F32) for part in _split3(a_r))
    end_c = acum_c[q_len - 1:q_len, :]
    dd_c = jnp.exp(end_c - acum_c) * dt_c
    src_r = acum_r - jnp.log(dt_r)

    x = x_ref[:, x_cols]
    cmat = c_ref[:, n_cols].astype(BF16)
    if ht_ref is None:
        bmat = b_ref[:, n_cols].astype(BF16)
        cb = _dot_nt(cmat, bmat)
        h_prev = h_ref[gl * nh:(gl + 1) * nh]
        y_off_all = _dot_nt(cmat, h_prev.reshape(nh * hd, SSM_STATE).astype(BF16))
    else:
        b_t = jnp.transpose(b_ref[:, n_cols]).astype(BF16)
        cb = jnp.dot(cmat, b_t, preferred_element_type=F32)
        ht_prev = ht_ref[gl]
        y_off_all = jnp.dot(cmat, ht_prev.astype(BF16), preferred_element_type=F32)

    lane = lax.broadcasted_iota(jnp.int32, (q_len, pair_w), 1)
    first_half = lane < hd

    def pair_bcast(cols, j0):
        return jnp.where(first_half,
                         jnp.broadcast_to(cols[:, j0:j0 + 1], (q_len, pair_w)),
                         jnp.broadcast_to(cols[:, j0 + 1:j0 + 2], (q_len, pair_w)))

    ys = []
    xws = []
    sumsq = jnp.zeros((q_len, 1), F32)
    for i in range(nh // 2):
        j0 = 2 * i
        x_pair = x[:, i * pair_w:(i + 1) * pair_w]
        y_diag = jnp.zeros((q_len, pair_w), F32)
        for j in (j0, j0 + 1):
            seg = jnp.broadcast_to(acum_c[:, j:j + 1], (q_len, q_len)) - src_r[j:j + 1, :]
            w = (cb * jnp.exp(jnp.where(causal, seg, NEG_INF))).astype(BF16)
            own = first_half if j == j0 else jnp.logical_not(first_half)
            y_diag = y_diag + jnp.dot(w, jnp.where(own, x_pair, 0.0).astype(BF16),
                                      preferred_element_type=F32)
        e_pair = jnp.exp(pair_bcast(acum_c, j0))
        d_pair = jnp.where(first_half[0:1, :], pc[2:3, j0:j0 + 1], pc[2:3, j0 + 1:j0 + 2])
        y = y_diag + y_off_all[:, i * pair_w:(i + 1) * pair_w] * e_pair + d_pair * x_pair
        y = y * _silu(z_ref[:, gl * gw + i * pair_w:gl * gw + (i + 1) * pair_w])
        sumsq = sumsq + jnp.sum(y * y, axis=-1, keepdims=True)
        ys.append(y)

        xw = (x_pair * pair_bcast(dd_c, j0)).astype(BF16)
        if ht_ref is None:
            s_pair = _dot_tn(xw, bmat)
            for jj, j in enumerate((j0, j0 + 1)):
                chunk_decay = jnp.exp(acum_r[j:j + 1, q_len - 1:q_len])
                h_ref[gl * nh + j] = h_prev[j] * chunk_decay + s_pair[jj * hd:(jj + 1) * hd, :]
        else:
            xws.append(xw)

    if ht_ref is not None:
        end_decay = jnp.exp(end_c)
        decay_row = jnp.concatenate(
            [jnp.where(first_half[0:1, :], end_decay[:, 2 * i:2 * i + 1], end_decay[:, 2 * i + 1:2 * i + 2])
             for i in range(nh // 2)], axis=1)
        ht_new = ht_prev * decay_row + jnp.dot(b_t, jnp.concatenate(xws, axis=1), preferred_element_type=F32)
        ht_ref[gl] = ht_new

        @pl.when(last_chunk)
        def _():
            for i in range(nh // 2):
                both = jnp.transpose(ht_new[:, i * pair_w:(i + 1) * pair_w])
                h_ref[gl * nh + 2 * i] = both[:hd, :]
                h_ref[gl * nh + 2 * i + 1] = both[hd:, :]

    inv = lax.rsqrt(sumsq * (1.0 / (nh * hd)) + EPS)
    for i in range(nh // 2):
        sl = slice(gl * gw + i * pair_w, gl * gw + (i + 1) * pair_w)
        y_ref[:, sl] = (ys[i] * inv * nrm_ref[:, sl]).astype(y_ref.dtype)


def _ssd_kernel(*refs, q_len, valid_len, has_h0, groups_per_step):
    if has_h0:
        h0_ref = refs[9]
        refs = refs[:9] + refs[10:] + (None,)
    h_ref, ht_ref = refs[-2:]
    chunk = pl.program_id(2)

    @pl.when(chunk == 0)
    def _():
        if has_h0:
            h_ref[...] = h0_ref[...]
        else:
            ht_ref[...] = jnp.zeros(ht_ref.shape, F32)

    for gl in range(groups_per_step):
        _ssd_group(gl, *refs, q_len=q_len, valid_len=valid_len,
                   last_chunk=chunk == pl.num_programs(2) - 1)


def ssd(u, dt_raw, z, dt_bias, a_log, d_skip, ssm_norm, h0, *, batch, seq, q_len, valid_len, groups_per_step):
    nh, g, gps = HEADS_PER_SSM_GROUP, SSM_GROUPS, groups_per_step
    nc = seq // q_len
    m = batch * seq
    dt4 = dt_raw[:, :SSM_HEADS].reshape(batch * nc, q_len, g, nh)
    dtc = jnp.transpose(dt4, (0, 2, 1, 3))
    dtr = jnp.transpose(dt4, (0, 2, 3, 1))
    pcol = jnp.stack([dt_bias, a_log, d_skip]).reshape(3, g, nh).transpose(1, 0, 2)
    prow = jnp.transpose(pcol, (0, 2, 1))
    b_off = D_INNER // (gps * SSM_STATE)
    c_off = b_off + g // gps
    row = lambda b, gi, c: b * nc + c
    in_specs = [
        pl.BlockSpec((q_len, gps * SSM_GROUP_WIDTH), lambda b, gi, c: (row(b, gi, c), gi)),
        pl.BlockSpec((q_len, gps * SSM_STATE), lambda b, gi, c: (row(b, gi, c), b_off + gi)),
        pl.BlockSpec((q_len, gps * SSM_STATE), lambda b, gi, c: (row(b, gi, c), c_off + gi)),
        pl.BlockSpec((None, gps, q_len, nh), lambda b, gi, c: (row(b, gi, c), gi, 0, 0)),
        pl.BlockSpec((None, gps, nh, q_len), lambda b, gi, c: (row(b, gi, c), gi, 0, 0)),
        pl.BlockSpec((q_len, gps * SSM_GROUP_WIDTH), lambda b, gi, c: (row(b, gi, c), gi)),
        pl.BlockSpec((gps, 3, nh), lambda b, gi, c: (gi, 0, 0)),
        pl.BlockSpec((gps, nh, 3), lambda b, gi, c: (gi, 0, 0)),
        pl.BlockSpec((1, gps * SSM_GROUP_WIDTH), lambda b, gi, c: (0, gi)),
    ]
    args = [u, u, u, dtc, dtr, z, pcol, prow, ssm_norm.reshape(1, D_INNER)]
    h_spec = pl.BlockSpec((None, gps * nh, SSM_HEAD_DIM, SSM_STATE), lambda b, gi, c: (b, gi, 0, 0))
    scratch = []
    if h0 is not None:
        in_specs.append(h_spec)
        args.append(h0)
    else:
        scratch.append(pltpu.VMEM((gps, SSM_STATE, SSM_GROUP_WIDTH), F32))
    return pl.pallas_call(
        functools.partial(_ssd_kernel, q_len=q_len, valid_len=valid_len, has_h0=h0 is not None,
                          groups_per_step=gps),
        grid=(batch, g // gps, nc),
        in_specs=in_specs,
        out_specs=[pl.BlockSpec((q_len, gps * SSM_GROUP_WIDTH), lambda b, gi, c: (row(b, gi, c), gi)), h_spec],
        out_shape=[jax.ShapeDtypeStruct((m, D_INNER), BF16),
                   jax.ShapeDtypeStruct((batch, SSM_HEADS, SSM_HEAD_DIM, SSM_STATE), F32)],
        scratch_shapes=scratch,
        compiler_params=_params(3),
        name="ssd",
    )(*args)


def _attn_prompt_kernel(*refs, kb, has_prev):
    if has_prev:
        q_ref, kp_ref, kc_ref, vp_ref, vc_ref, mb_ref, o_ref, l_ref = refs
    else:
        q_ref, kc_ref, vc_ref, mb_ref, o_ref, l_ref = refs
    not_first = pl.program_id(2) > 0
    scale = ATT_HEAD_DIM ** -0.5
    for h in range(ATT_HEADS_PER_GROUP):
        q = (_load_head(q_ref, h) * scale).astype(BF16)
        s_c = _dot_nt(q, _load_head(kc_ref, h).astype(BF16)) + mb_ref[h, :, kb:]
        m = jnp.max(s_c, axis=-1, keepdims=True)
        if has_prev:
            s_p = _dot_nt(q, _load_head(kp_ref, h).astype(BF16)) + mb_ref[h, :, :kb]
            s_p = jnp.where(not_first, s_p, NEG_INF)
            m = jnp.maximum(m, jnp.max(s_p, axis=-1, keepdims=True))
        p_c = jnp.exp(s_c - m)
        den = jnp.sum(p_c, axis=-1, keepdims=True)
        o = jnp.dot(p_c.astype(BF16), _load_head(vc_ref, h).astype(BF16), preferred_element_type=F32)
        if has_prev:
            p_p = jnp.exp(s_p - m)
            den = den + jnp.sum(p_p, axis=-1, keepdims=True)
            o = o + jnp.dot(p_p.astype(BF16), _load_head(vp_ref, h).astype(BF16), preferred_element_type=F32)
        _store_head(o_ref, h, o / den)
        _store_head(l_ref, h, jnp.broadcast_to(m + jnp.log(den), (kb, ATT_HEAD_DIM)))


def attn_prompt(q, kv, mbias, *, group, batch, seq, dil, kb):
    m_res = seq // dil
    nb = m_res // kb
    has_prev = nb > 1
    nh = ATT_HEADS_PER_GROUP
    q4 = q.reshape(batch * m_res, dil, q.shape[1], ATT_HEAD_DIM)
    kv4 = kv.reshape(batch * m_res, dil, 2 * nh, ATT_HEAD_DIM)

    def cur(head_block):
        return pl.BlockSpec((kb, None, nh, ATT_HEAD_DIM), lambda b, r, n: (b * nb + n, r, head_block, 0))

    def prev(head_block):
        return pl.BlockSpec((kb, None, nh, ATT_HEAD_DIM),
                            lambda b, r, n: (b * nb + jnp.maximum(n - 1, 0), r, head_block, 0))

    if has_prev:
        in_specs = [cur(group), prev(0), cur(0), prev(1), cur(1)]
        args = [q4, kv4, kv4, kv4, kv4]
    else:
        in_specs = [cur(group), cur(0), cur(1)]
        args = [q4, kv4, kv4]
    in_specs.append(pl.BlockSpec(mbias.shape, lambda b, r, n: (0, 0, 0)))
    args.append(mbias)
    o, lse = pl.pallas_call(
        functools.partial(_attn_prompt_kernel, kb=kb, has_prev=has_prev),
        grid=(batch, dil, nb),
        in_specs=in_specs,
        out_specs=[cur(0), cur(0)],
        out_shape=[jax.ShapeDtypeStruct((batch * m_res, dil, nh, ATT_HEAD_DIM), F32)] * 2,
        compiler_params=_params(3),
        name=f"attn_prompt_g{group}",
    )(*args)
    return o.reshape(batch * seq, nh, ATT_HEAD_DIM), lse.reshape(batch * seq, nh, ATT_HEAD_DIM)


def _attn_sample_kernel(*refs, n_sets, steps, q_rows):
    q_ref, new_ref = refs[:2]
    cache_refs = refs[2:2 + n_sets]
    bias_ref, o_ref, l_ref = refs[2 + n_sets:]
    gw = ATT_GROUP_WIDTH
    nh = ATT_HEADS_PER_GROUP
    hd = ATT_HEAD_DIM
    scale = ATT_HEAD_DIM ** -0.5
    bb = q_ref.shape[0]
    cache_rows = cache_refs[0].shape[1]
    new_rows = new_ref.shape[1]
    row_stride = 2 * nh
    per_set = steps // n_sets
    sub = lax.broadcasted_iota(jnp.int32, (nh, gw), 0)
    own = sub == lax.broadcasted_iota(jnp.int32, (nh, gw), 1) // hd
    for bi in range(bb):
        new_heads = [_load_rows(new_ref, bi * new_rows * 2 * nh + h, new_rows, 2 * nh) for h in range(2 * nh)]
        for ci in range(n_sets):
            q_blocks = []
            for tt in range(per_set):
                q_t = q_ref[bi, ci * per_set + tt] * scale
                q_blocks.append(jnp.where(own, jnp.concatenate([q_t] * nh, axis=1), 0.0))
            if q_rows > per_set * nh:
                q_blocks.append(jnp.zeros((q_rows - per_set * nh, gw), F32))
            qbd = jnp.concatenate(q_blocks, axis=0).astype(BF16)
            base = bi * cache_rows * row_stride
            cache_ref = cache_refs[ci]
            k_all = jnp.concatenate(
                [jnp.concatenate([_load_rows(cache_ref, base + h, cache_rows, row_stride), new_heads[h]], axis=0)
                 for h in range(nh)], axis=1).astype(BF16)
            v_all = jnp.concatenate(
                [jnp.concatenate([_load_rows(cache_ref, base + nh + h, cache_rows, row_stride), new_heads[nh + h]],
                                 axis=0) for h in range(nh)], axis=1).astype(BF16)
            s = _dot_nt(qbd, k_all) + bias_ref[ci]
            m = jnp.max(s, axis=-1, keepdims=True)
            p = jnp.exp(s - m)
            den = jnp.sum(p, axis=-1, keepdims=True)
            o_all = jnp.dot(p.astype(BF16), v_all, preferred_element_type=F32) / den
            lse = m + jnp.log(den)
            for tt in range(per_set):
                t = ci * per_set + tt
                rows = slice(tt * nh, (tt + 1) * nh)
                o_t = jnp.zeros((nh, hd), F32)
                for h in range(nh):
                    o_t = o_t + jnp.where(sub[:, :hd] == h, o_all[rows, h * hd:(h + 1) * hd], 0.0)
                o_ref[bi, t] = o_t
                l_ref[bi, t] = jnp.broadcast_to(lse[rows, :], (nh, hd))


def attn_sample(q, kv_new_pad, cache, bias, *, group, batch, steps, dil, batch_block):
    nh, hd = ATT_HEADS_PER_GROUP, ATT_HEAD_DIM
    bb = batch_block
    n_sets, q_rows, n_keys_tot = bias.shape
    new_rows = kv_new_pad.shape[1]
    cache_rows = n_keys_tot - new_rows
    q4 = q.reshape(batch, steps, q.shape[1], hd)
    if n_sets == 1:
        cache_view = cache.reshape(batch, cache_rows, 2 * nh, hd)
        cache_specs = [pl.BlockSpec((bb, cache_rows, 2 * nh, hd), lambda b: (b, 0, 0, 0))]
    else:
        cache_view = cache.reshape(batch, cache_rows, dil, 2 * nh, hd)
        cache_specs = [pl.BlockSpec((bb, cache_rows, None, 2 * nh, hd),
                                    functools.partial(lambda b, t: (b, 0, t, 0, 0), t=t))
                       for t in range(n_sets)]
    out_spec = pl.BlockSpec((bb, steps, nh, hd), lambda b: (b, 0, 0, 0))
    o, lse = pl.pallas_call(
        functools.partial(_attn_sample_kernel, n_sets=n_sets, steps=steps, q_rows=q_rows),
        grid=(batch // bb,),
        in_specs=[pl.BlockSpec((bb, steps, nh, hd), lambda b: (b, 0, group, 0)),
                  pl.BlockSpec((bb, new_rows, 2 * nh, hd), lambda b: (b, 0, 0, 0))]
                 + cache_specs
                 + [pl.BlockSpec(bias.shape, lambda b: (0, 0, 0))],
        out_specs=[out_spec, out_spec],
        out_shape=[jax.ShapeDtypeStruct((batch, steps, nh, hd), F32)] * 2,
        compiler_params=_params(1),
        name=f"attn_sample_g{group}",
    )(q4, kv_new_pad, *([cache_view] * n_sets), bias)
    return o.reshape(batch * steps, nh, hd), lse.reshape(batch * steps, nh, hd)


def _attn_combine_kernel(o0, o1, o2, l0, l1, l2, out_ref, mix_ref):
    ls = (l0[...], l1[...], l2[...])
    m = jnp.maximum(jnp.maximum(ls[0], ls[1]), ls[2])
    ws = [jnp.exp(l - m) for l in ls]
    num = ws[0] * o0[...] + ws[1] * o1[...] + ws[2] * o2[...]
    mix_ref[...] = num / (ws[0] + ws[1] + ws[2])
    for h in range(ATT_HEADS_PER_GROUP):
        out_ref[:, h * ATT_HEAD_DIM:(h + 1) * ATT_HEAD_DIM] = _load_head(mix_ref, h).astype(out_ref.dtype)


def attn_combine(outs, lses, tm=512):
    m, nh, hd = outs[0].shape
    tm = min(tm, m)
    spec = pl.BlockSpec((tm, nh, hd), lambda i: (i, 0, 0))
    return pl.pallas_call(
        _attn_combine_kernel,
        grid=(m // tm,),
        in_specs=[spec] * 6,
        out_specs=pl.BlockSpec((tm, nh * hd), lambda i: (i, 0)),
        out_shape=jax.ShapeDtypeStruct((m, nh * hd), BF16),
        scratch_shapes=[pltpu.VMEM((tm, nh, hd), F32)],
        compiler_params=_params(1),
        name="attn_combine",
    )(*outs, *lses)


def _t5_bucket(dist):
    max_exact = N_BUCKETS // 2
    d = np.maximum(dist, max_exact).astype(np.float32)
    large = max_exact + (np.log(d / max_exact) / np.log(MAX_DISTANCE / max_exact)
                         * (N_BUCKETS - max_exact)).astype(np.int32)
    large = np.minimum(large, N_BUCKETS - 1)
    return np.where(dist < max_exact, dist, large).astype(np.int32)


def _group_bias(rel_bias, g):
    win, dil = ATT_PATTERNS[g]
    buckets = _t5_bucket(np.arange(win // dil + 1) * dil)
    return rel_bias[buckets][:, g * ATT_HEADS_PER_GROUP:(g + 1) * ATT_HEADS_PER_GROUP]


def _prompt_mask_bias(bias_g, n_keys, kb):
    period = 3 * kb
    offs = np.arange(period)
    offs = np.where(offs >= 2 * kb, offs - period, offs)
    j_of = np.clip(kb - offs, 0, n_keys)
    a = jnp.transpose(bias_g[j_of]).astype(F32)
    nh = a.shape[0]
    tiled = jnp.tile(a, (1, kb))[:, :kb * (period - 1)].reshape(nh, kb, period - 1)
    rel = np.arange(kb)[:, None] + kb - np.arange(2 * kb)[None, :]
    valid = (rel >= 0) & (rel <= n_keys)
    return jnp.where(valid[None], tiled[:, :, :2 * kb], NEG_INF)


def _sample_mask_bias(bias_g, *, cache_len, dil, n_keys, steps, new_rows, shared_cache, q_rows):
    nh = ATT_HEADS_PER_GROUP
    n_cache = 1 if shared_cache else steps
    per_cache = steps // n_cache
    rows_per_cache = cache_len if shared_cache else cache_len // dil
    blocks = []
    for ci in range(n_cache):
        if shared_cache:
            cache_pos = np.arange(rows_per_cache)
        else:
            cache_pos = ci + dil * np.arange(rows_per_cache)
        key_pos = np.concatenate([cache_pos, cache_len + np.arange(new_rows)])
        key_live = np.concatenate([np.ones(rows_per_cache, bool), np.arange(new_rows) < steps])
        rows = []
        for tt in range(per_cache):
            t = ci * per_cache + tt
            diff = cache_len + t - key_pos
            valid = key_live & (diff >= 0) & (diff % dil == 0) & (diff // dil <= n_keys)
            j = np.clip(diff // dil, 0, n_keys)
            rows.append(jnp.where(valid[None, :], bias_g[j].T.astype(F32), NEG_INF))
        blk = jnp.concatenate(rows, axis=0)
        pad = q_rows - per_cache * nh
        if pad:
            blk = jnp.concatenate([blk, jnp.zeros((pad, blk.shape[1]), F32)], axis=0)
        blocks.append(blk)
    return jnp.stack(blocks)


SAMPLE_NEW_ROWS = 16
SAMPLE_Q_ROWS_MIN = 16
SSD_SAMPLE_Q = 16
SAMPLE_SHARED_BATCH_BLOCK = 4


IN_SIZES = (D_INNER, CONV_DIM, SSM_HEADS, ATT_WIDTH, ATT_WIDTH, ATT_WIDTH, D_MODEL, D_MODEL)
IN_OFFSETS = tuple(int(o) for o in np.concatenate([[0], np.cumsum(IN_SIZES)]))


IN_TILE = 1024
assert all(o % W_ROW_ALIGN == 0 for o in IN_OFFSETS)


def _in_rows(seg, stride=IN_TILE, first=0):
    return lambda j: IN_OFFSETS[seg] + first + j * stride


def _layer(x, w, *, batch, seq, is_prompt, caches, conv_state, ssm_state, rel_bias):
    m = batch * seq
    nh, hd = ATT_HEADS_PER_GROUP, ATT_HEAD_DIM
    tm = 1024 if m % 1024 == 0 else 512
    h = rmsnorm(x, w["norm_mix"], BF16)
    proj = lambda name, **kw: matmul(h, w["in_t"], tm=tm, tn=IN_TILE, name=name, **kw)
    z = proj("in_z", n=IN_SIZES[0], w_rows=_in_rows(0))
    dt_raw = matmul(h, w["in_t"], tm=tm, tn=LANES, n=LANES, w_rows=_in_rows(2), name="in_dt")
    q = proj("in_q", n=IN_SIZES[3], w_rows=_in_rows(3), epilogue="heads")
    kvs = [proj(f"in_kv{g}", n=2 * ATT_GROUP_WIDTH, epilogue="heads",
                w_rows=_in_rows(4, stride=IN_OFFSETS[5] - IN_OFFSETS[4], first=g * ATT_GROUP_WIDTH))
           for g in range(ATT_N_GROUPS)]
    gates = proj("in_gates", n=IN_SIZES[6] + IN_SIZES[7], w_rows=_in_rows(6))

    if is_prompt:
        u, xbc_tail = proj("in_xbc_conv", n=IN_SIZES[1], w_rows=_in_rows(1),
                           epilogue="conv_silu", conv=(w["conv_w"], w["conv_b"]), seq=seq)
        new_conv = xbc_tail[:, SUBLANES - (CONV_WIDTH - 1):]
        y_ssm, new_ssm = ssd(u, dt_raw, z, w["dt_bias"], w["a_log"], w["d_skip"], w["ssm_norm"], None,
                             batch=batch, seq=seq, q_len=SSD_CHUNK, valid_len=SSD_CHUNK, groups_per_step=2)
    else:
        xbc = proj("in_xbc", n=IN_SIZES[1], w_rows=_in_rows(1))
        xpad = jnp.concatenate([conv_state.reshape(batch, (CONV_WIDTH - 1) * CONV_DIM),
                                xbc.reshape(batch, seq * CONV_DIM)], axis=1)
        u = conv_sample(xpad, w["conv_w"], w["conv_b"], steps=seq)
        new_conv = xpad.reshape(batch, CONV_WIDTH - 1 + seq, CONV_DIM)[:, seq:]
        qp = SSD_SAMPLE_Q
        pad_rows = lambda a: jnp.pad(a.reshape(batch, seq, a.shape[-1]),
                                     ((0, 0), (0, qp - seq), (0, 0))).reshape(batch * qp, a.shape[-1])
        y_pad, new_ssm = ssd(pad_rows(u.reshape(m, CONV_DIM)), pad_rows(dt_raw), pad_rows(z),
                             w["dt_bias"], w["a_log"], w["d_skip"], w["ssm_norm"], ssm_state,
                             batch=batch, seq=qp, q_len=qp, valid_len=seq, groups_per_step=SSM_GROUPS)
        y_ssm = y_pad.reshape(batch, qp, D_INNER)[:, :seq].reshape(m, D_INNER)

    outs, lses, new_rows = [], [], []
    for g, (win, dil) in enumerate(ATT_PATTERNS):
        n_keys = win // dil
        bias_g = _group_bias(rel_bias, g)
        rows = kvs[g].reshape(batch, seq, 2, nh, hd)
        if is_prompt:
            mb = _prompt_mask_bias(bias_g, n_keys, n_keys)
            o, lse = attn_prompt(q, kvs[g], mb, group=g, batch=batch, seq=seq, dil=dil, kb=n_keys)
            keep = min(win, seq)
            rows = rows[:, seq - keep:]
        else:
            cache = caches[g]
            cache_len = cache.shape[1]
            shared = dil == 1
            q_rows = max(SAMPLE_Q_ROWS_MIN, (seq if shared else 1) * nh)
            sb = _sample_mask_bias(bias_g, cache_len=cache_len, dil=dil, n_keys=n_keys, steps=seq,
                                   new_rows=SAMPLE_NEW_ROWS, shared_cache=shared, q_rows=q_rows)
            kv_new = jnp.pad(kvs[g].reshape(batch, seq, 2 * nh, hd),
                             ((0, 0), (0, SAMPLE_NEW_ROWS - seq), (0, 0), (0, 0)))
            o, lse = attn_sample(q, kv_new, cache, sb, group=g, batch=batch, steps=seq, dil=dil,
                                 batch_block=SAMPLE_SHARED_BATCH_BLOCK if shared else 1)
        outs.append(o)
        lses.append(lse)
        new_rows.append(rows)
    o_att = attn_combine(outs, lses)

    merged = merge_proj(y_ssm, o_att, gates, w["ssm_proj"], w["att_proj"], tm=tm, tn=512)
    x1 = matmul(merged, w["out"], tm=tm, tn=1024, epilogue="residual", residual=x, name="out_proj")
    h2 = rmsnorm(x1, w["norm_mlp"], BF16)
    up = matmul(h2, w["up"], tm=tm, tn=1024, out_dtype=BF16, epilogue="relu2", name="mlp_up")
    x2 = matmul(up, w["down"], tm=512, tn=512, epilogue="residual", residual=x1, name="mlp_down")
    y = rmsnorm(x2, w["norm_final"], F32)
    return y, new_rows, new_conv, new_ssm


def kernel(x_prompt, x_sample, cache_win128, cache_win512, cache_win2048, state_conv, state_ssm, w_in, conv_w, conv_b, dt_bias, a_log, d_skip, ssm_norm, w_ssm_proj, w_att_proj, w_out, norm_mix, w_up, w_down, norm_mlp, rel_bias, norm_final):
    assert w_in.shape[0] == 1, "single-layer model"
    bp, sp, _ = x_prompt.shape
    bs, ss, _ = x_sample.shape
    w = dict(
        in_t=jnp.swapaxes(w_in[0], 0, 1),
        conv_w=conv_w[0], conv_b=conv_b[0], dt_bias=dt_bias[0], a_log=a_log[0], d_skip=d_skip[0],
        ssm_norm=ssm_norm[0], ssm_proj=w_ssm_proj[0].astype(BF16), att_proj=w_att_proj[0].astype(BF16),
        out=w_out[0], norm_mix=norm_mix[0], up=w_up[0],
        down=w_down[0].astype(BF16), norm_mlp=norm_mlp[0], norm_final=norm_final)

    yp, rows_p, conv_p, ssm_p = _layer(
        x_prompt.reshape(bp * sp, D_MODEL), w, batch=bp, seq=sp, is_prompt=True,
        caches=None, conv_state=None, ssm_state=None, rel_bias=rel_bias)
    ys, rows_s, conv_s, ssm_s = _layer(
        x_sample.reshape(bs * ss, D_MODEL), w, batch=bs, seq=ss, is_prompt=False,
        caches=(cache_win128[0], cache_win512[0], cache_win2048[0]),
        conv_state=state_conv[0], ssm_state=state_ssm[0], rel_bias=rel_bias)

    return (yp.reshape(bp, sp, D_MODEL), ys.reshape(bs, ss, D_MODEL),
            rows_p[0][None], rows_p[1][None], rows_p[2][None], conv_p[None], ssm_p[None],
            rows_s[0][None], rows_s[1][None], rows_s[2][None], conv_s[None], ssm_s[None])
```

```python
import functools
import math

import numpy as np
import jax
import jax.numpy as jnp
from jax import lax
from jax.experimental import pallas as pl
from jax.experimental.pallas import tpu as pltpu

D_MODEL = 2048
D_INNER = 2 * D_MODEL
SSM_HEAD_DIM = 64
SSM_HEADS = D_INNER // SSM_HEAD_DIM
SSM_GROUPS = 8
HEADS_PER_SSM_GROUP = SSM_HEADS // SSM_GROUPS
SSM_STATE = 128
SSM_GROUP_WIDTH = D_INNER // SSM_GROUPS
CONV_WIDTH = 4
CONV_DIM = D_INNER + 2 * SSM_GROUPS * SSM_STATE
SSD_CHUNK = 128
ATT_HEAD_DIM = 128
ATT_HEADS_PER_GROUP = 8
ATT_PATTERNS = ((128, 1), (512, 4), (2048, 16))
ATT_N_GROUPS = len(ATT_PATTERNS)
ATT_GROUP_WIDTH = ATT_HEADS_PER_GROUP * ATT_HEAD_DIM
ATT_WIDTH = ATT_N_GROUPS * ATT_GROUP_WIDTH
N_BUCKETS = 32
MAX_DISTANCE = max(w for w, _ in ATT_PATTERNS)
D_FF = 4 * D_MODEL
EPS = 1e-6

LANES = 128
SUBLANES = 8
VMEM_LIMIT_BYTES = 56 * 1024 * 1024

F32 = jnp.float32
BF16 = jnp.bfloat16
NEG_INF = float("-inf")


def _params(n_grid_dims):
    return pltpu.CompilerParams(
        dimension_semantics=("arbitrary",) * n_grid_dims,
        vmem_limit_bytes=VMEM_LIMIT_BYTES)


def _sigmoid(x):
    return 0.5 * jnp.tanh(0.5 * x) + 0.5


def _silu(x):
    return x * _sigmoid(x)


def _softplus(x):
    return jnp.maximum(x, 0.0) + jnp.log1p(jnp.exp(-jnp.abs(x)))


def _dot_nt(a, b):
    return lax.dot_general(a, b, (((1,), (1,)), ((), ())), preferred_element_type=F32)


def _dot_tn(a, b):
    return lax.dot_general(a, b, (((0,), (0,)), ((), ())), preferred_element_type=F32)


def _load_rows(ref, offset, rows, stride):
    flat = ref.reshape(math.prod(ref.shape[:-1]), ref.shape[-1])
    return flat[pl.ds(offset, rows, stride=stride), :]


def _store_rows(ref, offset, rows, stride, value):
    flat = ref.reshape(math.prod(ref.shape[:-1]), ref.shape[-1])
    flat[pl.ds(offset, rows, stride=stride), :] = value


def _load_head(ref, h):
    return _load_rows(ref, h, ref.shape[0], ref.shape[1])


def _store_head(ref, h, value):
    _store_rows(ref, h, ref.shape[0], ref.shape[1], value)


def _rmsnorm_kernel(x_ref, g_ref, o_ref):
    x = x_ref[...]
    ms = jnp.mean(x * x, axis=-1, keepdims=True)
    o_ref[...] = (x * lax.rsqrt(ms + EPS) * g_ref[...]).astype(o_ref.dtype)


def rmsnorm(x, g, out_dtype, tm=512):
    m, d = x.shape
    tm = min(tm, m)
    return pl.pallas_call(
        _rmsnorm_kernel,
        grid=(m // tm,),
        in_specs=[pl.BlockSpec((tm, d), lambda i: (i, 0)),
                  pl.BlockSpec((1, d), lambda i: (0, 0))],
        out_specs=pl.BlockSpec((tm, d), lambda i: (i, 0)),
        out_shape=jax.ShapeDtypeStruct((m, d), out_dtype),
        compiler_params=_params(1),
        name="rmsnorm",
    )(x, g.reshape(1, d))


def _bf16_weights(w_ref, wbf_ref):
    if wbf_ref is None:
        return w_ref

    @pl.when(pl.program_id(1) == 0)
    def _():
        wbf_ref[...] = w_ref[...].astype(BF16)

    return wbf_ref


CONV_COL_CHUNK = 256


def _tile_dot(a, w_ref, cols, w_transposed):
    if w_transposed:
        return _dot_nt(a, w_ref[cols, :])
    return jnp.dot(a, w_ref[:, cols], preferred_element_type=F32)


def _conv_silu_tiles(a_ref, w_ref, cw_ref, cb_ref, o_ref, tail_ref, pad_ref, *, seq_tiles, w_transposed):
    tm = a_ref.shape[0]
    first = SUBLANES - (CONV_WIDTH - 1)

    @pl.when(pl.program_id(1) % seq_tiles == 0)
    def _():
        pad_ref[0:SUBLANES, :] = jnp.zeros((SUBLANES, pad_ref.shape[1]), F32)

    a = a_ref[...]
    for c0 in range(0, o_ref.shape[1], CONV_COL_CHUNK):
        cols = slice(c0, c0 + CONV_COL_CHUNK)
        acc = _tile_dot(a, w_ref, cols, w_transposed)
        pad_ref[SUBLANES:SUBLANES + tm, cols] = acc
        conv = cb_ref[:, cols] + pad_ref[first:first + tm, cols] * cw_ref[0:1, cols]
        for i in range(1, CONV_WIDTH):
            conv = conv + pad_ref[first + i:first + i + tm, cols] * cw_ref[i:i + 1, cols]
        o_ref[:, cols] = _silu(conv)
        last_rows = acc[tm - SUBLANES:, :]
        pad_ref[0:SUBLANES, cols] = last_rows
        tail_ref[:, cols] = last_rows


def _write_epilogue(acc, epilogue, r_ref, o_ref):
    if epilogue == "residual":
        o_ref[...] = r_ref[...] + acc
    elif epilogue == "relu2":
        u = jnp.maximum(acc, 0.0)
        o_ref[...] = (u * u).astype(o_ref.dtype)
    elif epilogue == "heads":
        for j in range(o_ref.shape[1]):
            _store_head(o_ref, j, acc[:, j * LANES:(j + 1) * LANES])
    else:
        o_ref[...] = acc.astype(o_ref.dtype)


def _mm_kernel(*refs, epilogue, cast_w, seq_tiles, w_transposed, row_tiles, second):
    refs = list(refs)
    wbf_ref = pad_ref = None
    if epilogue == "conv_silu":
        pad_ref = refs.pop()
    if cast_w:
        wbf_ref = refs.pop()
    it = iter(refs)
    a_ref, w_ref = next(it), next(it)
    r_ref = next(it) if epilogue == "residual" else None
    cw_ref, cb_ref = (next(it), next(it)) if epilogue == "conv_silu" else (None, None)
    a2_ref = next(it) if second else None
    r2_ref = next(it) if second and epilogue == "residual" else None
    o_ref = next(it)
    tail_ref = next(it) if epilogue == "conv_silu" else None
    o2_ref = next(it) if second else None
    w_ref = _bf16_weights(w_ref, wbf_ref)

    def first_part():
        if epilogue == "conv_silu":
            _conv_silu_tiles(a_ref, w_ref, cw_ref, cb_ref, o_ref, tail_ref, pad_ref,
                             seq_tiles=seq_tiles, w_transposed=w_transposed)
        else:
            _write_epilogue(_tile_dot(a_ref[...], w_ref, slice(None), w_transposed), epilogue, r_ref, o_ref)

    def second_part():
        _write_epilogue(_tile_dot(a2_ref[...], w_ref, slice(None), w_transposed),
                        "none" if epilogue == "conv_silu" else epilogue, r2_ref, o2_ref)

    if second:
        pl.when(pl.program_id(1) < row_tiles)(first_part)
        pl.when(pl.program_id(1) == row_tiles)(second_part)
    else:
        first_part()


W_ROW_ALIGN = 64


def matmul(a, w, *, tm, tn, n=None, w_rows=None, out_dtype=F32, epilogue="none", residual=None,
           conv=None, seq=None, a2=None, residual2=None, name="matmul"):
    m, k = a.shape
    n = w.shape[1] if n is None else n
    tm = min(tm, m)
    tn = min(tn, n)
    assert m % tm == 0 and n % tn == 0
    row_tiles = m // tm
    second = a2 is not None
    cast_w = w.dtype != BF16
    row = (lambda i: jnp.minimum(i, row_tiles - 1)) if second else (lambda i: i)
    if w_rows is None:
        w_spec = pl.BlockSpec((k, tn), lambda j, i: (0, j))
        w_tile = (k, tn)
    else:
        w_spec = pl.BlockSpec((pl.Element(tn), pl.Element(k)),
                              lambda j, i: (pl.multiple_of(w_rows(j), W_ROW_ALIGN), 0))
        w_tile = (tn, k)
    in_specs = [pl.BlockSpec((tm, k), lambda j, i: (row(i), 0)), w_spec]
    args = [a, w]
    scratch = [pltpu.VMEM(w_tile, BF16)] if cast_w else []
    heads = epilogue == "heads"

    def out_for(rows, tile_rows, row_index):
        if heads:
            return (pl.BlockSpec((tile_rows, tn // LANES, LANES), lambda j, i: (row_index(i), j, 0)),
                    jax.ShapeDtypeStruct((rows, n // LANES, LANES), out_dtype))
        return (pl.BlockSpec((tile_rows, tn), lambda j, i: (row_index(i), j)),
                jax.ShapeDtypeStruct((rows, n), out_dtype))

    spec, shape = out_for(m, tm, row)
    out_specs, out_shapes = [spec], [shape]
    seq_tiles = None
    if epilogue == "residual":
        in_specs.append(pl.BlockSpec((tm, tn), lambda j, i: (row(i), j)))
        args.append(residual)
    elif epilogue == "conv_silu":
        assert seq % tm == 0
        seq_tiles = seq // tm
        conv_w, conv_b = conv
        in_specs += [pl.BlockSpec((CONV_WIDTH, tn), lambda j, i: (0, j)),
                     pl.BlockSpec((1, tn), lambda j, i: (0, j))]
        args += [conv_w, conv_b.reshape(1, n)]
        out_specs.append(pl.BlockSpec((None, SUBLANES, tn), lambda j, i: (row(i) // seq_tiles, 0, j)))
        out_shapes.append(jax.ShapeDtypeStruct((m // seq, SUBLANES, n), F32))
        scratch.append(pltpu.VMEM((tm + SUBLANES, tn), F32))
    if second:
        m2 = a2.shape[0]
        in_specs.append(pl.BlockSpec((m2, k), lambda j, i: (0, 0)))
        args.append(a2)
        if epilogue == "residual":
            in_specs.append(pl.BlockSpec((m2, tn), lambda j, i: (0, j)))
            args.append(residual2)
        spec, shape = out_for(m2, m2, lambda i: 0)
        out_specs.append(spec)
        out_shapes.append(shape)
    outs = pl.pallas_call(
        functools.partial(_mm_kernel, epilogue=epilogue, cast_w=cast_w, seq_tiles=seq_tiles,
                          w_transposed=w_rows is not None, row_tiles=row_tiles, second=second),
        grid=(n // tn, row_tiles + (1 if second else 0)),
        in_specs=in_specs,
        out_specs=out_specs,
        out_shape=out_shapes,
        scratch_shapes=scratch,
        compiler_params=_params(2),
        name=name,
    )(*args)
    return outs[0] if len(outs) == 1 else tuple(outs)


def _merge_kernel(*refs, cast_w, row_tiles, second):
    refs = list(refs)
    wsbf_ref = wabf_ref = None
    if cast_w:
        wabf_ref = refs.pop()
        wsbf_ref = refs.pop()
    it = iter(refs)
    first_in = [next(it) for _ in range(4)]
    ws_ref, wa_ref = next(it), next(it)
    second_in = [next(it) for _ in range(4)] if second else None
    out_ref = next(it)
    out2_ref = next(it) if second else None
    ws_ref = _bf16_weights(ws_ref, wsbf_ref)
    wa_ref = _bf16_weights(wa_ref, wabf_ref)

    def part(y_ref, o_ref, gs_ref, ga_ref, dst_ref):
        ps = jnp.dot(y_ref[...], ws_ref[...], preferred_element_type=F32)
        pa = jnp.dot(o_ref[...], wa_ref[...], preferred_element_type=F32)
        dst_ref[...] = (_sigmoid(gs_ref[...]) * ps + _sigmoid(ga_ref[...]) * pa).astype(dst_ref.dtype)

    if second:
        pl.when(pl.program_id(1) < row_tiles)(lambda: part(*first_in, out_ref))
        pl.when(pl.program_id(1) == row_tiles)(lambda: part(*second_in, out2_ref))
    else:
        part(*first_in, out_ref)


def merge_proj(y_ssm, o_att, gates, w_ssm, w_att, *, tm, tn, second=None):
    m = y_ssm.shape[0]
    n = w_ssm.shape[1]
    tm = min(tm, m)
    n_col = n // tn
    row_tiles = m // tm
    row = (lambda i: jnp.minimum(i, row_tiles - 1)) if second else (lambda i: i)
    cast_w = w_ssm.dtype != BF16
    scratch = []
    if cast_w:
        scratch = [pltpu.VMEM((w_ssm.shape[0], tn), BF16), pltpu.VMEM((w_att.shape[0], tn), BF16)]

    def operand_specs(tile_rows, row_index):
        return [pl.BlockSpec((tile_rows, y_ssm.shape[1]), lambda j, i: (row_index(i), 0)),
                pl.BlockSpec((tile_rows, o_att.shape[1]), lambda j, i: (row_index(i), 0)),
                pl.BlockSpec((tile_rows, tn), lambda j, i: (row_index(i), j)),
                pl.BlockSpec((tile_rows, tn), lambda j, i: (row_index(i), j + n_col))]

    in_specs = operand_specs(tm, row) + [pl.BlockSpec((w_ssm.shape[0], tn), lambda j, i: (0, j)),
                                        pl.BlockSpec((w_att.shape[0], tn), lambda j, i: (0, j))]
    args = [y_ssm, o_att, gates, gates, w_ssm, w_att]
    out_specs = [pl.BlockSpec((tm, tn), lambda j, i: (row(i), j))]
    out_shapes = [jax.ShapeDtypeStruct((m, n), BF16)]
    if second:
        y2, o2, g2 = second
        m2 = y2.shape[0]
        in_specs += operand_specs(m2, lambda i: 0)
        args += [y2, o2, g2, g2]
        out_specs.append(pl.BlockSpec((m2, tn), lambda j, i: (0, j)))
        out_shapes.append(jax.ShapeDtypeStruct((m2, n), BF16))
    outs = pl.pallas_call(
        functools.partial(_merge_kernel, cast_w=cast_w, row_tiles=row_tiles, second=bool(second)),
        grid=(n_col, row_tiles + (1 if second else 0)),
        in_specs=in_specs,
        out_specs=out_specs,
        out_shape=out_shapes,
        scratch_shapes=scratch,
        compiler_params=_params(2),
        name="merge_proj",
    )(*args)
    return outs[0] if len(outs) == 1 else tuple(outs)


def _conv_sample_kernel(x0_ref, x1_ref, x2_ref, x3_ref, w_ref, b_ref, o_ref):
    acc = b_ref[...] + x0_ref[...] * w_ref[0:1, :]
    for i, x_ref in enumerate((x1_ref, x2_ref, x3_ref), start=1):
        acc = acc + x_ref[...] * w_ref[i:i + 1, :]
    o_ref[...] = _silu(acc)


def conv_sample(xpad, conv_w, conv_b, *, steps, tc=512):
    bsz = xpad.shape[0]
    c = conv_w.shape[1]
    nct = c // tc
    x_specs = [pl.BlockSpec((bsz, tc), functools.partial(lambda t, j, i: (0, (t + i) * nct + j), i=i))
               for i in range(CONV_WIDTH)]
    return pl.pallas_call(
        _conv_sample_kernel,
        grid=(steps, nct),
        in_specs=x_specs + [pl.BlockSpec((CONV_WIDTH, tc), lambda t, j: (0, j)),
                            pl.BlockSpec((1, tc), lambda t, j: (0, j))],
        out_specs=pl.BlockSpec((bsz, tc), lambda t, j: (0, t * nct + j)),
        out_shape=jax.ShapeDtypeStruct((bsz, steps * c), F32),
        compiler_params=_params(2),
        name="conv_sample",
    )(xpad, xpad, xpad, xpad, conv_w, conv_b.reshape(1, c))


def _split3(a):
    hi = a.astype(BF16)
    rest = a - hi.astype(F32)
    mid = rest.astype(BF16)
    return hi, mid, (rest - mid.astype(F32)).astype(BF16)


def _ssd_group(gl, x_ref, b_ref, c_ref, dtc_ref, dtr_ref, z_ref, pc_ref, pr_ref, nrm_ref, y_ref, h_ref,
               ht_ref, *, q_len, valid_len, last_chunk):
    nh, hd = HEADS_PER_SSM_GROUP, SSM_HEAD_DIM
    pair_w = 2 * hd
    gw = SSM_GROUP_WIDTH
    x_cols = slice(gl * gw, (gl + 1) * gw)
    n_cols = slice(gl * SSM_STATE, (gl + 1) * SSM_STATE)

    pc = pc_ref[gl]
    pr = pr_ref[gl]
    dt_c = _softplus(dtc_ref[gl] + pc[0:1, :])
    dt_r = _softplus(dtr_ref[gl] + pr[:, 0:1])
    if valid_len < q_len:
        dt_c = jnp.where(lax.broadcasted_iota(jnp.int32, dt_c.shape, 0) < valid_len, dt_c, 0.0)
        dt_r = jnp.where(lax.broadcasted_iota(jnp.int32, dt_r.shape, 1) < valid_len, dt_r, 0.0)
    a_c = dt_c * (-jnp.exp(pc[1:2, :]))
    a_r = dt_r * (-jnp.exp(pr[:, 1:2]))
    ti = lax.broadcasted_iota(jnp.int32, (q_len, q_len), 0)
    tj = lax.broadcasted_iota(jnp.int32, (q_len, q_len), 1)
    causal = ti >= tj
    tril = causal.astype(BF16)
    triu = (ti <= tj).astype(BF16)
    acum_c = sum(jnp.dot(tril, part, preferred_element_type=F32) for part in _split3(a_c))
    acum_r = sum(jnp.dot(part, triu, preferred_element_type=F32) for part in _split3(a_r))
    end_c = acum_c[q_len - 1:q_len, :]
    dd_c = jnp.exp(end_c - acum_c) * dt_c
    src_r = acum_r - jnp.log(dt_r)

    x = x_ref[:, x_cols]
    cmat = c_ref[:, n_cols].astype(BF16)
    if ht_ref is None:
        bmat = b_ref[:, n_cols].astype(BF16)
        cb = _dot_nt(cmat, bmat)
        h_prev = h_ref[gl * nh:(gl + 1) * nh]
        y_off_all = _dot_nt(cmat, h_prev.reshape(nh * hd, SSM_STATE).astype(BF16))
    else:
        b_t = jnp.transpose(b_ref[:, n_cols]).astype(BF16)
        cb = jnp.dot(cmat, b_t, preferred_element_type=F32)
        ht_prev = ht_ref[gl]
        y_off_all = jnp.dot(cmat, ht_prev.astype(BF16), preferred_element_type=F32)

    lane = lax.broadcasted_iota(jnp.int32, (q_len, pair_w), 1)
    first_half = lane < hd

    def pair_bcast(cols, j0):
        return jnp.where(first_half,
                         jnp.broadcast_to(cols[:, j0:j0 + 1], (q_len, pair_w)),
                         jnp.broadcast_to(cols[:, j0 + 1:j0 + 2], (q_len, pair_w)))

    ys = []
    xws = []
    sumsq = jnp.zeros((q_len, 1), F32)
    for i in range(nh // 2):
        j0 = 2 * i
        x_pair = x[:, i * pair_w:(i + 1) * pair_w]
        y_diag = jnp.zeros((q_len, pair_w), F32)
        for j in (j0, j0 + 1):
            seg = jnp.broadcast_to(acum_c[:, j:j + 1], (q_len, q_len)) - src_r[j:j + 1, :]
            w = (cb * jnp.exp(jnp.where(causal, seg, NEG_INF))).astype(BF16)
            own = first_half if j == j0 else jnp.logical_not(first_half)
            y_diag = y_diag + jnp.dot(w, jnp.where(own, x_pair, 0.0).astype(BF16),
                                      preferred_element_type=F32)
        e_pair = jnp.exp(pair_bcast(acum_c, j0))
        d_pair = jnp.where(first_half[0:1, :], pc[2:3, j0:j0 + 1], pc[2:3, j0 + 1:j0 + 2])
        y = y_diag + y_off_all[:, i * pair_w:(i + 1) * pair_w] * e_pair + d_pair * x_pair
        y = y * _silu(z_ref[:, gl * gw + i * pair_w:gl * gw + (i + 1) * pair_w])
        sumsq = sumsq + jnp.sum(y * y, axis=-1, keepdims=True)
        ys.append(y)

        xw = (x_pair * pair_bcast(dd_c, j0)).astype(BF16)
        if ht_ref is None:
            s_pair = _dot_tn(xw, bmat)
            for jj, j in enumerate((j0, j0 + 1)):
                chunk_decay = jnp.exp(acum_r[j:j + 1, q_len - 1:q_len])
                h_ref[gl * nh + j] = h_prev[j] * chunk_decay + s_pair[jj * hd:(jj + 1) * hd, :]
        else:
            xws.append(xw)

    if ht_ref is not None:
        end_decay = jnp.exp(end_c)
        decay_row = jnp.concatenate(
            [jnp.where(first_half[0:1, :], end_decay[:, 2 * i:2 * i + 1], end_decay[:, 2 * i + 1:2 * i + 2])
             for i in range(nh // 2)], axis=1)
        ht_new = ht_prev * decay_row + jnp.dot(b_t, jnp.concatenate(xws, axis=1), preferred_element_type=F32)
        ht_ref[gl] = ht_new

        @pl.when(last_chunk)
        def _():
            for i in range(nh // 2):
                both = jnp.transpose(ht_new[:, i * pair_w:(i + 1) * pair_w])
                h_ref[gl * nh + 2 * i] = both[:hd, :]
                h_ref[gl * nh + 2 * i + 1] = both[hd:, :]

    inv = lax.rsqrt(sumsq * (1.0 / (nh * hd)) + EPS)
    for i in range(nh // 2):
        sl = slice(gl * gw + i * pair_w, gl * gw + (i + 1) * pair_w)
        y_ref[:, sl] = (ys[i] * inv * nrm_ref[:, sl]).astype(y_ref.dtype)


def _ssd_kernel(*refs, q_len, valid_len, has_h0, groups_per_step):
    if has_h0:
        h0_ref = refs[9]
        refs = refs[:9] + refs[10:] + (None,)
    h_ref, ht_ref = refs[-2:]
    chunk = pl.program_id(2)

    @pl.when(chunk == 0)
    def _():
        if has_h0:
            h_ref[...] = h0_ref[...]
        else:
            ht_ref[...] = jnp.zeros(ht_ref.shape, F32)

    for gl in range(groups_per_step):
        _ssd_group(gl, *refs, q_len=q_len, valid_len=valid_len,
                   last_chunk=chunk == pl.num_programs(2) - 1)


def ssd(u, dt_raw, z, dt_bias, a_log, d_skip, ssm_norm, h0, *, batch, seq, q_len, valid_len, groups_per_step):
    nh, g, gps = HEADS_PER_SSM_GROUP, SSM_GROUPS, groups_per_step
    nc = seq // q_len
    m = batch * seq
    dt4 = dt_raw[:, :SSM_HEADS].reshape(batch * nc, q_len, g, nh)
    dtc = jnp.transpose(dt4, (0, 2, 1, 3))
    dtr = jnp.transpose(dt4, (0, 2, 3, 1))
    pcol = jnp.stack([dt_bias, a_log, d_skip]).reshape(3, g, nh).transpose(1, 0, 2)
    prow = jnp.transpose(pcol, (0, 2, 1))
    b_off = D_INNER // (gps * SSM_STATE)
    c_off = b_off + g // gps
    row = lambda b, gi, c: b * nc + c
    in_specs = [
        pl.BlockSpec((q_len, gps * SSM_GROUP_WIDTH), lambda b, gi, c: (row(b, gi, c), gi)),
        pl.BlockSpec((q_len, gps * SSM_STATE), lambda b, gi, c: (row(b, gi, c), b_off + gi)),
        pl.BlockSpec((q_len, gps * SSM_STATE), lambda b, gi, c: (row(b, gi, c), c_off + gi)),
        pl.BlockSpec((None, gps, q_len, nh), lambda b, gi, c: (row(b, gi, c), gi, 0, 0)),
        pl.BlockSpec((None, gps, nh, q_len), lambda b, gi, c: (row(b, gi, c), gi, 0, 0)),
        pl.BlockSpec((q_len, gps * SSM_GROUP_WIDTH), lambda b, gi, c: (row(b, gi, c), gi)),
        pl.BlockSpec((gps, 3, nh), lambda b, gi, c: (gi, 0, 0)),
        pl.BlockSpec((gps, nh, 3), lambda b, gi, c: (gi, 0, 0)),
        pl.BlockSpec((1, gps * SSM_GROUP_WIDTH), lambda b, gi, c: (0, gi)),
    ]
    args = [u, u, u, dtc, dtr, z, pcol, prow, ssm_norm.reshape(1, D_INNER)]
    h_spec = pl.BlockSpec((None, gps * nh, SSM_HEAD_DIM, SSM_STATE), lambda b, gi, c: (b, gi, 0, 0))
    scratch = []
    if h0 is not None:
        in_specs.append(h_spec)
        args.append(h0)
    else:
        scratch.append(pltpu.VMEM((gps, SSM_STATE, SSM_GROUP_WIDTH), F32))
    return pl.pallas_call(
        functools.partial(_ssd_kernel, q_len=q_len, valid_len=valid_len, has_h0=h0 is not None,
                          groups_per_step=gps),
        grid=(batch, g // gps, nc),
        in_specs=in_specs,
        out_specs=[pl.BlockSpec((q_len, gps * SSM_GROUP_WIDTH), lambda b, gi, c: (row(b, gi, c), gi)), h_spec],
        out_shape=[jax.ShapeDtypeStruct((m, D_INNER), BF16),
                   jax.ShapeDtypeStruct((batch, SSM_HEADS, SSM_HEAD_DIM, SSM_STATE), F32)],
        scratch_shapes=scratch,
        compiler_params=_params(3),
        name="ssd",
    )(*args)


def _attn_prompt_kernel(*refs, kb, has_prev):
    if has_prev:
        q_ref, kp_ref, kc_ref, vp_ref, vc_ref, mb_ref, o_ref, l_ref = refs
    else:
        q_ref, kc_ref, vc_ref, mb_ref, o_ref, l_ref = refs
    not_first = pl.program_id(2) > 0
    scale = ATT_HEAD_DIM ** -0.5
    for h in range(ATT_HEADS_PER_GROUP):
        q = (_load_head(q_ref, h) * scale).astype(BF16)
        s_c = _dot_nt(q, _load_head(kc_ref, h).astype(BF16)) + mb_ref[h, :, kb:]
        m = jnp.max(s_c, axis=-1, keepdims=True)
        if has_prev:
            s_p = _dot_nt(q, _load_head(kp_ref, h).astype(BF16)) + mb_ref[h, :, :kb]
            s_p = jnp.where(not_first, s_p, NEG_INF)
            m = jnp.maximum(m, jnp.max(s_p, axis=-1, keepdims=True))
        p_c = jnp.exp(s_c - m)
        den = jnp.sum(p_c, axis=-1, keepdims=True)
        o = jnp.dot(p_c.astype(BF16), _load_head(vc_ref, h).astype(BF16), preferred_element_type=F32)
        if has_prev:
            p_p = jnp.exp(s_p - m)
            den = den + jnp.sum(p_p, axis=-1, keepdims=True)
            o = o + jnp.dot(p_p.astype(BF16), _load_head(vp_ref, h).astype(BF16), preferred_element_type=F32)
        _store_head(o_ref, h, o / den)
        _store_head(l_ref, h, jnp.broadcast_to(m + jnp.log(den), (kb, ATT_HEAD_DIM)))


def attn_prompt(q, kv, mbias, *, group, batch, seq, dil, kb):
    m_res = seq // dil
    nb = m_res // kb
    has_prev = nb > 1
    nh = ATT_HEADS_PER_GROUP
    q4 = q.reshape(batch * m_res, dil, q.shape[1], ATT_HEAD_DIM)
    kv4 = kv.reshape(batch * m_res, dil, 2 * nh, ATT_HEAD_DIM)

    def cur(head_block):
        return pl.BlockSpec((kb, None, nh, ATT_HEAD_DIM), lambda b, r, n: (b * nb + n, r, head_block, 0))

    def prev(head_block):
        return pl.BlockSpec((kb, None, nh, ATT_HEAD_DIM),
                            lambda b, r, n: (b * nb + jnp.maximum(n - 1, 0), r, head_block, 0))

    if has_prev:
        in_specs = [cur(group), prev(0), cur(0), prev(1), cur(1)]
        args = [q4, kv4, kv4, kv4, kv4]
    else:
        in_specs = [cur(group), cur(0), cur(1)]
        args = [q4, kv4, kv4]
    in_specs.append(pl.BlockSpec(mbias.shape, lambda b, r, n: (0, 0, 0)))
    args.append(mbias)
    o, lse = pl.pallas_call(
        functools.partial(_attn_prompt_kernel, kb=kb, has_prev=has_prev),
        grid=(batch, dil, nb),
        in_specs=in_specs,
        out_specs=[cur(0), cur(0)],
        out_shape=[jax.ShapeDtypeStruct((batch * m_res, dil, nh, ATT_HEAD_DIM), F32)] * 2,
        compiler_params=_params(3),
        name=f"attn_prompt_g{group}",
    )(*args)
    return o.reshape(batch * seq, nh, ATT_HEAD_DIM), lse.reshape(batch * seq, nh, ATT_HEAD_DIM)


def _attn_sample_kernel(*refs, n_sets, steps, q_rows):
    q_ref, new_ref = refs[:2]
    cache_refs = refs[2:2 + n_sets]
    bias_ref, o_ref, l_ref = refs[2 + n_sets:]
    gw = ATT_GROUP_WIDTH
    nh = ATT_HEADS_PER_GROUP
    hd = ATT_HEAD_DIM
    scale = ATT_HEAD_DIM ** -0.5
    bb = q_ref.shape[0]
    cache_rows = cache_refs[0].shape[1]
    new_rows = new_ref.shape[1]
    row_stride = 2 * nh
    per_set = steps // n_sets
    sub = lax.broadcasted_iota(jnp.int32, (nh, gw), 0)
    own = sub == lax.broadcasted_iota(jnp.int32, (nh, gw), 1) // hd
    for bi in range(bb):
        new_heads = [_load_rows(new_ref, bi * new_rows * 2 * nh + h, new_rows, 2 * nh) for h in range(2 * nh)]
        for ci in range(n_sets):
            q_blocks = []
            for tt in range(per_set):
                q_t = q_ref[bi, ci * per_set + tt] * scale
                q_blocks.append(jnp.where(own, jnp.concatenate([q_t] * nh, axis=1), 0.0))
            if q_rows > per_set * nh:
                q_blocks.append(jnp.zeros((q_rows - per_set * nh, gw), F32))
            qbd = jnp.concatenate(q_blocks, axis=0).astype(BF16)
            base = bi * cache_rows * row_stride
            cache_ref = cache_refs[ci]
            k_all = jnp.concatenate(
                [jnp.concatenate([_load_rows(cache_ref, base + h, cache_rows, row_stride), new_heads[h]], axis=0)
                 for h in range(nh)], axis=1).astype(BF16)
            v_all = jnp.concatenate(
                [jnp.concatenate([_load_rows(cache_ref, base + nh + h, cache_rows, row_stride), new_heads[nh + h]],
                                 axis=0) for h in range(nh)], axis=1).astype(BF16)
            s = _dot_nt(qbd, k_all) + bias_ref[ci]
            m = jnp.max(s, axis=-1, keepdims=True)
            p = jnp.exp(s - m)
            den = jnp.sum(p, axis=-1, keepdims=True)
            o_all = jnp.dot(p.astype(BF16), v_all, preferred_element_type=F32) / den
            lse = m + jnp.log(den)
            for tt in range(per_set):
                t = ci * per_set + tt
                rows = slice(tt * nh, (tt + 1) * nh)
                o_t = jnp.zeros((nh, hd), F32)
                for h in range(nh):
                    o_t = o_t + jnp.where(sub[:, :hd] == h, o_all[rows, h * hd:(h + 1) * hd], 0.0)
                o_ref[bi, t] = o_t
                l_ref[bi, t] = jnp.broadcast_to(lse[rows, :], (nh, hd))


def attn_sample(q, kv_new_pad, cache, bias, *, group, batch, steps, dil, batch_block):
    nh, hd = ATT_HEADS_PER_GROUP, ATT_HEAD_DIM
    bb = batch_block
    n_sets, q_rows, n_keys_tot = bias.shape
    new_rows = kv_new_pad.shape[1]
    cache_rows = n_keys_tot - new_rows
    q4 = q.reshape(batch, steps, q.shape[1], hd)
    if n_sets == 1:
        cache_view = cache.reshape(batch, cache_rows, 2 * nh, hd)
        cache_specs = [pl.BlockSpec((bb, cache_rows, 2 * nh, hd), lambda b: (b, 0, 0, 0))]
    else:
        cache_view = cache.reshape(batch, cache_rows, dil, 2 * nh, hd)
        cache_specs = [pl.BlockSpec((bb, cache_rows, None, 2 * nh, hd),
                                    functools.partial(lambda b, t: (b, 0, t, 0, 0), t=t))
                       for t in range(n_sets)]
    out_spec = pl.BlockSpec((bb, steps, nh, hd), lambda b: (b, 0, 0, 0))
    o, lse = pl.pallas_call(
        functools.partial(_attn_sample_kernel, n_sets=n_sets, steps=steps, q_rows=q_rows),
        grid=(batch // bb,),
        in_specs=[pl.BlockSpec((bb, steps, nh, hd), lambda b: (b, 0, group, 0)),
                  pl.BlockSpec((bb, new_rows, 2 * nh, hd), lambda b: (b, 0, 0, 0))]
                 + cache_specs
                 + [pl.BlockSpec(bias.shape, lambda b: (0, 0, 0))],
        out_specs=[out_spec, out_spec],
        out_shape=[jax.ShapeDtypeStruct((batch, steps, nh, hd), F32)] * 2,
        compiler_params=_params(1),
        name=f"attn_sample_g{group}",
    )(q4, kv_new_pad, *([cache_view] * n_sets), bias)
    return o.reshape(batch * steps, nh, hd), lse.reshape(batch * steps, nh, hd)


def _attn_combine_kernel(o0, o1, o2, l0, l1, l2, out_ref, mix_ref):
    ls = (l0[...], l1[...], l2[...])
    m = jnp.maximum(jnp.maximum(ls[0], ls[1]), ls[2])
    ws = [jnp.exp(l - m) for l in ls]
    num = ws[0] * o0[...] + ws[1] * o1[...] + ws[2] * o2[...]
    mix_ref[...] = num / (ws[0] + ws[1] + ws[2])
    for h in range(ATT_HEADS_PER_GROUP):
        out_ref[:, h * ATT_HEAD_DIM:(h + 1) * ATT_HEAD_DIM] = _load_head(mix_ref, h).astype(out_ref.dtype)


def attn_combine(outs, lses, tm=512):
    m, nh, hd = outs[0].shape
    tm = min(tm, m)
    spec = pl.BlockSpec((tm, nh, hd), lambda i: (i, 0, 0))
    return pl.pallas_call(
        _attn_combine_kernel,
        grid=(m // tm,),
        in_specs=[spec] * 6,
        out_specs=pl.BlockSpec((tm, nh * hd), lambda i: (i, 0)),
        out_shape=jax.ShapeDtypeStruct((m, nh * hd), BF16),
        scratch_shapes=[pltpu.VMEM((tm, nh, hd), F32)],
        compiler_params=_params(1),
        name="attn_combine",
    )(*outs, *lses)


def _t5_bucket(dist):
    max_exact = N_BUCKETS // 2
    d = np.maximum(dist, max_exact).astype(np.float32)
    large = max_exact + (np.log(d / max_exact) / np.log(MAX_DISTANCE / max_exact)
                         * (N_BUCKETS - max_exact)).astype(np.int32)
    large = np.minimum(large, N_BUCKETS - 1)
    return np.where(dist < max_exact, dist, large).astype(np.int32)


def _group_bias(rel_bias, g):
    win, dil = ATT_PATTERNS[g]
    buckets = _t5_bucket(np.arange(win // dil + 1) * dil)
    return rel_bias[buckets][:, g * ATT_HEADS_PER_GROUP:(g + 1) * ATT_HEADS_PER_GROUP]


def _prompt_mask_bias(bias_g, n_keys, kb):
    period = 3 * kb
    offs = np.arange(period)
    offs = np.where(offs >= 2 * kb, offs - period, offs)
    j_of = np.clip(kb - offs, 0, n_keys)
    a = jnp.transpose(bias_g[j_of]).astype(F32)
    nh = a.shape[0]
    tiled = jnp.tile(a, (1, kb))[:, :kb * (period - 1)].reshape(nh, kb, period - 1)
    rel = np.arange(kb)[:, None] + kb - np.arange(2 * kb)[None, :]
    valid = (rel >= 0) & (rel <= n_keys)
    return jnp.where(valid[None], tiled[:, :, :2 * kb], NEG_INF)


def _sample_mask_bias(bias_g, *, cache_len, dil, n_keys, steps, new_rows, shared_cache, q_rows):
    nh = ATT_HEADS_PER_GROUP
    n_cache = 1 if shared_cache else steps
    per_cache = steps // n_cache
    rows_per_cache = cache_len if shared_cache else cache_len // dil
    blocks = []
    for ci in range(n_cache):
        if shared_cache:
            cache_pos = np.arange(rows_per_cache)
        else:
            cache_pos = ci + dil * np.arange(rows_per_cache)
        key_pos = np.concatenate([cache_pos, cache_len + np.arange(new_rows)])
        key_live = np.concatenate([np.ones(rows_per_cache, bool), np.arange(new_rows) < steps])
        rows = []
        for tt in range(per_cache):
            t = ci * per_cache + tt
            diff = cache_len + t - key_pos
            valid = key_live & (diff >= 0) & (diff % dil == 0) & (diff // dil <= n_keys)
            j = np.clip(diff // dil, 0, n_keys)
            rows.append(jnp.where(valid[None, :], bias_g[j].T.astype(F32), NEG_INF))
        blk = jnp.concatenate(rows, axis=0)
        pad = q_rows - per_cache * nh
        if pad:
            blk = jnp.concatenate([blk, jnp.zeros((pad, blk.shape[1]), F32)], axis=0)
        blocks.append(blk)
    return jnp.stack(blocks)


SAMPLE_NEW_ROWS = 16
SAMPLE_Q_ROWS_MIN = 16
SSD_SAMPLE_Q = 16
SAMPLE_SHARED_BATCH_BLOCK = 4


IN_SIZES = (D_INNER, CONV_DIM, SSM_HEADS, ATT_WIDTH, ATT_WIDTH, ATT_WIDTH, D_MODEL, D_MODEL)
IN_OFFSETS = tuple(int(o) for o in np.concatenate([[0], np.cumsum(IN_SIZES)]))


IN_TILE = 1024
assert all(o % W_ROW_ALIGN == 0 for o in IN_OFFSETS)


def _in_rows(seg, stride=IN_TILE, first=0):
    return lambda j: IN_OFFSETS[seg] + first + j * stride


PROMPT_ROW_TILE = 1024


def _ssd_branches(u_p, u_s, dt_p, dt_s, z_p, z_s, w, conv_state, ssm_state, *, bp, sp, bs, ss):
    params = (w["dt_bias"], w["a_log"], w["d_skip"], w["ssm_norm"])
    y_p, ssm_p = ssd(u_p, dt_p, z_p, *params, None, batch=bp, seq=sp, q_len=SSD_CHUNK,
                     valid_len=SSD_CHUNK, groups_per_step=2)
    qp = SSD_SAMPLE_Q
    pad_rows = lambda a: jnp.pad(a.reshape(bs, ss, a.shape[-1]),
                                 ((0, 0), (0, qp - ss), (0, 0))).reshape(bs * qp, a.shape[-1])
    y_pad, ssm_s = ssd(pad_rows(u_s), pad_rows(dt_s), pad_rows(z_s), *params, ssm_state,
                       batch=bs, seq=qp, q_len=qp, valid_len=ss, groups_per_step=SSM_GROUPS)
    y_s = y_pad.reshape(bs, qp, D_INNER)[:, :ss].reshape(bs * ss, D_INNER)
    return y_p, y_s, ssm_p, ssm_s


def _attention_branches(q_p, q_s, kvs_p, kvs_s, caches, rel_bias, *, bp, sp, bs, ss):
    nh, hd = ATT_HEADS_PER_GROUP, ATT_HEAD_DIM
    outs_p, lses_p, outs_s, lses_s = [], [], [], []
    for g, (win, dil) in enumerate(ATT_PATTERNS):
        n_keys = win // dil
        bias_g = _group_bias(rel_bias, g)
        mb = _prompt_mask_bias(bias_g, n_keys, n_keys)
        o, lse = attn_prompt(q_p, kvs_p[g], mb, group=g, batch=bp, seq=sp, dil=dil, kb=n_keys)
        outs_p.append(o)
        lses_p.append(lse)

        cache = caches[g]
        cache_len = cache.shape[1]
        shared = dil == 1
        q_rows = max(SAMPLE_Q_ROWS_MIN, (ss if shared else 1) * nh)
        sb = _sample_mask_bias(bias_g, cache_len=cache_len, dil=dil, n_keys=n_keys, steps=ss,
                               new_rows=SAMPLE_NEW_ROWS, shared_cache=shared, q_rows=q_rows)
        kv_new = jnp.pad(kvs_s[g].reshape(bs, ss, 2 * nh, hd),
                         ((0, 0), (0, SAMPLE_NEW_ROWS - ss), (0, 0), (0, 0)))
        o, lse = attn_sample(q_s, kv_new, cache, sb, group=g, batch=bs, steps=ss, dil=dil,
                             batch_block=SAMPLE_SHARED_BATCH_BLOCK if shared else 1)
        outs_s.append(o)
        lses_s.append(lse)
    return attn_combine(outs_p, lses_p), attn_combine(outs_s, lses_s)


def _forward(x_p, x_s, w, caches, conv_state, ssm_state, rel_bias, *, bp, sp, bs, ss):
    nh, hd = ATT_HEADS_PER_GROUP, ATT_HEAD_DIM
    tm = PROMPT_ROW_TILE
    h_p = rmsnorm(x_p, w["norm_mix"], BF16)
    h_s = rmsnorm(x_s, w["norm_mix"], BF16)
    proj = lambda name, **kw: matmul(h_p, w["in_t"], a2=h_s, tm=tm, tn=IN_TILE, name=name, **kw)
    z_p, z_s = proj("in_z", n=IN_SIZES[0], w_rows=_in_rows(0))
    dt_p, dt_s = matmul(h_p, w["in_t"], a2=h_s, tm=tm, tn=LANES, n=LANES, w_rows=_in_rows(2), name="in_dt")
    q_p, q_s = proj("in_q", n=IN_SIZES[3], w_rows=_in_rows(3), epilogue="heads")
    kvs = [proj(f"in_kv{g}", n=2 * ATT_GROUP_WIDTH, epilogue="heads",
                w_rows=_in_rows(4, stride=IN_OFFSETS[5] - IN_OFFSETS[4], first=g * ATT_GROUP_WIDTH))
           for g in range(ATT_N_GROUPS)]
    kvs_p, kvs_s = [kv[0] for kv in kvs], [kv[1] for kv in kvs]
    gates_p, gates_s = proj("in_gates", n=IN_SIZES[6] + IN_SIZES[7], w_rows=_in_rows(6))
    u_p, xbc_tail, xbc_s = proj("in_xbc_conv", n=IN_SIZES[1], w_rows=_in_rows(1),
                                epilogue="conv_silu", conv=(w["conv_w"], w["conv_b"]), seq=sp)
    conv_p = xbc_tail[:, SUBLANES - (CONV_WIDTH - 1):]
    xpad = jnp.concatenate([conv_state.reshape(bs, (CONV_WIDTH - 1) * CONV_DIM),
                            xbc_s.reshape(bs, ss * CONV_DIM)], axis=1)
    u_s = conv_sample(xpad, w["conv_w"], w["conv_b"], steps=ss).reshape(bs * ss, CONV_DIM)
    conv_s = xpad.reshape(bs, CONV_WIDTH - 1 + ss, CONV_DIM)[:, ss:]

    y_p, y_s, ssm_p, ssm_s = _ssd_branches(u_p, u_s, dt_p, dt_s, z_p, z_s, w, conv_state, ssm_state,
                                           bp=bp, sp=sp, bs=bs, ss=ss)
    o_p, o_s = _attention_branches(q_p, q_s, kvs_p, kvs_s, caches, rel_bias, bp=bp, sp=sp, bs=bs, ss=ss)

    merged_p = merge_proj(y_p, o_p, gates_p, w["ssm_proj"], w["att_proj"], tm=tm, tn=512)
    merged_s = merge_proj(y_s, o_s, gates_s, w["ssm_proj"], w["att_proj"], tm=tm, tn=512)
    x1_p, x1_s = matmul(merged_p, w["out"], a2=merged_s, tm=tm, tn=512, epilogue="residual",
                        residual=x_p, residual2=x_s, name="out_proj")
    h2_p = rmsnorm(x1_p, w["norm_mlp"], BF16)
    h2_s = rmsnorm(x1_s, w["norm_mlp"], BF16)
    up_p, up_s = matmul(h2_p, w["up"], a2=h2_s, tm=tm, tn=1024, out_dtype=BF16, epilogue="relu2", name="mlp_up")
    x2_p = matmul(up_p, w["down"], tm=512, tn=512, epilogue="residual", residual=x1_p, name="mlp_down")
    x2_s = matmul(up_s, w["down"], tm=512, tn=512, epilogue="residual", residual=x1_s, name="mlp_down")
    y_out_p = rmsnorm(x2_p, w["norm_final"], F32)
    y_out_s = rmsnorm(x2_s, w["norm_final"], F32)

    rows_p = [kv.reshape(bp, sp, 2, nh, hd)[:, sp - min(win, sp):] for kv, (win, _) in zip(kvs_p, ATT_PATTERNS)]
    rows_s = [kv.reshape(bs, ss, 2, nh, hd) for kv in kvs_s]
    return (y_out_p, rows_p, conv_p, ssm_p), (y_out_s, rows_s, conv_s, ssm_s)


def kernel(x_prompt, x_sample, cache_win128, cache_win512, cache_win2048, state_conv, state_ssm, w_in, conv_w, conv_b, dt_bias, a_log, d_skip, ssm_norm, w_ssm_proj, w_att_proj, w_out, norm_mix, w_up, w_down, norm_mlp, rel_bias, norm_final):
    assert w_in.shape[0] == 1, "single-layer model"
    bp, sp, _ = x_prompt.shape
    bs, ss, _ = x_sample.shape
    w = dict(
        in_t=jnp.swapaxes(w_in[0], 0, 1),
        conv_w=conv_w[0], conv_b=conv_b[0], dt_bias=dt_bias[0], a_log=a_log[0], d_skip=d_skip[0],
        ssm_norm=ssm_norm[0], ssm_proj=w_ssm_proj[0].astype(BF16), att_proj=w_att_proj[0].astype(BF16),
        out=w_out[0], norm_mix=norm_mix[0], up=w_up[0],
        down=w_down[0].astype(BF16), norm_mlp=norm_mlp[0], norm_final=norm_final)

    (yp, rows_p, conv_p, ssm_p), (ys, rows_s, conv_s, ssm_s) = _forward(
        x_prompt.reshape(bp * sp, D_MODEL), x_sample.reshape(bs * ss, D_MODEL), w,
        (cache_win128[0], cache_win512[0], cache_win2048[0]), state_conv[0], state_ssm[0], rel_bias,
        bp=bp, sp=sp, bs=bs, ss=ss)

    return (yp.reshape(bp, sp, D_MODEL), ys.reshape(bs, ss, D_MODEL),
            rows_p[0][None], rows_p[1][None], rows_p[2][None], conv_p[None], ssm_p[None],
            rows_s[0][None], rows_s[1][None], rows_s[2][None], conv_s[None], ssm_s[None])
```

```python
import functools
import math

import numpy as np
import jax
import jax.numpy as jnp
from jax import lax
from jax.experimental import pallas as pl
from jax.experimental.pallas import tpu as pltpu

D_MODEL = 2048
D_INNER = 2 * D_MODEL
SSM_HEAD_DIM = 64
SSM_HEADS = D_INNER // SSM_HEAD_DIM
SSM_GROUPS = 8
HEADS_PER_SSM_GROUP = SSM_HEADS // SSM_GROUPS
SSM_STATE = 128
SSM_GROUP_WIDTH = D_INNER // SSM_GROUPS
CONV_WIDTH = 4
CONV_DIM = D_INNER + 2 * SSM_GROUPS * SSM_STATE
SSD_CHUNK = 128
ATT_HEAD_DIM = 128
ATT_HEADS_PER_GROUP = 8
ATT_PATTERNS = ((128, 1), (512, 4), (2048, 16))
ATT_N_GROUPS = len(ATT_PATTERNS)
ATT_GROUP_WIDTH = ATT_HEADS_PER_GROUP * ATT_HEAD_DIM
ATT_WIDTH = ATT_N_GROUPS * ATT_GROUP_WIDTH
N_BUCKETS = 32
MAX_DISTANCE = max(w for w, _ in ATT_PATTERNS)
D_FF = 4 * D_MODEL
EPS = 1e-6

LANES = 128
SUBLANES = 8
VMEM_LIMIT_BYTES = 56 * 1024 * 1024

F32 = jnp.float32
BF16 = jnp.bfloat16
NEG_INF = float("-inf")


def _params(n_grid_dims):
    return pltpu.CompilerParams(
        dimension_semantics=("arbitrary",) * n_grid_dims,
        vmem_limit_bytes=VMEM_LIMIT_BYTES)


def _sigmoid(x):
    return 0.5 * jnp.tanh(0.5 * x) + 0.5


def _silu(x):
    return x * _sigmoid(x)


def _softplus(x):
    return jnp.maximum(x, 0.0) + jnp.log1p(jnp.exp(-jnp.abs(x)))


def _dot_nt(a, b):
    return lax.dot_general(a, b, (((1,), (1,)), ((), ())), preferred_element_type=F32)


def _dot_tn(a, b):
    return lax.dot_general(a, b, (((0,), (0,)), ((), ())), preferred_element_type=F32)


def _load_rows(ref, offset, rows, stride):
    flat = ref.reshape(math.prod(ref.shape[:-1]), ref.shape[-1])
    return flat[pl.ds(offset, rows, stride=stride), :]


def _store_rows(ref, offset, rows, stride, value):
    flat = ref.reshape(math.prod(ref.shape[:-1]), ref.shape[-1])
    flat[pl.ds(offset, rows, stride=stride), :] = value


def _load_head(ref, h):
    return _load_rows(ref, h, ref.shape[0], ref.shape[1])


def _store_head(ref, h, value):
    _store_rows(ref, h, ref.shape[0], ref.shape[1], value)


def _rmsnorm_kernel(x_ref, g_ref, o_ref):
    x = x_ref[...]
    ms = jnp.mean(x * x, axis=-1, keepdims=True)
    o_ref[...] = (x * lax.rsqrt(ms + EPS) * g_ref[...]).astype(o_ref.dtype)


def rmsnorm(x, g, out_dtype, tm=512):
    m, d = x.shape
    tm = min(tm, m)
    return pl.pallas_call(
        _rmsnorm_kernel,
        grid=(m // tm,),
        in_specs=[pl.BlockSpec((tm, d), lambda i: (i, 0)),
                  pl.BlockSpec((1, d), lambda i: (0, 0))],
        out_specs=pl.BlockSpec((tm, d), lambda i: (i, 0)),
        out_shape=jax.ShapeDtypeStruct((m, d), out_dtype),
        compiler_params=_params(1),
        name="rmsnorm",
    )(x, g.reshape(1, d))


def _bf16_weights(w_ref, wbf_ref):
    if wbf_ref is None:
        return w_ref

    @pl.when(pl.program_id(1) == 0)
    def _():
        wbf_ref[...] = w_ref[...].astype(BF16)

    return wbf_ref


CONV_COL_CHUNK = 256


def _tile_dot(a, w_ref, cols, w_transposed):
    if w_transposed:
        return _dot_nt(a, w_ref[cols, :])
    return jnp.dot(a, w_ref[:, cols], preferred_element_type=F32)


def _conv_silu_tiles(a_ref, w_ref, cw_ref, cb_ref, o_ref, tail_ref, pad_ref, *, seq_tiles, w_transposed):
    tm = a_ref.shape[0]
    first = SUBLANES - (CONV_WIDTH - 1)

    @pl.when(pl.program_id(1) % seq_tiles == 0)
    def _():
        pad_ref[0:SUBLANES, :] = jnp.zeros((SUBLANES, pad_ref.shape[1]), F32)

    a = a_ref[...]
    for c0 in range(0, o_ref.shape[1], CONV_COL_CHUNK):
        cols = slice(c0, c0 + CONV_COL_CHUNK)
        acc = _tile_dot(a, w_ref, cols, w_transposed)
        pad_ref[SUBLANES:SUBLANES + tm, cols] = acc
        conv = cb_ref[:, cols] + pad_ref[first:first + tm, cols] * cw_ref[0:1, cols]
        for i in range(1, CONV_WIDTH):
            conv = conv + pad_ref[first + i:first + i + tm, cols] * cw_ref[i:i + 1, cols]
        o_ref[:, cols] = _silu(conv)
        last_rows = acc[tm - SUBLANES:, :]
        pad_ref[0:SUBLANES, cols] = last_rows
        tail_ref[:, cols] = last_rows


def _write_epilogue(acc, epilogue, r_ref, o_ref):
    if epilogue == "residual":
        o_ref[...] = r_ref[...] + acc
    elif epilogue == "relu2":
        u = jnp.maximum(acc, 0.0)
        o_ref[...] = (u * u).astype(o_ref.dtype)
    elif epilogue == "heads":
        for j in range(o_ref.shape[1]):
            _store_head(o_ref, j, acc[:, j * LANES:(j + 1) * LANES])
    else:
        o_ref[...] = acc.astype(o_ref.dtype)


def _mm_kernel(*refs, epilogue, cast_w, seq_tiles, w_transposed, row_tiles, second):
    refs = list(refs)
    wbf_ref = pad_ref = None
    if epilogue == "conv_silu":
        pad_ref = refs.pop()
    if cast_w:
        wbf_ref = refs.pop()
    it = iter(refs)
    a_ref, w_ref = next(it), next(it)
    r_ref = next(it) if epilogue == "residual" else None
    cw_ref, cb_ref = (next(it), next(it)) if epilogue == "conv_silu" else (None, None)
    a2_ref = next(it) if second else None
    r2_ref = next(it) if second and epilogue == "residual" else None
    o_ref = next(it)
    tail_ref = next(it) if epilogue == "conv_silu" else None
    o2_ref = next(it) if second else None
    w_ref = _bf16_weights(w_ref, wbf_ref)

    def first_part():
        if epilogue == "conv_silu":
            _conv_silu_tiles(a_ref, w_ref, cw_ref, cb_ref, o_ref, tail_ref, pad_ref,
                             seq_tiles=seq_tiles, w_transposed=w_transposed)
        else:
            _write_epilogue(_tile_dot(a_ref[...], w_ref, slice(None), w_transposed), epilogue, r_ref, o_ref)

    def second_part():
        _write_epilogue(_tile_dot(a2_ref[...], w_ref, slice(None), w_transposed),
                        "none" if epilogue == "conv_silu" else epilogue, r2_ref, o2_ref)

    if second:
        pl.when(pl.program_id(1) < row_tiles)(first_part)
        pl.when(pl.program_id(1) == row_tiles)(second_part)
    else:
        first_part()


W_ROW_ALIGN = 64


def matmul(a, w, *, tm, tn, n=None, w_rows=None, out_dtype=F32, epilogue="none", residual=None,
           conv=None, seq=None, a2=None, residual2=None, weight_buffers=2, name="matmul"):
    m, k = a.shape
    n = w.shape[1] if n is None else n
    tm = min(tm, m)
    tn = min(tn, n)
    assert m % tm == 0 and n % tn == 0
    row_tiles = m // tm
    second = a2 is not None
    cast_w = w.dtype != BF16
    row = (lambda i: jnp.minimum(i, row_tiles - 1)) if second else (lambda i: i)
    w_mode = dict(pipeline_mode=pl.Buffered(weight_buffers)) if weight_buffers != 2 else {}
    if w_rows is None:
        w_spec = pl.BlockSpec((k, tn), lambda j, i: (0, j), **w_mode)
        w_tile = (k, tn)
    else:
        w_spec = pl.BlockSpec((pl.Element(tn), pl.Element(k)),
                              lambda j, i: (pl.multiple_of(w_rows(j), W_ROW_ALIGN), 0), **w_mode)
        w_tile = (tn, k)
    in_specs = [pl.BlockSpec((tm, k), lambda j, i: (row(i), 0)), w_spec]
    args = [a, w]
    scratch = [pltpu.VMEM(w_tile, BF16)] if cast_w else []
    heads = epilogue == "heads"

    def out_for(rows, tile_rows, row_index):
        if heads:
            return (pl.BlockSpec((tile_rows, tn // LANES, LANES), lambda j, i: (row_index(i), j, 0)),
                    jax.ShapeDtypeStruct((rows, n // LANES, LANES), out_dtype))
        return (pl.BlockSpec((tile_rows, tn), lambda j, i: (row_index(i), j)),
                jax.ShapeDtypeStruct((rows, n), out_dtype))

    spec, shape = out_for(m, tm, row)
    out_specs, out_shapes = [spec], [shape]
    seq_tiles = None
    if epilogue == "residual":
        in_specs.append(pl.BlockSpec((tm, tn), lambda j, i: (row(i), j)))
        args.append(residual)
    elif epilogue == "conv_silu":
        assert seq % tm == 0
        seq_tiles = seq // tm
        conv_w, conv_b = conv
        in_specs += [pl.BlockSpec((CONV_WIDTH, tn), lambda j, i: (0, j)),
                     pl.BlockSpec((1, tn), lambda j, i: (0, j))]
        args += [conv_w, conv_b.reshape(1, n)]
        out_specs.append(pl.BlockSpec((None, SUBLANES, tn), lambda j, i: (row(i) // seq_tiles, 0, j)))
        out_shapes.append(jax.ShapeDtypeStruct((m // seq, SUBLANES, n), F32))
        scratch.append(pltpu.VMEM((tm + SUBLANES, tn), F32))
    if second:
        m2 = a2.shape[0]
        in_specs.append(pl.BlockSpec((m2, k), lambda j, i: (0, 0)))
        args.append(a2)
        if epilogue == "residual":
            in_specs.append(pl.BlockSpec((m2, tn), lambda j, i: (0, j)))
            args.append(residual2)
        spec, shape = out_for(m2, m2, lambda i: 0)
        out_specs.append(spec)
        out_shapes.append(shape)
    outs = pl.pallas_call(
        functools.partial(_mm_kernel, epilogue=epilogue, cast_w=cast_w, seq_tiles=seq_tiles,
                          w_transposed=w_rows is not None, row_tiles=row_tiles, second=second),
        grid=(n // tn, row_tiles + (1 if second else 0)),
        in_specs=in_specs,
        out_specs=out_specs,
        out_shape=out_shapes,
        scratch_shapes=scratch,
        compiler_params=_params(2),
        name=name,
    )(*args)
    return outs[0] if len(outs) == 1 else tuple(outs)


def _merge_kernel(*refs, cast_w, row_tiles, second):
    refs = list(refs)
    wsbf_ref = wabf_ref = None
    if cast_w:
        wabf_ref = refs.pop()
        wsbf_ref = refs.pop()
    it = iter(refs)
    first_in = [next(it) for _ in range(4)]
    ws_ref, wa_ref = next(it), next(it)
    second_in = [next(it) for _ in range(4)] if second else None
    out_ref = next(it)
    out2_ref = next(it) if second else None
    ws_ref = _bf16_weights(ws_ref, wsbf_ref)
    wa_ref = _bf16_weights(wa_ref, wabf_ref)

    def part(y_ref, o_ref, gs_ref, ga_ref, dst_ref):
        ps = jnp.dot(y_ref[...], ws_ref[...], preferred_element_type=F32)
        pa = jnp.dot(o_ref[...], wa_ref[...], preferred_element_type=F32)
        dst_ref[...] = (_sigmoid(gs_ref[...]) * ps + _sigmoid(ga_ref[...]) * pa).astype(dst_ref.dtype)

    if second:
        pl.when(pl.program_id(1) < row_tiles)(lambda: part(*first_in, out_ref))
        pl.when(pl.program_id(1) == row_tiles)(lambda: part(*second_in, out2_ref))
    else:
        part(*first_in, out_ref)


def merge_proj(y_ssm, o_att, gates, w_ssm, w_att, *, tm, tn, second=None):
    m = y_ssm.shape[0]
    n = w_ssm.shape[1]
    tm = min(tm, m)
    n_col = n // tn
    row_tiles = m // tm
    row = (lambda i: jnp.minimum(i, row_tiles - 1)) if second else (lambda i: i)
    cast_w = w_ssm.dtype != BF16
    scratch = []
    if cast_w:
        scratch = [pltpu.VMEM((w_ssm.shape[0], tn), BF16), pltpu.VMEM((w_att.shape[0], tn), BF16)]

    def operand_specs(tile_rows, row_index):
        return [pl.BlockSpec((tile_rows, y_ssm.shape[1]), lambda j, i: (row_index(i), 0)),
                pl.BlockSpec((tile_rows, o_att.shape[1]), lambda j, i: (row_index(i), 0)),
                pl.BlockSpec((tile_rows, tn), lambda j, i: (row_index(i), j)),
                pl.BlockSpec((tile_rows, tn), lambda j, i: (row_index(i), j + n_col))]

    in_specs = operand_specs(tm, row) + [pl.BlockSpec((w_ssm.shape[0], tn), lambda j, i: (0, j)),
                                        pl.BlockSpec((w_att.shape[0], tn), lambda j, i: (0, j))]
    args = [y_ssm, o_att, gates, gates, w_ssm, w_att]
    out_specs = [pl.BlockSpec((tm, tn), lambda j, i: (row(i), j))]
    out_shapes = [jax.ShapeDtypeStruct((m, n), BF16)]
    if second:
        y2, o2, g2 = second
        m2 = y2.shape[0]
        in_specs += operand_specs(m2, lambda i: 0)
        args += [y2, o2, g2, g2]
        out_specs.append(pl.BlockSpec((m2, tn), lambda j, i: (0, j)))
        out_shapes.append(jax.ShapeDtypeStruct((m2, n), BF16))
    outs = pl.pallas_call(
        functools.partial(_merge_kernel, cast_w=cast_w, row_tiles=row_tiles, second=bool(second)),
        grid=(n_col, row_tiles + (1 if second else 0)),
        in_specs=in_specs,
        out_specs=out_specs,
        out_shape=out_shapes,
        scratch_shapes=scratch,
        compiler_params=_params(2),
        name="merge_proj",
    )(*args)
    return outs[0] if len(outs) == 1 else tuple(outs)


def _conv_sample_kernel(x0_ref, x1_ref, x2_ref, x3_ref, w_ref, b_ref, o_ref):
    acc = b_ref[...] + x0_ref[...] * w_ref[0:1, :]
    for i, x_ref in enumerate((x1_ref, x2_ref, x3_ref), start=1):
        acc = acc + x_ref[...] * w_ref[i:i + 1, :]
    o_ref[...] = _silu(acc)


def conv_sample(xpad, conv_w, conv_b, *, steps, tc=512):
    bsz = xpad.shape[0]
    c = conv_w.shape[1]
    nct = c // tc
    x_specs = [pl.BlockSpec((bsz, tc), functools.partial(lambda t, j, i: (0, (t + i) * nct + j), i=i))
               for i in range(CONV_WIDTH)]
    return pl.pallas_call(
        _conv_sample_kernel,
        grid=(steps, nct),
        in_specs=x_specs + [pl.BlockSpec((CONV_WIDTH, tc), lambda t, j: (0, j)),
                            pl.BlockSpec((1, tc), lambda t, j: (0, j))],
        out_specs=pl.BlockSpec((bsz, tc), lambda t, j: (0, t * nct + j)),
        out_shape=jax.ShapeDtypeStruct((bsz, steps * c), F32),
        compiler_params=_params(2),
        name="conv_sample",
    )(xpad, xpad, xpad, xpad, conv_w, conv_b.reshape(1, c))


def _split3(a):
    hi = a.astype(BF16)
    rest = a - hi.astype(F32)
    mid = rest.astype(BF16)
    return hi, mid, (rest - mid.astype(F32)).astype(BF16)


def _ssd_group(gl, x_ref, b_ref, c_ref, dtc_ref, dtr_ref, z_ref, pc_ref, pr_ref, nrm_ref, y_ref, h_ref,
               ht_ref, *, q_len, valid_len, last_chunk):
    nh, hd = HEADS_PER_SSM_GROUP, SSM_HEAD_DIM
    pair_w = 2 * hd
    gw = SSM_GROUP_WIDTH
    x_cols = slice(gl * gw, (gl + 1) * gw)
    n_cols = slice(gl * SSM_STATE, (gl + 1) * SSM_STATE)

    pc = pc_ref[gl]
    pr = pr_ref[gl]
    dt_c = _softplus(dtc_ref[gl] + pc[0:1, :])
    dt_r = _softplus(dtr_ref[gl] + pr[:, 0:1])
    if valid_len < q_len:
        dt_c = jnp.where(lax.broadcasted_iota(jnp.int32, dt_c.shape, 0) < valid_len, dt_c, 0.0)
        dt_r = jnp.where(lax.broadcasted_iota(jnp.int32, dt_r.shape, 1) < valid_len, dt_r, 0.0)
    a_c = dt_c * (-jnp.exp(pc[1:2, :]))
    a_r = dt_r * (-jnp.exp(pr[:, 1:2]))
    ti = lax.broadcasted_iota(jnp.int32, (q_len, q_len), 0)
    tj = lax.broadcasted_iota(jnp.int32, (q_len, q_len), 1)
    causal = ti >= tj
    tril = causal.astype(BF16)
    triu = (ti <= tj).astype(BF16)
    acum_c = sum(jnp.dot(tril, part, preferred_element_type=F32) for part in _split3(a_c))
    acum_r = sum(jnp.dot(part, triu, preferred_element_type=F32) for part in _split3(a_r))
    end_c = acum_c[q_len - 1:q_len, :]
    dd_c = jnp.exp(end_c - acum_c) * dt_c
    src_r = acum_r - jnp.log(dt_r)

    x = x_ref[:, x_cols]
    cmat = c_ref[:, n_cols].astype(BF16)
    if ht_ref is None:
        bmat = b_ref[:, n_cols].astype(BF16)
        cb = _dot_nt(cmat, bmat)
        h_prev = h_ref[gl * nh:(gl + 1) * nh]
        y_off_all = _dot_nt(cmat, h_prev.reshape(nh * hd, SSM_STATE).astype(BF16))
    else:
        b_t = jnp.transpose(b_ref[:, n_cols]).astype(BF16)
        cb = jnp.dot(cmat, b_t, preferred_element_type=F32)
        ht_prev = ht_ref[gl]
        y_off_all = jnp.dot(cmat, ht_prev.astype(BF16), preferred_element_type=F32)

    lane = lax.broadcasted_iota(jnp.int32, (q_len, pair_w), 1)
    first_half = lane < hd

    def pair_bcast(cols, j0):
        return jnp.where(first_half,
                         jnp.broadcast_to(cols[:, j0:j0 + 1], (q_len, pair_w)),
                         jnp.broadcast_to(cols[:, j0 + 1:j0 + 2], (q_len, pair_w)))

    ys = []
    xws = []
    sumsq = jnp.zeros((q_len, 1), F32)
    for i in range(nh // 2):
        j0 = 2 * i
        x_pair = x[:, i * pair_w:(i + 1) * pair_w]
        y_diag = jnp.zeros((q_len, pair_w), F32)
        for j in (j0, j0 + 1):
            seg = jnp.broadcast_to(acum_c[:, j:j + 1], (q_len, q_len)) - src_r[j:j + 1, :]
            w = (cb * jnp.exp(jnp.where(causal, seg, NEG_INF))).astype(BF16)
            own = first_half if j == j0 else jnp.logical_not(first_half)
            y_diag = y_diag + jnp.dot(w, jnp.where(own, x_pair, 0.0).astype(BF16),
                                      preferred_element_type=F32)
        e_pair = jnp.exp(pair_bcast(acum_c, j0))
        d_pair = jnp.where(first_half[0:1, :], pc[2:3, j0:j0 + 1], pc[2:3, j0 + 1:j0 + 2])
        y = y_diag + y_off_all[:, i * pair_w:(i + 1) * pair_w] * e_pair + d_pair * x_pair
        y = y * _silu(z_ref[:, gl * gw + i * pair_w:gl * gw + (i + 1) * pair_w])
        sumsq = sumsq + jnp.sum(y * y, axis=-1, keepdims=True)
        ys.append(y)

        xw = (x_pair * pair_bcast(dd_c, j0)).astype(BF16)
        if ht_ref is None:
            s_pair = _dot_tn(xw, bmat)
            for jj, j in enumerate((j0, j0 + 1)):
                chunk_decay = jnp.exp(acum_r[j:j + 1, q_len - 1:q_len])
                h_ref[gl * nh + j] = h_prev[j] * chunk_decay + s_pair[jj * hd:(jj + 1) * hd, :]
        else:
            xws.append(xw)

    if ht_ref is not None:
        end_decay = jnp.exp(end_c)
        decay_row = jnp.concatenate(
            [jnp.where(first_half[0:1, :], end_decay[:, 2 * i:2 * i + 1], end_decay[:, 2 * i + 1:2 * i + 2])
             for i in range(nh // 2)], axis=1)
        ht_new = ht_prev * decay_row + jnp.dot(b_t, jnp.concatenate(xws, axis=1), preferred_element_type=F32)
        ht_ref[gl] = ht_new

        @pl.when(last_chunk)
        def _():
            for i in range(nh // 2):
                both = jnp.transpose(ht_new[:, i * pair_w:(i + 1) * pair_w])
                h_ref[gl * nh + 2 * i] = both[:hd, :]
                h_ref[gl * nh + 2 * i + 1] = both[hd:, :]

    inv = lax.rsqrt(sumsq * (1.0 / (nh * hd)) + EPS)
    for i in range(nh // 2):
        sl = slice(gl * gw + i * pair_w, gl * gw + (i + 1) * pair_w)
        y_ref[:, sl] = (ys[i] * inv * nrm_ref[:, sl]).astype(y_ref.dtype)


def _ssd_kernel(*refs, q_len, valid_len, has_h0, groups_per_step):
    if has_h0:
        h0_ref = refs[9]
        refs = refs[:9] + refs[10:] + (None,)
    h_ref, ht_ref = refs[-2:]
    chunk = pl.program_id(2)

    @pl.when(chunk == 0)
    def _():
        if has_h0:
            h_ref[...] = h0_ref[...]
        else:
            ht_ref[...] = jnp.zeros(ht_ref.shape, F32)

    for gl in range(groups_per_step):
        _ssd_group(gl, *refs, q_len=q_len, valid_len=valid_len,
                   last_chunk=chunk == pl.num_programs(2) - 1)


def ssd(u, dt_raw, z, dt_bias, a_log, d_skip, ssm_norm, h0, *, batch, seq, q_len, valid_len, groups_per_step):
    nh, g, gps = HEADS_PER_SSM_GROUP, SSM_GROUPS, groups_per_step
    nc = seq // q_len
    m = batch * seq
    dt4 = dt_raw[:, :SSM_HEADS].reshape(batch * nc, q_len, g, nh)
    dtc = jnp.transpose(dt4, (0, 2, 1, 3))
    dtr = jnp.transpose(dt4, (0, 2, 3, 1))
    pcol = jnp.stack([dt_bias, a_log, d_skip]).reshape(3, g, nh).transpose(1, 0, 2)
    prow = jnp.transpose(pcol, (0, 2, 1))
    b_off = D_INNER // (gps * SSM_STATE)
    c_off = b_off + g // gps
    row = lambda b, gi, c: b * nc + c
    in_specs = [
        pl.BlockSpec((q_len, gps * SSM_GROUP_WIDTH), lambda b, gi, c: (row(b, gi, c), gi)),
        pl.BlockSpec((q_len, gps * SSM_STATE), lambda b, gi, c: (row(b, gi, c), b_off + gi)),
        pl.BlockSpec((q_len, gps * SSM_STATE), lambda b, gi, c: (row(b, gi, c), c_off + gi)),
        pl.BlockSpec((None, gps, q_len, nh), lambda b, gi, c: (row(b, gi, c), gi, 0, 0)),
        pl.BlockSpec((None, gps, nh, q_len), lambda b, gi, c: (row(b, gi, c), gi, 0, 0)),
        pl.BlockSpec((q_len, gps * SSM_GROUP_WIDTH), lambda b, gi, c: (row(b, gi, c), gi)),
        pl.BlockSpec((gps, 3, nh), lambda b, gi, c: (gi, 0, 0)),
        pl.BlockSpec((gps, nh, 3), lambda b, gi, c: (gi, 0, 0)),
        pl.BlockSpec((1, gps * SSM_GROUP_WIDTH), lambda b, gi, c: (0, gi)),
    ]
    args = [u, u, u, dtc, dtr, z, pcol, prow, ssm_norm.reshape(1, D_INNER)]
    h_spec = pl.BlockSpec((None, gps * nh, SSM_HEAD_DIM, SSM_STATE), lambda b, gi, c: (b, gi, 0, 0))
    scratch = []
    if h0 is not None:
        in_specs.append(h_spec)
        args.append(h0)
    else:
        scratch.append(pltpu.VMEM((gps, SSM_STATE, SSM_GROUP_WIDTH), F32))
    return pl.pallas_call(
        functools.partial(_ssd_kernel, q_len=q_len, valid_len=valid_len, has_h0=h0 is not None,
                          groups_per_step=gps),
        grid=(batch, g // gps, nc),
        in_specs=in_specs,
        out_specs=[pl.BlockSpec((q_len, gps * SSM_GROUP_WIDTH), lambda b, gi, c: (row(b, gi, c), gi)), h_spec],
        out_shape=[jax.ShapeDtypeStruct((m, D_INNER), BF16),
                   jax.ShapeDtypeStruct((batch, SSM_HEADS, SSM_HEAD_DIM, SSM_STATE), F32)],
        scratch_shapes=scratch,
        compiler_params=_params(3),
        name="ssd",
    )(*args)


def _attn_prompt_kernel(*refs, kb, has_prev):
    if has_prev:
        q_ref, kp_ref, kc_ref, vp_ref, vc_ref, mb_ref, o_ref, l_ref = refs
    else:
        q_ref, kc_ref, vc_ref, mb_ref, o_ref, l_ref = refs
    not_first = pl.program_id(2) > 0
    scale = ATT_HEAD_DIM ** -0.5
    for h in range(ATT_HEADS_PER_GROUP):
        q = (_load_head(q_ref, h) * scale).astype(BF16)
        s_c = _dot_nt(q, _load_head(kc_ref, h).astype(BF16)) + mb_ref[h, :, kb:]
        m = jnp.max(s_c, axis=-1, keepdims=True)
        if has_prev:
            s_p = _dot_nt(q, _load_head(kp_ref, h).astype(BF16)) + mb_ref[h, :, :kb]
            s_p = jnp.where(not_first, s_p, NEG_INF)
            m = jnp.maximum(m, jnp.max(s_p, axis=-1, keepdims=True))
        p_c = jnp.exp(s_c - m)
        den = jnp.sum(p_c, axis=-1, keepdims=True)
        o = jnp.dot(p_c.astype(BF16), _load_head(vc_ref, h).astype(BF16), preferred_element_type=F32)
        if has_prev:
            p_p = jnp.exp(s_p - m)
            den = den + jnp.sum(p_p, axis=-1, keepdims=True)
            o = o + jnp.dot(p_p.astype(BF16), _load_head(vp_ref, h).astype(BF16), preferred_element_type=F32)
        _store_head(o_ref, h, o / den)
        _store_head(l_ref, h, jnp.broadcast_to(m + jnp.log(den), (kb, ATT_HEAD_DIM)))


def attn_prompt(q, kv, mbias, *, group, batch, seq, dil, kb):
    m_res = seq // dil
    nb = m_res // kb
    has_prev = nb > 1
    nh = ATT_HEADS_PER_GROUP
    q4 = q.reshape(batch * m_res, dil, q.shape[1], ATT_HEAD_DIM)
    kv4 = kv.reshape(batch * m_res, dil, 2 * nh, ATT_HEAD_DIM)

    def cur(head_block):
        return pl.BlockSpec((kb, None, nh, ATT_HEAD_DIM), lambda b, r, n: (b * nb + n, r, head_block, 0))

    def prev(head_block):
        return pl.BlockSpec((kb, None, nh, ATT_HEAD_DIM),
                            lambda b, r, n: (b * nb + jnp.maximum(n - 1, 0), r, head_block, 0))

    if has_prev:
        in_specs = [cur(group), prev(0), cur(0), prev(1), cur(1)]
        args = [q4, kv4, kv4, kv4, kv4]
    else:
        in_specs = [cur(group), cur(0), cur(1)]
        args = [q4, kv4, kv4]
    in_specs.append(pl.BlockSpec(mbias.shape, lambda b, r, n: (0, 0, 0)))
    args.append(mbias)
    o, lse = pl.pallas_call(
        functools.partial(_attn_prompt_kernel, kb=kb, has_prev=has_prev),
        grid=(batch, dil, nb),
        in_specs=in_specs,
        out_specs=[cur(0), cur(0)],
        out_shape=[jax.ShapeDtypeStruct((batch * m_res, dil, nh, ATT_HEAD_DIM), F32)] * 2,
        compiler_params=_params(3),
        name=f"attn_prompt_g{group}",
    )(*args)
    return o.reshape(batch * seq, nh, ATT_HEAD_DIM), lse.reshape(batch * seq, nh, ATT_HEAD_DIM)


def _attn_sample_kernel(*refs, n_sets, steps, q_rows):
    q_ref, new_ref = refs[:2]
    cache_refs = refs[2:2 + n_sets]
    bias_ref, o_ref, l_ref = refs[2 + n_sets:]
    gw = ATT_GROUP_WIDTH
    nh = ATT_HEADS_PER_GROUP
    hd = ATT_HEAD_DIM
    scale = ATT_HEAD_DIM ** -0.5
    bb = q_ref.shape[0]
    cache_rows = cache_refs[0].shape[1]
    new_rows = new_ref.shape[1]
    row_stride = 2 * nh
    per_set = steps // n_sets
    sub = lax.broadcasted_iota(jnp.int32, (nh, gw), 0)
    own = sub == lax.broadcasted_iota(jnp.int32, (nh, gw), 1) // hd
    for bi in range(bb):
        new_heads = [_load_rows(new_ref, bi * new_rows * 2 * nh + h, new_rows, 2 * nh) for h in range(2 * nh)]
        for ci in range(n_sets):
            q_blocks = []
            for tt in range(per_set):
                q_t = q_ref[bi, ci * per_set + tt] * scale
                q_blocks.append(jnp.where(own, jnp.concatenate([q_t] * nh, axis=1), 0.0))
            if q_rows > per_set * nh:
                q_blocks.append(jnp.zeros((q_rows - per_set * nh, gw), F32))
            qbd = jnp.concatenate(q_blocks, axis=0).astype(BF16)
            base = bi * cache_rows * row_stride
            cache_ref = cache_refs[ci]
            k_all = jnp.concatenate(
                [jnp.concatenate([_load_rows(cache_ref, base + h, cache_rows, row_stride), new_heads[h]], axis=0)
                 for h in range(nh)], axis=1).astype(BF16)
            v_all = jnp.concatenate(
                [jnp.concatenate([_load_rows(cache_ref, base + nh + h, cache_rows, row_stride), new_heads[nh + h]],
                                 axis=0) for h in range(nh)], axis=1).astype(BF16)
            s = _dot_nt(qbd, k_all) + bias_ref[ci]
            m = jnp.max(s, axis=-1, keepdims=True)
            p = jnp.exp(s - m)
            den = jnp.sum(p, axis=-1, keepdims=True)
            o_all = jnp.dot(p.astype(BF16), v_all, preferred_element_type=F32) / den
            lse = m + jnp.log(den)
            for tt in range(per_set):
                t = ci * per_set + tt
                rows = slice(tt * nh, (tt + 1) * nh)
                o_t = jnp.zeros((nh, hd), F32)
                for h in range(nh):
                    o_t = o_t + jnp.where(sub[:, :hd] == h, o_all[rows, h * hd:(h + 1) * hd], 0.0)
                o_ref[bi, t] = o_t
                l_ref[bi, t] = jnp.broadcast_to(lse[rows, :], (nh, hd))


def attn_sample(q, kv_new_pad, cache, bias, *, group, batch, steps, dil, batch_block):
    nh, hd = ATT_HEADS_PER_GROUP, ATT_HEAD_DIM
    bb = batch_block
    n_sets, q_rows, n_keys_tot = bias.shape
    new_rows = kv_new_pad.shape[1]
    cache_rows = n_keys_tot - new_rows
    q4 = q.reshape(batch, steps, q.shape[1], hd)
    if n_sets == 1:
        cache_view = cache.reshape(batch, cache_rows, 2 * nh, hd)
        cache_specs = [pl.BlockSpec((bb, cache_rows, 2 * nh, hd), lambda b: (b, 0, 0, 0))]
    else:
        cache_view = cache.reshape(batch, cache_rows, dil, 2 * nh, hd)
        cache_specs = [pl.BlockSpec((bb, cache_rows, None, 2 * nh, hd),
                                    functools.partial(lambda b, t: (b, 0, t, 0, 0), t=t))
                       for t in range(n_sets)]
    out_spec = pl.BlockSpec((bb, steps, nh, hd), lambda b: (b, 0, 0, 0))
    o, lse = pl.pallas_call(
        functools.partial(_attn_sample_kernel, n_sets=n_sets, steps=steps, q_rows=q_rows),
        grid=(batch // bb,),
        in_specs=[pl.BlockSpec((bb, steps, nh, hd), lambda b: (b, 0, group, 0)),
                  pl.BlockSpec((bb, new_rows, 2 * nh, hd), lambda b: (b, 0, 0, 0))]
                 + cache_specs
                 + [pl.BlockSpec(bias.shape, lambda b: (0, 0, 0))],
        out_specs=[out_spec, out_spec],
        out_shape=[jax.ShapeDtypeStruct((batch, steps, nh, hd), F32)] * 2,
        compiler_params=_params(1),
        name=f"attn_sample_g{group}",
    )(q4, kv_new_pad, *([cache_view] * n_sets), bias)
    return o.reshape(batch * steps, nh, hd), lse.reshape(batch * steps, nh, hd)


def _attn_combine_kernel(o0, o1, o2, l0, l1, l2, out_ref, mix_ref):
    ls = (l0[...], l1[...], l2[...])
    m = jnp.maximum(jnp.maximum(ls[0], ls[1]), ls[2])
    ws = [jnp.exp(l - m) for l in ls]
    num = ws[0] * o0[...] + ws[1] * o1[...] + ws[2] * o2[...]
    mix_ref[...] = num / (ws[0] + ws[1] + ws[2])
    for h in range(ATT_HEADS_PER_GROUP):
        out_ref[:, h * ATT_HEAD_DIM:(h + 1) * ATT_HEAD_DIM] = _load_head(mix_ref, h).astype(out_ref.dtype)


def attn_combine(outs, lses, tm=512):
    m, nh, hd = outs[0].shape
    tm = min(tm, m)
    spec = pl.BlockSpec((tm, nh, hd), lambda i: (i, 0, 0))
    return pl.pallas_call(
        _attn_combine_kernel,
        grid=(m // tm,),
        in_specs=[spec] * 6,
        out_specs=pl.BlockSpec((tm, nh * hd), lambda i: (i, 0)),
        out_shape=jax.ShapeDtypeStruct((m, nh * hd), BF16),
        scratch_shapes=[pltpu.VMEM((tm, nh, hd), F32)],
        compiler_params=_params(1),
        name="attn_combine",
    )(*outs, *lses)


def _t5_bucket(dist):
    max_exact = N_BUCKETS // 2
    d = np.maximum(dist, max_exact).astype(np.float32)
    large = max_exact + (np.log(d / max_exact) / np.log(MAX_DISTANCE / max_exact)
                         * (N_BUCKETS - max_exact)).astype(np.int32)
    large = np.minimum(large, N_BUCKETS - 1)
    return np.where(dist < max_exact, dist, large).astype(np.int32)


def _group_bias(rel_bias, g):
    win, dil = ATT_PATTERNS[g]
    buckets = _t5_bucket(np.arange(win // dil + 1) * dil)
    return rel_bias[buckets][:, g * ATT_HEADS_PER_GROUP:(g + 1) * ATT_HEADS_PER_GROUP]


def _prompt_mask_bias(bias_g, n_keys, kb):
    period = 3 * kb
    offs = np.arange(period)
    offs = np.where(offs >= 2 * kb, offs - period, offs)
    j_of = np.clip(kb - offs, 0, n_keys)
    a = jnp.transpose(bias_g[j_of]).astype(F32)
    nh = a.shape[0]
    tiled = jnp.tile(a, (1, kb))[:, :kb * (period - 1)].reshape(nh, kb, period - 1)
    rel = np.arange(kb)[:, None] + kb - np.arange(2 * kb)[None, :]
    valid = (rel >= 0) & (rel <= n_keys)
    return jnp.where(valid[None], tiled[:, :, :2 * kb], NEG_INF)


def _sample_mask_bias(bias_g, *, cache_len, dil, n_keys, steps, new_rows, shared_cache, q_rows):
    nh = ATT_HEADS_PER_GROUP
    n_cache = 1 if shared_cache else steps
    per_cache = steps // n_cache
    rows_per_cache = cache_len if shared_cache else cache_len // dil
    blocks = []
    for ci in range(n_cache):
        if shared_cache:
            cache_pos = np.arange(rows_per_cache)
        else:
            cache_pos = ci + dil * np.arange(rows_per_cache)
        key_pos = np.concatenate([cache_pos, cache_len + np.arange(new_rows)])
        key_live = np.concatenate([np.ones(rows_per_cache, bool), np.arange(new_rows) < steps])
        rows = []
        for tt in range(per_cache):
            t = ci * per_cache + tt
            diff = cache_len + t - key_pos
            valid = key_live & (diff >= 0) & (diff % dil == 0) & (diff // dil <= n_keys)
            j = np.clip(diff // dil, 0, n_keys)
            rows.append(jnp.where(valid[None, :], bias_g[j].T.astype(F32), NEG_INF))
        blk = jnp.concatenate(rows, axis=0)
        pad = q_rows - per_cache * nh
        if pad:
            blk = jnp.concatenate([blk, jnp.zeros((pad, blk.shape[1]), F32)], axis=0)
        blocks.append(blk)
    return jnp.stack(blocks)


SAMPLE_NEW_ROWS = 16
SAMPLE_Q_ROWS_MIN = 16
SSD_SAMPLE_Q = 16
SAMPLE_SHARED_BATCH_BLOCK = 4
SAMPLE_DILATED_BATCH_BLOCK = 2
SSD_PROMPT_GROUPS_PER_STEP = 4


IN_SIZES = (D_INNER, CONV_DIM, SSM_HEADS, ATT_WIDTH, ATT_WIDTH, ATT_WIDTH, D_MODEL, D_MODEL)
IN_OFFSETS = tuple(int(o) for o in np.concatenate([[0], np.cumsum(IN_SIZES)]))


IN_TILE = 1024
WIDE_TILE = 2048
assert all(o % W_ROW_ALIGN == 0 for o in IN_OFFSETS)


def _in_rows(seg, stride=IN_TILE, first=0):
    return lambda j: IN_OFFSETS[seg] + first + j * stride


PROMPT_ROW_TILE = 1024


def _ssd_branches(u_p, u_s, dt_p, dt_s, z_p, z_s, w, conv_state, ssm_state, *, bp, sp, bs, ss):
    params = (w["dt_bias"], w["a_log"], w["d_skip"], w["ssm_norm"])
    y_p, ssm_p = ssd(u_p, dt_p, z_p, *params, None, batch=bp, seq=sp, q_len=SSD_CHUNK,
                     valid_len=SSD_CHUNK, groups_per_step=SSD_PROMPT_GROUPS_PER_STEP)
    qp = SSD_SAMPLE_Q
    pad_rows = lambda a: jnp.pad(a.reshape(bs, ss, a.shape[-1]),
                                 ((0, 0), (0, qp - ss), (0, 0))).reshape(bs * qp, a.shape[-1])
    y_pad, ssm_s = ssd(pad_rows(u_s), pad_rows(dt_s), pad_rows(z_s), *params, ssm_state,
                       batch=bs, seq=qp, q_len=qp, valid_len=ss, groups_per_step=SSM_GROUPS)
    y_s = y_pad.reshape(bs, qp, D_INNER)[:, :ss].reshape(bs * ss, D_INNER)
    return y_p, y_s, ssm_p, ssm_s


def _attention_branches(q_p, q_s, kvs_p, kvs_s, caches, rel_bias, *, bp, sp, bs, ss):
    nh, hd = ATT_HEADS_PER_GROUP, ATT_HEAD_DIM
    outs_p, lses_p, outs_s, lses_s = [], [], [], []
    for g, (win, dil) in enumerate(ATT_PATTERNS):
        n_keys = win // dil
        bias_g = _group_bias(rel_bias, g)
        mb = _prompt_mask_bias(bias_g, n_keys, n_keys)
        o, lse = attn_prompt(q_p, kvs_p[g], mb, group=g, batch=bp, seq=sp, dil=dil, kb=n_keys)
        outs_p.append(o)
        lses_p.append(lse)

        cache = caches[g]
        cache_len = cache.shape[1]
        shared = dil == 1
        q_rows = max(SAMPLE_Q_ROWS_MIN, (ss if shared else 1) * nh)
        sb = _sample_mask_bias(bias_g, cache_len=cache_len, dil=dil, n_keys=n_keys, steps=ss,
                               new_rows=SAMPLE_NEW_ROWS, shared_cache=shared, q_rows=q_rows)
        kv_new = jnp.pad(kvs_s[g].reshape(bs, ss, 2 * nh, hd),
                         ((0, 0), (0, SAMPLE_NEW_ROWS - ss), (0, 0), (0, 0)))
        o, lse = attn_sample(q_s, kv_new, cache, sb, group=g, batch=bs, steps=ss, dil=dil,
                             batch_block=SAMPLE_SHARED_BATCH_BLOCK if shared else SAMPLE_DILATED_BATCH_BLOCK)
        outs_s.append(o)
        lses_s.append(lse)
    return attn_combine(outs_p, lses_p), attn_combine(outs_s, lses_s)


def _forward(x_p, x_s, w, caches, conv_state, ssm_state, rel_bias, *, bp, sp, bs, ss):
    nh, hd = ATT_HEADS_PER_GROUP, ATT_HEAD_DIM
    tm = PROMPT_ROW_TILE
    h_p = rmsnorm(x_p, w["norm_mix"], BF16)
    h_s = rmsnorm(x_s, w["norm_mix"], BF16)
    proj = lambda name, tn=IN_TILE, rows=tm, **kw: matmul(h_p, w["in_t"], a2=h_s, tm=rows, tn=tn, name=name, **kw)
    wide = dict(tn=WIDE_TILE, weight_buffers=1)
    wide_in = dict(rows=tm // 2, **wide)
    z_p, z_s = proj("in_z", n=IN_SIZES[0], w_rows=_in_rows(0, stride=WIDE_TILE), **wide_in)
    dt_p, dt_s = matmul(h_p, w["in_t"], a2=h_s, tm=tm, tn=LANES, n=LANES, w_rows=_in_rows(2), name="in_dt")
    q_p, q_s = proj("in_q", n=IN_SIZES[3], w_rows=_in_rows(3), epilogue="heads")
    kvs = [proj(f"in_kv{g}", n=2 * ATT_GROUP_WIDTH, epilogue="heads",
                w_rows=_in_rows(4, stride=IN_OFFSETS[5] - IN_OFFSETS[4], first=g * ATT_GROUP_WIDTH))
           for g in range(ATT_N_GROUPS)]
    kvs_p, kvs_s = [kv[0] for kv in kvs], [kv[1] for kv in kvs]
    gates_p, gates_s = proj("in_gates", n=IN_SIZES[6] + IN_SIZES[7], w_rows=_in_rows(6, stride=WIDE_TILE), **wide_in)
    u_p, xbc_tail = matmul(h_p, w["in_t"], tm=tm, tn=IN_TILE, n=IN_SIZES[1], w_rows=_in_rows(1), name="in_xbc_conv",
                           epilogue="conv_silu", conv=(w["conv_w"], w["conv_b"]), seq=sp)
    xbc_s = matmul(h_s, w["in_t"], tm=tm, tn=IN_TILE, n=IN_SIZES[1], w_rows=_in_rows(1), name="in_xbc")
    conv_p = xbc_tail[:, SUBLANES - (CONV_WIDTH - 1):]
    xpad = jnp.concatenate([conv_state.reshape(bs, (CONV_WIDTH - 1) * CONV_DIM),
                            xbc_s.reshape(bs, ss * CONV_DIM)], axis=1)
    u_s = conv_sample(xpad, w["conv_w"], w["conv_b"], steps=ss).reshape(bs * ss, CONV_DIM)
    conv_s = xpad.reshape(bs, CONV_WIDTH - 1 + ss, CONV_DIM)[:, ss:]

    y_p, y_s, ssm_p, ssm_s = _ssd_branches(u_p, u_s, dt_p, dt_s, z_p, z_s, w, conv_state, ssm_state,
                                           bp=bp, sp=sp, bs=bs, ss=ss)
    o_p, o_s = _attention_branches(q_p, q_s, kvs_p, kvs_s, caches, rel_bias, bp=bp, sp=sp, bs=bs, ss=ss)

    merged_p = merge_proj(y_p, o_p, gates_p, w["ssm_proj"], w["att_proj"], tm=tm, tn=512)
    merged_s = merge_proj(y_s, o_s, gates_s, w["ssm_proj"], w["att_proj"], tm=tm, tn=512)
    out_proj = lambda a, res: matmul(a, w["out"], tm=tm, tn=1024, epilogue="residual", residual=res, name="out_proj")
    x1_p, x1_s = out_proj(merged_p, x_p), out_proj(merged_s, x_s)
    h2_p = rmsnorm(x1_p, w["norm_mlp"], BF16)
    h2_s = rmsnorm(x1_s, w["norm_mlp"], BF16)
    mlp_up = lambda a: matmul(a, w["up"], tm=tm, out_dtype=BF16, epilogue="relu2", name="mlp_up", **wide)
    up_p, up_s = mlp_up(h2_p), mlp_up(h2_s)
    x2_p = matmul(up_p, w["down"], tm=512, tn=512, epilogue="residual", residual=x1_p, name="mlp_down")
    x2_s = matmul(up_s, w["down"], tm=512, tn=512, epilogue="residual", residual=x1_s, name="mlp_down")
    y_out_p = rmsnorm(x2_p, w["norm_final"], F32)
    y_out_s = rmsnorm(x2_s, w["norm_final"], F32)

    rows_p = [kv.reshape(bp, sp, 2, nh, hd)[:, sp - min(win, sp):] for kv, (win, _) in zip(kvs_p, ATT_PATTERNS)]
    rows_s = [kv.reshape(bs, ss, 2, nh, hd) for kv in kvs_s]
    return (y_out_p, rows_p, conv_p, ssm_p), (y_out_s, rows_s, conv_s, ssm_s)


def kernel(x_prompt, x_sample, cache_win128, cache_win512, cache_win2048, state_conv, state_ssm, w_in, conv_w, conv_b, dt_bias, a_log, d_skip, ssm_norm, w_ssm_proj, w_att_proj, w_out, norm_mix, w_up, w_down, norm_mlp, rel_bias, norm_final):
    assert w_in.shape[0] == 1, "single-layer model"
    bp, sp, _ = x_prompt.shape
    bs, ss, _ = x_sample.shape
    w = dict(
        in_t=jnp.swapaxes(w_in[0], 0, 1),
        conv_w=conv_w[0], conv_b=conv_b[0], dt_bias=dt_bias[0], a_log=a_log[0], d_skip=d_skip[0],
        ssm_norm=ssm_norm[0], ssm_proj=w_ssm_proj[0].astype(BF16), att_proj=w_att_proj[0].astype(BF16),
        out=w_out[0], norm_mix=norm_mix[0], up=w_up[0],
        down=w_down[0].astype(BF16), norm_mlp=norm_mlp[0], norm_final=norm_final)

    (yp, rows_p, conv_p, ssm_p), (ys, rows_s, conv_s, ssm_s) = _forward(
        x_prompt.reshape(bp * sp, D_MODEL), x_sample.reshape(bs * ss, D_MODEL), w,
        (cache_win128[0], cache_win512[0], cache_win2048[0]), state_conv[0], state_ssm[0], rel_bias,
        bp=bp, sp=sp, bs=bs, ss=ss)

    return (yp.reshape(bp, sp, D_MODEL), ys.reshape(bs, ss, D_MODEL),
            rows_p[0][None], rows_p[1][None], rows_p[2][None], conv_p[None], ssm_p[None],
            rows_s[0][None], rows_s[1][None], rows_s[2][None], conv_s[None], ssm_s[None])
```

```python
import functools
import math

import numpy as np
import jax
import jax.numpy as jnp
from jax import lax
from jax.experimental import pallas as pl
from jax.experimental.pallas import tpu as pltpu

D_MODEL = 2048
D_INNER = 2 * D_MODEL
SSM_HEAD_DIM = 64
SSM_HEADS = D_INNER // SSM_HEAD_DIM
SSM_GROUPS = 8
HEADS_PER_SSM_GROUP = SSM_HEADS // SSM_GROUPS
SSM_STATE = 128
SSM_GROUP_WIDTH = D_INNER // SSM_GROUPS
CONV_WIDTH = 4
CONV_DIM = D_INNER + 2 * SSM_GROUPS * SSM_STATE
SSD_CHUNK = 128
ATT_HEAD_DIM = 128
ATT_HEADS_PER_GROUP = 8
ATT_PATTERNS = ((128, 1), (512, 4), (2048, 16))
ATT_N_GROUPS = len(ATT_PATTERNS)
ATT_GROUP_WIDTH = ATT_HEADS_PER_GROUP * ATT_HEAD_DIM
ATT_WIDTH = ATT_N_GROUPS * ATT_GROUP_WIDTH
N_BUCKETS = 32
MAX_DISTANCE = max(w for w, _ in ATT_PATTERNS)
D_FF = 4 * D_MODEL
EPS = 1e-6

LANES = 128
SUBLANES = 8
VMEM_LIMIT_BYTES = 56 * 1024 * 1024

F32 = jnp.float32
BF16 = jnp.bfloat16
NEG_INF = float("-inf")


def _params(n_grid_dims):
    return pltpu.CompilerParams(
        dimension_semantics=("arbitrary",) * n_grid_dims,
        vmem_limit_bytes=VMEM_LIMIT_BYTES)


def _sigmoid(x):
    return 0.5 * jnp.tanh(0.5 * x) + 0.5


def _silu(x):
    return x * _sigmoid(x)


def _softplus(x):
    return jnp.maximum(x, 0.0) + jnp.log1p(jnp.exp(-jnp.abs(x)))


def _dot_nt(a, b):
    return lax.dot_general(a, b, (((1,), (1,)), ((), ())), preferred_element_type=F32)


def _dot_tn(a, b):
    return lax.dot_general(a, b, (((0,), (0,)), ((), ())), preferred_element_type=F32)


def _load_rows(ref, offset, rows, stride):
    flat = ref.reshape(math.prod(ref.shape[:-1]), ref.shape[-1])
    return flat[pl.ds(offset, rows, stride=stride), :]


def _store_rows(ref, offset, rows, stride, value):
    flat = ref.reshape(math.prod(ref.shape[:-1]), ref.shape[-1])
    flat[pl.ds(offset, rows, stride=stride), :] = value


def _load_head(ref, h):
    return _load_rows(ref, h, ref.shape[0], ref.shape[1])


def _store_head(ref, h, value):
    _store_rows(ref, h, ref.shape[0], ref.shape[1], value)


def _rmsnorm_kernel(x_ref, g_ref, o_ref):
    x = x_ref[...]
    ms = jnp.mean(x * x, axis=-1, keepdims=True)
    o_ref[...] = (x * lax.rsqrt(ms + EPS) * g_ref[...]).astype(o_ref.dtype)


def rmsnorm(x, g, out_dtype, tm=512):
    m, d = x.shape
    tm = min(tm, m)
    return pl.pallas_call(
        _rmsnorm_kernel,
        grid=(m // tm,),
        in_specs=[pl.BlockSpec((tm, d), lambda i: (i, 0)),
                  pl.BlockSpec((1, d), lambda i: (0, 0))],
        out_specs=pl.BlockSpec((tm, d), lambda i: (i, 0)),
        out_shape=jax.ShapeDtypeStruct((m, d), out_dtype),
        compiler_params=_params(1),
        name="rmsnorm",
    )(x, g.reshape(1, d))


def _bf16_weights(w_ref, wbf_ref):
    if wbf_ref is None:
        return w_ref

    @pl.when(pl.program_id(1) == 0)
    def _():
        wbf_ref[...] = w_ref[...].astype(BF16)

    return wbf_ref


CONV_COL_CHUNK = 256


def _tile_dot(a, w_ref, cols, w_transposed):
    if w_transposed:
        return _dot_nt(a, w_ref[cols, :])
    return jnp.dot(a, w_ref[:, cols], preferred_element_type=F32)


def _conv_silu_tiles(a_ref, w_ref, cw_ref, cb_ref, o_ref, tail_ref, pad_ref, *, seq_tiles, w_transposed):
    tm = a_ref.shape[0]
    first = SUBLANES - (CONV_WIDTH - 1)

    @pl.when(pl.program_id(1) % seq_tiles == 0)
    def _():
        pad_ref[0:SUBLANES, :] = jnp.zeros((SUBLANES, pad_ref.shape[1]), F32)

    a = a_ref[...]
    for c0 in range(0, o_ref.shape[1], CONV_COL_CHUNK):
        cols = slice(c0, c0 + CONV_COL_CHUNK)
        acc = _tile_dot(a, w_ref, cols, w_transposed)
        pad_ref[SUBLANES:SUBLANES + tm, cols] = acc
        conv = cb_ref[:, cols] + pad_ref[first:first + tm, cols] * cw_ref[0:1, cols]
        for i in range(1, CONV_WIDTH):
            conv = conv + pad_ref[first + i:first + i + tm, cols] * cw_ref[i:i + 1, cols]
        o_ref[:, cols] = _silu(conv)
        last_rows = acc[tm - SUBLANES:, :]
        pad_ref[0:SUBLANES, cols] = last_rows
        tail_ref[:, cols] = last_rows


def _write_epilogue(acc, epilogue, r_ref, o_ref):
    if epilogue == "residual":
        o_ref[...] = r_ref[...] + acc
    elif epilogue == "relu2":
        u = jnp.maximum(acc, 0.0)
        o_ref[...] = (u * u).astype(o_ref.dtype)
    elif epilogue == "heads":
        for j in range(o_ref.shape[1]):
            _store_head(o_ref, j, acc[:, j * LANES:(j + 1) * LANES])
    else:
        o_ref[...] = acc.astype(o_ref.dtype)


def _mm_kernel(*refs, epilogue, cast_w, seq_tiles, w_transposed, row_tiles, second):
    refs = list(refs)
    wbf_ref = pad_ref = None
    if epilogue == "conv_silu":
        pad_ref = refs.pop()
    if cast_w:
        wbf_ref = refs.pop()
    it = iter(refs)
    a_ref, w_ref = next(it), next(it)
    r_ref = next(it) if epilogue == "residual" else None
    cw_ref, cb_ref = (next(it), next(it)) if epilogue == "conv_silu" else (None, None)
    a2_ref = next(it) if second else None
    r2_ref = next(it) if second and epilogue == "residual" else None
    o_ref = next(it)
    tail_ref = next(it) if epilogue == "conv_silu" else None
    o2_ref = next(it) if second else None
    w_ref = _bf16_weights(w_ref, wbf_ref)

    def first_part():
        if epilogue == "conv_silu":
            _conv_silu_tiles(a_ref, w_ref, cw_ref, cb_ref, o_ref, tail_ref, pad_ref,
                             seq_tiles=seq_tiles, w_transposed=w_transposed)
        else:
            _write_epilogue(_tile_dot(a_ref[...], w_ref, slice(None), w_transposed), epilogue, r_ref, o_ref)

    def second_part():
        _write_epilogue(_tile_dot(a2_ref[...], w_ref, slice(None), w_transposed),
                        "none" if epilogue == "conv_silu" else epilogue, r2_ref, o2_ref)

    if second:
        pl.when(pl.program_id(1) < row_tiles)(first_part)
        pl.when(pl.program_id(1) == row_tiles)(second_part)
    else:
        first_part()


W_ROW_ALIGN = 64


def matmul(a, w, *, tm, tn, n=None, w_rows=None, out_dtype=F32, epilogue="none", residual=None,
           conv=None, seq=None, a2=None, residual2=None, name="matmul"):
    m, k = a.shape
    n = w.shape[1] if n is None else n
    tm = min(tm, m)
    tn = min(tn, n)
    assert m % tm == 0 and n % tn == 0
    row_tiles = m // tm
    second = a2 is not None
    cast_w = w.dtype != BF16
    row = (lambda i: jnp.minimum(i, row_tiles - 1)) if second else (lambda i: i)
    if w_rows is None:
        w_spec = pl.BlockSpec((k, tn), lambda j, i: (0, j))
        w_tile = (k, tn)
    else:
        w_spec = pl.BlockSpec((pl.Element(tn), pl.Element(k)),
                              lambda j, i: (pl.multiple_of(w_rows(j), W_ROW_ALIGN), 0))
        w_tile = (tn, k)
    in_specs = [pl.BlockSpec((tm, k), lambda j, i: (row(i), 0)), w_spec]
    args = [a, w]
    scratch = [pltpu.VMEM(w_tile, BF16)] if cast_w else []
    heads = epilogue == "heads"

    def out_for(rows, tile_rows, row_index):
        if heads:
            return (pl.BlockSpec((tile_rows, tn // LANES, LANES), lambda j, i: (row_index(i), j, 0)),
                    jax.ShapeDtypeStruct((rows, n // LANES, LANES), out_dtype))
        return (pl.BlockSpec((tile_rows, tn), lambda j, i: (row_index(i), j)),
                jax.ShapeDtypeStruct((rows, n), out_dtype))

    spec, shape = out_for(m, tm, row)
    out_specs, out_shapes = [spec], [shape]
    seq_tiles = None
    if epilogue == "residual":
        in_specs.append(pl.BlockSpec((tm, tn), lambda j, i: (row(i), j)))
        args.append(residual)
    elif epilogue == "conv_silu":
        assert seq % tm == 0
        seq_tiles = seq // tm
        conv_w, conv_b = conv
        in_specs += [pl.BlockSpec((CONV_WIDTH, tn), lambda j, i: (0, j)),
                     pl.BlockSpec((1, tn), lambda j, i: (0, j))]
        args += [conv_w, conv_b.reshape(1, n)]
        out_specs.append(pl.BlockSpec((None, SUBLANES, tn), lambda j, i: (row(i) // seq_tiles, 0, j)))
        out_shapes.append(jax.ShapeDtypeStruct((m // seq, SUBLANES, n), F32))
        scratch.append(pltpu.VMEM((tm + SUBLANES, tn), F32))
    if second:
        m2 = a2.shape[0]
        in_specs.append(pl.BlockSpec((m2, k), lambda j, i: (0, 0)))
        args.append(a2)
        if epilogue == "residual":
            in_specs.append(pl.BlockSpec((m2, tn), lambda j, i: (0, j)))
            args.append(residual2)
        spec, shape = out_for(m2, m2, lambda i: 0)
        out_specs.append(spec)
        out_shapes.append(shape)
    outs = pl.pallas_call(
        functools.partial(_mm_kernel, epilogue=epilogue, cast_w=cast_w, seq_tiles=seq_tiles,
                          w_transposed=w_rows is not None, row_tiles=row_tiles, second=second),
        grid=(n // tn, row_tiles + (1 if second else 0)),
        in_specs=in_specs,
        out_specs=out_specs,
        out_shape=out_shapes,
        scratch_shapes=scratch,
        compiler_params=_params(2),
        name=name,
    )(*args)
    return outs[0] if len(outs) == 1 else tuple(outs)


def _merge_kernel(*refs, cast_w, row_tiles, second):
    refs = list(refs)
    wsbf_ref = wabf_ref = None
    if cast_w:
        wabf_ref = refs.pop()
        wsbf_ref = refs.pop()
    it = iter(refs)
    first_in = [next(it) for _ in range(4)]
    ws_ref, wa_ref = next(it), next(it)
    second_in = [next(it) for _ in range(4)] if second else None
    out_ref = next(it)
    out2_ref = next(it) if second else None
    ws_ref = _bf16_weights(ws_ref, wsbf_ref)
    wa_ref = _bf16_weights(wa_ref, wabf_ref)

    def part(y_ref, o_ref, gs_ref, ga_ref, dst_ref):
        ps = jnp.dot(y_ref[...], ws_ref[...], preferred_element_type=F32)
        pa = jnp.dot(o_ref[...], wa_ref[...], preferred_element_type=F32)
        dst_ref[...] = (_sigmoid(gs_ref[...]) * ps + _sigmoid(ga_ref[...]) * pa).astype(dst_ref.dtype)

    if second:
        pl.when(pl.program_id(1) < row_tiles)(lambda: part(*first_in, out_ref))
        pl.when(pl.program_id(1) == row_tiles)(lambda: part(*second_in, out2_ref))
    else:
        part(*first_in, out_ref)


def merge_proj(y_ssm, o_att, gates, w_ssm, w_att, *, tm, tn, second=None):
    m = y_ssm.shape[0]
    n = w_ssm.shape[1]
    tm = min(tm, m)
    n_col = n // tn
    row_tiles = m // tm
    row = (lambda i: jnp.minimum(i, row_tiles - 1)) if second else (lambda i: i)
    cast_w = w_ssm.dtype != BF16
    scratch = []
    if cast_w:
        scratch = [pltpu.VMEM((w_ssm.shape[0], tn), BF16), pltpu.VMEM((w_att.shape[0], tn), BF16)]

    def operand_specs(tile_rows, row_index):
        return [pl.BlockSpec((tile_rows, y_ssm.shape[1]), lambda j, i: (row_index(i), 0)),
                pl.BlockSpec((tile_rows, o_att.shape[1]), lambda j, i: (row_index(i), 0)),
                pl.BlockSpec((tile_rows, tn), lambda j, i: (row_index(i), j)),
                pl.BlockSpec((tile_rows, tn), lambda j, i: (row_index(i), j + n_col))]

    in_specs = operand_specs(tm, row) + [pl.BlockSpec((w_ssm.shape[0], tn), lambda j, i: (0, j)),
                                        pl.BlockSpec((w_att.shape[0], tn), lambda j, i: (0, j))]
    args = [y_ssm, o_att, gates, gates, w_ssm, w_att]
    out_specs = [pl.BlockSpec((tm, tn), lambda j, i: (row(i), j))]
    out_shapes = [jax.ShapeDtypeStruct((m, n), BF16)]
    if second:
        y2, o2, g2 = second
        m2 = y2.shape[0]
        in_specs += operand_specs(m2, lambda i: 0)
        args += [y2, o2, g2, g2]
        out_specs.append(pl.BlockSpec((m2, tn), lambda j, i: (0, j)))
        out_shapes.append(jax.ShapeDtypeStruct((m2, n), BF16))
    outs = pl.pallas_call(
        functools.partial(_merge_kernel, cast_w=cast_w, row_tiles=row_tiles, second=bool(second)),
        grid=(n_col, row_tiles + (1 if second else 0)),
        in_specs=in_specs,
        out_specs=out_specs,
        out_shape=out_shapes,
        scratch_shapes=scratch,
        compiler_params=_params(2),
        name="merge_proj",
    )(*args)
    return outs[0] if len(outs) == 1 else tuple(outs)


def _conv_sample_kernel(x0_ref, x1_ref, x2_ref, x3_ref, w_ref, b_ref, o_ref):
    acc = b_ref[...] + x0_ref[...] * w_ref[0:1, :]
    for i, x_ref in enumerate((x1_ref, x2_ref, x3_ref), start=1):
        acc = acc + x_ref[...] * w_ref[i:i + 1, :]
    o_ref[...] = _silu(acc)


def conv_sample(xpad, conv_w, conv_b, *, steps, tc=512):
    bsz = xpad.shape[0]
    c = conv_w.shape[1]
    nct = c // tc
    x_specs = [pl.BlockSpec((bsz, tc), functools.partial(lambda t, j, i: (0, (t + i) * nct + j), i=i))
               for i in range(CONV_WIDTH)]
    return pl.pallas_call(
        _conv_sample_kernel,
        grid=(steps, nct),
        in_specs=x_specs + [pl.BlockSpec((CONV_WIDTH, tc), lambda t, j: (0, j)),
                            pl.BlockSpec((1, tc), lambda t, j: (0, j))],
        out_specs=pl.BlockSpec((bsz, tc), lambda t, j: (0, t * nct + j)),
        out_shape=jax.ShapeDtypeStruct((bsz, steps * c), F32),
        compiler_params=_params(2),
        name="conv_sample",
    )(xpad, xpad, xpad, xpad, conv_w, conv_b.reshape(1, c))


def _split3(a):
    hi = a.astype(BF16)
    rest = a - hi.astype(F32)
    mid = rest.astype(BF16)
    return hi, mid, (rest - mid.astype(F32)).astype(BF16)


def _ssd_group(gl, x_ref, b_ref, c_ref, dtc_ref, dtr_ref, z_ref, pc_ref, pr_ref, nrm_ref, y_ref, h_ref,
               ht_ref, *, q_len, valid_len, last_chunk):
    nh, hd = HEADS_PER_SSM_GROUP, SSM_HEAD_DIM
    pair_w = 2 * hd
    gw = SSM_GROUP_WIDTH
    x_cols = slice(gl * gw, (gl + 1) * gw)
    n_cols = slice(gl * SSM_STATE, (gl + 1) * SSM_STATE)

    pc = pc_ref[gl]
    pr = pr_ref[gl]
    dt_c = _softplus(dtc_ref[gl] + pc[0:1, :])
    dt_r = _softplus(dtr_ref[gl] + pr[:, 0:1])
    if valid_len < q_len:
        dt_c = jnp.where(lax.broadcasted_iota(jnp.int32, dt_c.shape, 0) < valid_len, dt_c, 0.0)
        dt_r = jnp.where(lax.broadcasted_iota(jnp.int32, dt_r.shape, 1) < valid_len, dt_r, 0.0)
    a_c = dt_c * (-jnp.exp(pc[1:2, :]))
    a_r = dt_r * (-jnp.exp(pr[:, 1:2]))
    ti = lax.broadcasted_iota(jnp.int32, (q_len, q_len), 0)
    tj = lax.broadcasted_iota(jnp.int32, (q_len, q_len), 1)
    causal = ti >= tj
    tril = causal.astype(BF16)
    triu = (ti <= tj).astype(BF16)
    acum_c = sum(jnp.dot(tril, part, preferred_element_type=F32) for part in _split3(a_c))
    acum_r = sum(jnp.dot(part, triu, preferred_element_type=F32) for part in _split3(a_r))
    end_c = acum_c[q_len - 1:q_len, :]
    dd_c = jnp.exp(end_c - acum_c) * dt_c
    src_r = acum_r - jnp.log(dt_r)

    x = x_ref[:, x_cols]
    cmat = c_ref[:, n_cols].astype(BF16)
    if ht_ref is None:
        bmat = b_ref[:, n_cols].astype(BF16)
        cb = _dot_nt(cmat, bmat)
        h_prev = h_ref[gl * nh:(gl + 1) * nh]
        y_off_all = _dot_nt(cmat, h_prev.reshape(nh * hd, SSM_STATE).astype(BF16))
    else:
        b_t = jnp.transpose(b_ref[:, n_cols]).astype(BF16)
        cb = jnp.dot(cmat, b_t, preferred_element_type=F32)
        ht_prev = ht_ref[gl]
        y_off_all = jnp.dot(cmat, ht_prev.astype(BF16), preferred_element_type=F32)

    lane = lax.broadcasted_iota(jnp.int32, (q_len, pair_w), 1)
    first_half = lane < hd

    def pair_bcast(cols, j0):
        return jnp.where(first_half,
                         jnp.broadcast_to(cols[:, j0:j0 + 1], (q_len, pair_w)),
                         jnp.broadcast_to(cols[:, j0 + 1:j0 + 2], (q_len, pair_w)))

    yield

    y_diags = []
    for i in range(nh // 2):
        j0 = 2 * i
        x_pair = x[:, i * pair_w:(i + 1) * pair_w]
        y_diag = jnp.zeros((q_len, pair_w), F32)
        for j in (j0, j0 + 1):
            seg = jnp.broadcast_to(acum_c[:, j:j + 1], (q_len, q_len)) - src_r[j:j + 1, :]
            w = (cb * jnp.exp(jnp.where(causal, seg, NEG_INF))).astype(BF16)
            own = first_half if j == j0 else jnp.logical_not(first_half)
            y_diag = y_diag + jnp.dot(w, jnp.where(own, x_pair, 0.0).astype(BF16),
                                      preferred_element_type=F32)
        y_diags.append(y_diag)
    yield

    ys = []
    xws = []
    sumsq = jnp.zeros((q_len, 1), F32)
    for i in range(nh // 2):
        j0 = 2 * i
        x_pair = x[:, i * pair_w:(i + 1) * pair_w]
        e_pair = jnp.exp(pair_bcast(acum_c, j0))
        d_pair = jnp.where(first_half[0:1, :], pc[2:3, j0:j0 + 1], pc[2:3, j0 + 1:j0 + 2])
        y = y_diags[i] + y_off_all[:, i * pair_w:(i + 1) * pair_w] * e_pair + d_pair * x_pair
        y = y * _silu(z_ref[:, gl * gw + i * pair_w:gl * gw + (i + 1) * pair_w])
        sumsq = sumsq + jnp.sum(y * y, axis=-1, keepdims=True)
        ys.append(y)
        xws.append((x_pair * pair_bcast(dd_c, j0)).astype(BF16))

    if ht_ref is None:
        for i in range(nh // 2):
            s_pair = _dot_tn(xws[i], bmat)
            for jj, j in enumerate((2 * i, 2 * i + 1)):
                chunk_decay = jnp.exp(acum_r[j:j + 1, q_len - 1:q_len])
                h_ref[gl * nh + j] = h_prev[j] * chunk_decay + s_pair[jj * hd:(jj + 1) * hd, :]
    else:
        end_decay = jnp.exp(end_c)
        decay_row = jnp.concatenate(
            [jnp.where(first_half[0:1, :], end_decay[:, 2 * i:2 * i + 1], end_decay[:, 2 * i + 1:2 * i + 2])
             for i in range(nh // 2)], axis=1)
        ht_new = ht_prev * decay_row + jnp.dot(b_t, jnp.concatenate(xws, axis=1), preferred_element_type=F32)
        ht_ref[gl] = ht_new

        @pl.when(last_chunk)
        def _():
            for i in range(nh // 2):
                both = jnp.transpose(ht_new[:, i * pair_w:(i + 1) * pair_w])
                h_ref[gl * nh + 2 * i] = both[:hd, :]
                h_ref[gl * nh + 2 * i + 1] = both[hd:, :]

    inv = lax.rsqrt(sumsq * (1.0 / (nh * hd)) + EPS)
    for i in range(nh // 2):
        sl = slice(gl * gw + i * pair_w, gl * gw + (i + 1) * pair_w)
        y_ref[:, sl] = (ys[i] * inv * nrm_ref[:, sl]).astype(y_ref.dtype)


def _ssd_kernel(*refs, q_len, valid_len, has_h0, groups_per_step):
    if has_h0:
        h0_ref = refs[9]
        refs = refs[:9] + refs[10:] + (None,)
    h_ref, ht_ref = refs[-2:]
    chunk = pl.program_id(2)

    @pl.when(chunk == 0)
    def _():
        if has_h0:
            h_ref[...] = h0_ref[...]
        else:
            ht_ref[...] = jnp.zeros(ht_ref.shape, F32)

    groups = [_ssd_group(gl, *refs, q_len=q_len, valid_len=valid_len,
                         last_chunk=chunk == pl.num_programs(2) - 1) for gl in range(groups_per_step)]
    if has_h0:
        while groups:
            groups = [g for g in groups if next(g, StopIteration) is not StopIteration]
    else:
        for g in groups:
            for _ in g:
                pass


def ssd(u, dt_raw, z, dt_bias, a_log, d_skip, ssm_norm, h0, *, batch, seq, q_len, valid_len, groups_per_step):
    nh, g, gps = HEADS_PER_SSM_GROUP, SSM_GROUPS, groups_per_step
    nc = seq // q_len
    m = batch * seq
    dt4 = dt_raw[:, :SSM_HEADS].reshape(batch * nc, q_len, g, nh)
    dtc = jnp.transpose(dt4, (0, 2, 1, 3))
    dtr = jnp.transpose(dt4, (0, 2, 3, 1))
    pcol = jnp.stack([dt_bias, a_log, d_skip]).reshape(3, g, nh).transpose(1, 0, 2)
    prow = jnp.transpose(pcol, (0, 2, 1))
    b_off = D_INNER // (gps * SSM_STATE)
    c_off = b_off + g // gps
    row = lambda b, gi, c: b * nc + c
    in_specs = [
        pl.BlockSpec((q_len, gps * SSM_GROUP_WIDTH), lambda b, gi, c: (row(b, gi, c), gi)),
        pl.BlockSpec((q_len, gps * SSM_STATE), lambda b, gi, c: (row(b, gi, c), b_off + gi)),
        pl.BlockSpec((q_len, gps * SSM_STATE), lambda b, gi, c: (row(b, gi, c), c_off + gi)),
        pl.BlockSpec((None, gps, q_len, nh), lambda b, gi, c: (row(b, gi, c), gi, 0, 0)),
        pl.BlockSpec((None, gps, nh, q_len), lambda b, gi, c: (row(b, gi, c), gi, 0, 0)),
        pl.BlockSpec((q_len, gps * SSM_GROUP_WIDTH), lambda b, gi, c: (row(b, gi, c), gi)),
        pl.BlockSpec((gps, 3, nh), lambda b, gi, c: (gi, 0, 0)),
        pl.BlockSpec((gps, nh, 3), lambda b, gi, c: (gi, 0, 0)),
        pl.BlockSpec((1, gps * SSM_GROUP_WIDTH), lambda b, gi, c: (0, gi)),
    ]
    args = [u, u, u, dtc, dtr, z, pcol, prow, ssm_norm.reshape(1, D_INNER)]
    h_spec = pl.BlockSpec((None, gps * nh, SSM_HEAD_DIM, SSM_STATE), lambda b, gi, c: (b, gi, 0, 0))
    scratch = []
    if h0 is not None:
        in_specs.append(h_spec)
        args.append(h0)
    else:
        scratch.append(pltpu.VMEM((gps, SSM_STATE, SSM_GROUP_WIDTH), F32))
    return pl.pallas_call(
        functools.partial(_ssd_kernel, q_len=q_len, valid_len=valid_len, has_h0=h0 is not None,
                          groups_per_step=gps),
        grid=(batch, g // gps, nc),
        in_specs=in_specs,
        out_specs=[pl.BlockSpec((q_len, gps * SSM_GROUP_WIDTH), lambda b, gi, c: (row(b, gi, c), gi)), h_spec],
        out_shape=[jax.ShapeDtypeStruct((m, D_INNER), BF16),
                   jax.ShapeDtypeStruct((batch, SSM_HEADS, SSM_HEAD_DIM, SSM_STATE), F32)],
        scratch_shapes=scratch,
        compiler_params=_params(3),
        name="ssd",
    )(*args)


def _attn_prompt_kernel(*refs, kb, has_prev):
    if has_prev:
        q_ref, kp_ref, kc_ref, vp_ref, vc_ref, mb_ref, o_ref, l_ref = refs
    else:
        q_ref, kc_ref, vc_ref, mb_ref, o_ref, l_ref = refs
    not_first = pl.program_id(2) > 0
    scale = ATT_HEAD_DIM ** -0.5
    heads = range(ATT_HEADS_PER_GROUP)
    scores = []
    for h in heads:
        q = (_load_head(q_ref, h) * scale).astype(BF16)
        s_c = _dot_nt(q, _load_head(kc_ref, h).astype(BF16)) + mb_ref[h, :, kb:]
        s_p = None
        if has_prev:
            s_p = _dot_nt(q, _load_head(kp_ref, h).astype(BF16)) + mb_ref[h, :, :kb]
            s_p = jnp.where(not_first, s_p, NEG_INF)
        scores.append((s_c, s_p))
    probs = []
    for s_c, s_p in scores:
        m = jnp.max(s_c, axis=-1, keepdims=True)
        if has_prev:
            m = jnp.maximum(m, jnp.max(s_p, axis=-1, keepdims=True))
        p_c = jnp.exp(s_c - m)
        den = jnp.sum(p_c, axis=-1, keepdims=True)
        p_p = None
        if has_prev:
            p_p = jnp.exp(s_p - m)
            den = den + jnp.sum(p_p, axis=-1, keepdims=True)
            p_p = p_p.astype(BF16)
        probs.append((p_c.astype(BF16), p_p, den, m + jnp.log(den)))
    for h, (p_c, p_p, den, lse) in zip(heads, probs):
        o = jnp.dot(p_c, _load_head(vc_ref, h).astype(BF16), preferred_element_type=F32)
        if has_prev:
            o = o + jnp.dot(p_p, _load_head(vp_ref, h).astype(BF16), preferred_element_type=F32)
        _store_head(o_ref, h, o / den)
        _store_head(l_ref, h, jnp.broadcast_to(lse, (kb, ATT_HEAD_DIM)))


def attn_prompt(q, kv, mbias, *, group, batch, seq, dil, kb):
    m_res = seq // dil
    nb = m_res // kb
    has_prev = nb > 1
    nh = ATT_HEADS_PER_GROUP
    q4 = q.reshape(batch * m_res, dil, q.shape[1], ATT_HEAD_DIM)
    kv4 = kv.reshape(batch * m_res, dil, 2 * nh, ATT_HEAD_DIM)

    def cur(head_block):
        return pl.BlockSpec((kb, None, nh, ATT_HEAD_DIM), lambda b, r, n: (b * nb + n, r, head_block, 0))

    def prev(head_block):
        return pl.BlockSpec((kb, None, nh, ATT_HEAD_DIM),
                            lambda b, r, n: (b * nb + jnp.maximum(n - 1, 0), r, head_block, 0))

    if has_prev:
        in_specs = [cur(group), prev(0), cur(0), prev(1), cur(1)]
        args = [q4, kv4, kv4, kv4, kv4]
    else:
        in_specs = [cur(group), cur(0), cur(1)]
        args = [q4, kv4, kv4]
    in_specs.append(pl.BlockSpec(mbias.shape, lambda b, r, n: (0, 0, 0)))
    args.append(mbias)
    o, lse = pl.pallas_call(
        functools.partial(_attn_prompt_kernel, kb=kb, has_prev=has_prev),
        grid=(batch, dil, nb),
        in_specs=in_specs,
        out_specs=[cur(0), cur(0)],
        out_shape=[jax.ShapeDtypeStruct((batch * m_res, dil, nh, ATT_HEAD_DIM), F32)] * 2,
        compiler_params=_params(3),
        name=f"attn_prompt_g{group}",
    )(*args)
    return o.reshape(batch * seq, nh, ATT_HEAD_DIM), lse.reshape(batch * seq, nh, ATT_HEAD_DIM)


def _attn_sample_kernel(*refs, n_sets, steps, q_rows):
    q_ref, new_ref = refs[:2]
    cache_refs = refs[2:2 + n_sets]
    bias_ref, o_ref, l_ref = refs[2 + n_sets:]
    gw = ATT_GROUP_WIDTH
    nh = ATT_HEADS_PER_GROUP
    hd = ATT_HEAD_DIM
    scale = ATT_HEAD_DIM ** -0.5
    bb = q_ref.shape[0]
    cache_rows = cache_refs[0].shape[1]
    new_rows = new_ref.shape[1]
    row_stride = 2 * nh
    per_set = steps // n_sets
    sub = lax.broadcasted_iota(jnp.int32, (nh, gw), 0)
    own = sub == lax.broadcasted_iota(jnp.int32, (nh, gw), 1) // hd
    sets = [(bi, ci) for bi in range(bb) for ci in range(n_sets)]

    def keys_or_values(bi, ci, first_head):
        base = bi * cache_rows * row_stride
        return jnp.concatenate(
            [jnp.concatenate([_load_rows(cache_refs[ci], base + first_head + h, cache_rows, row_stride),
                              _load_rows(new_ref, bi * new_rows * 2 * nh + first_head + h, new_rows, 2 * nh)], axis=0)
             for h in range(nh)], axis=1).astype(BF16)

    scores = []
    for bi, ci in sets:
        q_blocks = []
        for tt in range(per_set):
            q_t = q_ref[bi, ci * per_set + tt] * scale
            q_blocks.append(jnp.where(own, jnp.concatenate([q_t] * nh, axis=1), 0.0))
        if q_rows > per_set * nh:
            q_blocks.append(jnp.zeros((q_rows - per_set * nh, gw), F32))
        qbd = jnp.concatenate(q_blocks, axis=0).astype(BF16)
        scores.append(_dot_nt(qbd, keys_or_values(bi, ci, 0)) + bias_ref[ci])
    probs = []
    for s in scores:
        m = jnp.max(s, axis=-1, keepdims=True)
        p = jnp.exp(s - m)
        den = jnp.sum(p, axis=-1, keepdims=True)
        probs.append((p.astype(BF16), den, m + jnp.log(den)))
    for (bi, ci), (p, den, lse) in zip(sets, probs):
        o_all = jnp.dot(p, keys_or_values(bi, ci, nh), preferred_element_type=F32) / den
        for tt in range(per_set):
            t = ci * per_set + tt
            rows = slice(tt * nh, (tt + 1) * nh)
            o_t = jnp.zeros((nh, hd), F32)
            for h in range(nh):
                o_t = o_t + jnp.where(sub[:, :hd] == h, o_all[rows, h * hd:(h + 1) * hd], 0.0)
            o_ref[bi, t] = o_t
            l_ref[bi, t] = jnp.broadcast_to(lse[rows, :], (nh, hd))


def attn_sample(q, kv_new_pad, cache, bias, *, group, batch, steps, dil, batch_block):
    nh, hd = ATT_HEADS_PER_GROUP, ATT_HEAD_DIM
    bb = batch_block
    n_sets, q_rows, n_keys_tot = bias.shape
    new_rows = kv_new_pad.shape[1]
    cache_rows = n_keys_tot - new_rows
    q4 = q.reshape(batch, steps, q.shape[1], hd)
    if n_sets == 1:
        cache_view = cache.reshape(batch, cache_rows, 2 * nh, hd)
        cache_specs = [pl.BlockSpec((bb, cache_rows, 2 * nh, hd), lambda b: (b, 0, 0, 0))]
    else:
        cache_view = cache.reshape(batch, cache_rows, dil, 2 * nh, hd)
        cache_specs = [pl.BlockSpec((bb, cache_rows, None, 2 * nh, hd),
                                    functools.partial(lambda b, t: (b, 0, t, 0, 0), t=t))
                       for t in range(n_sets)]
    out_spec = pl.BlockSpec((bb, steps, nh, hd), lambda b: (b, 0, 0, 0))
    o, lse = pl.pallas_call(
        functools.partial(_attn_sample_kernel, n_sets=n_sets, steps=steps, q_rows=q_rows),
        grid=(batch // bb,),
        in_specs=[pl.BlockSpec((bb, steps, nh, hd), lambda b: (b, 0, group, 0)),
                  pl.BlockSpec((bb, new_rows, 2 * nh, hd), lambda b: (b, 0, 0, 0))]
                 + cache_specs
                 + [pl.BlockSpec(bias.shape, lambda b: (0, 0, 0))],
        out_specs=[out_spec, out_spec],
        out_shape=[jax.ShapeDtypeStruct((batch, steps, nh, hd), F32)] * 2,
        compiler_params=_params(1),
        name=f"attn_sample_g{group}",
    )(q4, kv_new_pad, *([cache_view] * n_sets), bias)
    return o.reshape(batch * steps, nh, hd), lse.reshape(batch * steps, nh, hd)


def _attn_combine_kernel(o0, o1, o2, l0, l1, l2, out_ref, mix_ref):
    ls = (l0[...], l1[...], l2[...])
    m = jnp.maximum(jnp.maximum(ls[0], ls[1]), ls[2])
    ws = [jnp.exp(l - m) for l in ls]
    num = ws[0] * o0[...] + ws[1] * o1[...] + ws[2] * o2[...]
    mix_ref[...] = num / (ws[0] + ws[1] + ws[2])
    for h in range(ATT_HEADS_PER_GROUP):
        out_ref[:, h * ATT_HEAD_DIM:(h + 1) * ATT_HEAD_DIM] = _load_head(mix_ref, h).astype(out_ref.dtype)


def attn_combine(outs, lses, tm=512):
    m, nh, hd = outs[0].shape
    tm = min(tm, m)
    spec = pl.BlockSpec((tm, nh, hd), lambda i: (i, 0, 0))
    return pl.pallas_call(
        _attn_combine_kernel,
        grid=(m // tm,),
        in_specs=[spec] * 6,
        out_specs=pl.BlockSpec((tm, nh * hd), lambda i: (i, 0)),
        out_shape=jax.ShapeDtypeStruct((m, nh * hd), BF16),
        scratch_shapes=[pltpu.VMEM((tm, nh, hd), F32)],
        compiler_params=_params(1),
        name="attn_combine",
    )(*outs, *lses)


def _t5_bucket(dist):
    max_exact = N_BUCKETS // 2
    d = np.maximum(dist, max_exact).astype(np.float32)
    large = max_exact + (np.log(d / max_exact) / np.log(MAX_DISTANCE / max_exact)
                         * (N_BUCKETS - max_exact)).astype(np.int32)
    large = np.minimum(large, N_BUCKETS - 1)
    return np.where(dist < max_exact, dist, large).astype(np.int32)


def _group_bias(rel_bias, g):
    win, dil = ATT_PATTERNS[g]
    buckets = _t5_bucket(np.arange(win // dil + 1) * dil)
    return rel_bias[buckets][:, g * ATT_HEADS_PER_GROUP:(g + 1) * ATT_HEADS_PER_GROUP]


def _prompt_mask_bias(bias_g, n_keys, kb):
    period = 3 * kb
    offs = np.arange(period)
    offs = np.where(offs >= 2 * kb, offs - period, offs)
    j_of = np.clip(kb - offs, 0, n_keys)
    a = jnp.transpose(bias_g[j_of]).astype(F32)
    nh = a.shape[0]
    tiled = jnp.tile(a, (1, kb))[:, :kb * (period - 1)].reshape(nh, kb, period - 1)
    rel = np.arange(kb)[:, None] + kb - np.arange(2 * kb)[None, :]
    valid = (rel >= 0) & (rel <= n_keys)
    return jnp.where(valid[None], tiled[:, :, :2 * kb], NEG_INF)


def _sample_mask_bias(bias_g, *, cache_len, dil, n_keys, steps, new_rows, shared_cache, q_rows):
    nh = ATT_HEADS_PER_GROUP
    n_cache = 1 if shared_cache else steps
    per_cache = steps // n_cache
    rows_per_cache = cache_len if shared_cache else cache_len // dil
    blocks = []
    for ci in range(n_cache):
        if shared_cache:
            cache_pos = np.arange(rows_per_cache)
        else:
            cache_pos = ci + dil * np.arange(rows_per_cache)
        key_pos = np.concatenate([cache_pos, cache_len + np.arange(new_rows)])
        key_live = np.concatenate([np.ones(rows_per_cache, bool), np.arange(new_rows) < steps])
        rows = []
        for tt in range(per_cache):
            t = ci * per_cache + tt
            diff = cache_len + t - key_pos
            valid = key_live & (diff >= 0) & (diff % dil == 0) & (diff // dil <= n_keys)
            j = np.clip(diff // dil, 0, n_keys)
            rows.append(jnp.where(valid[None, :], bias_g[j].T.astype(F32), NEG_INF))
        blk = jnp.concatenate(rows, axis=0)
        pad = q_rows - per_cache * nh
        if pad:
            blk = jnp.concatenate([blk, jnp.zeros((pad, blk.shape[1]), F32)], axis=0)
        blocks.append(blk)
    return jnp.stack(blocks)


SAMPLE_NEW_ROWS = 16
SAMPLE_Q_ROWS_MIN = 16
SSD_SAMPLE_Q = 16
SAMPLE_SHARED_BATCH_BLOCK = 4
SAMPLE_DILATED_BATCH_BLOCK = 2
SSD_PROMPT_GROUPS_PER_STEP = 4


IN_SIZES = (D_INNER, CONV_DIM, SSM_HEADS, ATT_WIDTH, ATT_WIDTH, ATT_WIDTH, D_MODEL, D_MODEL)
IN_OFFSETS = tuple(int(o) for o in np.concatenate([[0], np.cumsum(IN_SIZES)]))


IN_TILE = 1024
assert all(o % W_ROW_ALIGN == 0 for o in IN_OFFSETS)


def _in_rows(seg, stride=IN_TILE, first=0):
    return lambda j: IN_OFFSETS[seg] + first + j * stride


PROMPT_ROW_TILE = 1024


def _ssd_branches(u_p, u_s, dt_p, dt_s, z_p, z_s, w, conv_state, ssm_state, *, bp, sp, bs, ss):
    params = (w["dt_bias"], w["a_log"], w["d_skip"], w["ssm_norm"])
    y_p, ssm_p = ssd(u_p, dt_p, z_p, *params, None, batch=bp, seq=sp, q_len=SSD_CHUNK,
                     valid_len=SSD_CHUNK, groups_per_step=SSD_PROMPT_GROUPS_PER_STEP)
    qp = SSD_SAMPLE_Q
    pad_rows = lambda a: jnp.pad(a.reshape(bs, ss, a.shape[-1]),
                                 ((0, 0), (0, qp - ss), (0, 0))).reshape(bs * qp, a.shape[-1])
    y_pad, ssm_s = ssd(pad_rows(u_s), pad_rows(dt_s), pad_rows(z_s), *params, ssm_state,
                       batch=bs, seq=qp, q_len=qp, valid_len=ss, groups_per_step=SSM_GROUPS)
    y_s = y_pad.reshape(bs, qp, D_INNER)[:, :ss].reshape(bs * ss, D_INNER)
    return y_p, y_s, ssm_p, ssm_s


def _attention_branches(q_p, q_s, kvs_p, kvs_s, caches, rel_bias, *, bp, sp, bs, ss):
    nh, hd = ATT_HEADS_PER_GROUP, ATT_HEAD_DIM
    outs_p, lses_p, outs_s, lses_s = [], [], [], []
    for g, (win, dil) in enumerate(ATT_PATTERNS):
        n_keys = win // dil
        bias_g = _group_bias(rel_bias, g)
        mb = _prompt_mask_bias(bias_g, n_keys, n_keys)
        o, lse = attn_prompt(q_p, kvs_p[g], mb, group=g, batch=bp, seq=sp, dil=dil, kb=n_keys)
        outs_p.append(o)
        lses_p.append(lse)

        cache = caches[g]
        cache_len = cache.shape[1]
        shared = dil == 1
        q_rows = max(SAMPLE_Q_ROWS_MIN, (ss if shared else 1) * nh)
        sb = _sample_mask_bias(bias_g, cache_len=cache_len, dil=dil, n_keys=n_keys, steps=ss,
                               new_rows=SAMPLE_NEW_ROWS, shared_cache=shared, q_rows=q_rows)
        kv_new = jnp.pad(kvs_s[g].reshape(bs, ss, 2 * nh, hd),
                         ((0, 0), (0, SAMPLE_NEW_ROWS - ss), (0, 0), (0, 0)))
        o, lse = attn_sample(q_s, kv_new, cache, sb, group=g, batch=bs, steps=ss, dil=dil,
                             batch_block=SAMPLE_SHARED_BATCH_BLOCK if shared else SAMPLE_DILATED_BATCH_BLOCK)
        outs_s.append(o)
        lses_s.append(lse)
    return attn_combine(outs_p, lses_p), attn_combine(outs_s, lses_s)


def _forward(x_p, x_s, w, caches, conv_state, ssm_state, rel_bias, *, bp, sp, bs, ss):
    nh, hd = ATT_HEADS_PER_GROUP, ATT_HEAD_DIM
    tm = PROMPT_ROW_TILE
    h_p = rmsnorm(x_p, w["norm_mix"], BF16)
    h_s = rmsnorm(x_s, w["norm_mix"], BF16)
    proj = lambda name, **kw: matmul(h_p, w["in_t"], a2=h_s, tm=tm, tn=IN_TILE, name=name, **kw)
    z_p, z_s = proj("in_z", n=IN_SIZES[0], w_rows=_in_rows(0))
    dt_p, dt_s = matmul(h_p, w["in_t"], a2=h_s, tm=tm, tn=LANES, n=LANES, w_rows=_in_rows(2), name="in_dt")
    q_p, q_s = proj("in_q", n=IN_SIZES[3], w_rows=_in_rows(3), epilogue="heads")
    kvs = [proj(f"in_kv{g}", n=2 * ATT_GROUP_WIDTH, epilogue="heads",
                w_rows=_in_rows(4, stride=IN_OFFSETS[5] - IN_OFFSETS[4], first=g * ATT_GROUP_WIDTH))
           for g in range(ATT_N_GROUPS)]
    kvs_p, kvs_s = [kv[0] for kv in kvs], [kv[1] for kv in kvs]
    gates_p, gates_s = proj("in_gates", n=IN_SIZES[6] + IN_SIZES[7], w_rows=_in_rows(6))
    u_p, xbc_tail = matmul(h_p, w["in_t"], tm=tm, tn=IN_TILE, n=IN_SIZES[1], w_rows=_in_rows(1), name="in_xbc_conv",
                           epilogue="conv_silu", conv=(w["conv_w"], w["conv_b"]), seq=sp)
    xbc_s = matmul(h_s, w["in_t"], tm=tm, tn=IN_TILE, n=IN_SIZES[1], w_rows=_in_rows(1), name="in_xbc")
    conv_p = xbc_tail[:, SUBLANES - (CONV_WIDTH - 1):]
    xpad = jnp.concatenate([conv_state.reshape(bs, (CONV_WIDTH - 1) * CONV_DIM),
                            xbc_s.reshape(bs, ss * CONV_DIM)], axis=1)
    u_s = conv_sample(xpad, w["conv_w"], w["conv_b"], steps=ss).reshape(bs * ss, CONV_DIM)
    conv_s = xpad.reshape(bs, CONV_WIDTH - 1 + ss, CONV_DIM)[:, ss:]

    y_p, y_s, ssm_p, ssm_s = _ssd_branches(u_p, u_s, dt_p, dt_s, z_p, z_s, w, conv_state, ssm_state,
                                           bp=bp, sp=sp, bs=bs, ss=ss)
    o_p, o_s = _attention_branches(q_p, q_s, kvs_p, kvs_s, caches, rel_bias, bp=bp, sp=sp, bs=bs, ss=ss)

    merged_p = merge_proj(y_p, o_p, gates_p, w["ssm_proj"], w["att_proj"], tm=tm, tn=512)
    merged_s = merge_proj(y_s, o_s, gates_s, w["ssm_proj"], w["att_proj"], tm=tm, tn=512)
    out_proj = lambda a, res: matmul(a, w["out"], tm=tm, tn=1024, epilogue="residual", residual=res, name="out_proj")
    x1_p, x1_s = out_proj(merged_p, x_p), out_proj(merged_s, x_s)
    h2_p = rmsnorm(x1_p, w["norm_mlp"], BF16)
    h2_s = rmsnorm(x1_s, w["norm_mlp"], BF16)
    mlp_up = lambda a: matmul(a, w["up"], tm=tm, tn=1024, out_dtype=BF16, epilogue="relu2", name="mlp_up")
    up_p, up_s = mlp_up(h2_p), mlp_up(h2_s)
    x2_p = matmul(up_p, w["down"], tm=512, tn=512, epilogue="residual", residual=x1_p, name="mlp_down")
    x2_s = matmul(up_s, w["down"], tm=512, tn=512, epilogue="residual", residual=x1_s, name="mlp_down")
    y_out_p = rmsnorm(x2_p, w["norm_final"], F32)
    y_out_s = rmsnorm(x2_s, w["norm_final"], F32)

    rows_p = [kv.reshape(bp, sp, 2, nh, hd)[:, sp - min(win, sp):] for kv, (win, _) in zip(kvs_p, ATT_PATTERNS)]
    rows_s = [kv.reshape(bs, ss, 2, nh, hd) for kv in kvs_s]
    return (y_out_p, rows_p, conv_p, ssm_p), (y_out_s, rows_s, conv_s, ssm_s)


def kernel(x_prompt, x_sample, cache_win128, cache_win512, cache_win2048, state_conv, state_ssm, w_in, conv_w, conv_b, dt_bias, a_log, d_skip, ssm_norm, w_ssm_proj, w_att_proj, w_out, norm_mix, w_up, w_down, norm_mlp, rel_bias, norm_final):
    assert w_in.shape[0] == 1, "single-layer model"
    bp, sp, _ = x_prompt.shape
    bs, ss, _ = x_sample.shape
    w = dict(
        in_t=jnp.swapaxes(w_in[0], 0, 1),
        conv_w=conv_w[0], conv_b=conv_b[0], dt_bias=dt_bias[0], a_log=a_log[0], d_skip=d_skip[0],
        ssm_norm=ssm_norm[0], ssm_proj=w_ssm_proj[0].astype(BF16), att_proj=w_att_proj[0].astype(BF16),
        out=w_out[0], norm_mix=norm_mix[0], up=w_up[0],
        down=w_down[0].astype(BF16), norm_mlp=norm_mlp[0], norm_final=norm_final)

    (yp, rows_p, conv_p, ssm_p), (ys, rows_s, conv_s, ssm_s) = _forward(
        x_prompt.reshape(bp * sp, D_MODEL), x_sample.reshape(bs * ss, D_MODEL), w,
        (cache_win128[0], cache_win512[0], cache_win2048[0]), state_conv[0], state_ssm[0], rel_bias,
        bp=bp, sp=sp, bs=bs, ss=ss)

    return (yp.reshape(bp, sp, D_MODEL), ys.reshape(bs, ss, D_MODEL),
            rows_p[0][None], rows_p[1][None], rows_p[2][None], conv_p[None], ssm_p[None],
            rows_s[0][None], rows_s[1][None], rows_s[2][None], conv_s[None], ssm_s[None])
```

```python
import functools
import math

import numpy as np
import jax
import jax.numpy as jnp
from jax import lax
from jax.experimental import pallas as pl
from jax.experimental.pallas import tpu as pltpu

D_MODEL = 2048
D_INNER = 2 * D_MODEL
SSM_HEAD_DIM = 64
SSM_HEADS = D_INNER // SSM_HEAD_DIM
SSM_GROUPS = 8
HEADS_PER_SSM_GROUP = SSM_HEADS // SSM_GROUPS
SSM_STATE = 128
SSM_GROUP_WIDTH = D_INNER // SSM_GROUPS
CONV_WIDTH = 4
CONV_DIM = D_INNER + 2 * SSM_GROUPS * SSM_STATE
SSD_CHUNK = 128
ATT_HEAD_DIM = 128
ATT_HEADS_PER_GROUP = 8
ATT_PATTERNS = ((128, 1), (512, 4), (2048, 16))
ATT_N_GROUPS = len(ATT_PATTERNS)
ATT_GROUP_WIDTH = ATT_HEADS_PER_GROUP * ATT_HEAD_DIM
ATT_WIDTH = ATT_N_GROUPS * ATT_GROUP_WIDTH
N_BUCKETS = 32
MAX_DISTANCE = max(w for w, _ in ATT_PATTERNS)
D_FF = 4 * D_MODEL
EPS = 1e-6

LANES = 128
SUBLANES = 8
VMEM_LIMIT_BYTES = 56 * 1024 * 1024

F32 = jnp.float32
BF16 = jnp.bfloat16
NEG_INF = float("-inf")


def _params(n_grid_dims):
    return pltpu.CompilerParams(
        dimension_semantics=("arbitrary",) * n_grid_dims,
        vmem_limit_bytes=VMEM_LIMIT_BYTES)


def _sigmoid(x):
    return 0.5 * jnp.tanh(0.5 * x) + 0.5


def _silu(x):
    return x * _sigmoid(x)


def _softplus(x):
    return jnp.maximum(x, 0.0) + jnp.log1p(jnp.exp(-jnp.abs(x)))


def _dot_nt(a, b):
    return lax.dot_general(a, b, (((1,), (1,)), ((), ())), preferred_element_type=F32)


def _dot_tn(a, b):
    return lax.dot_general(a, b, (((0,), (0,)), ((), ())), preferred_element_type=F32)


def _load_rows(ref, offset, rows, stride):
    flat = ref.reshape(math.prod(ref.shape[:-1]), ref.shape[-1])
    return flat[pl.ds(offset, rows, stride=stride), :]


def _store_rows(ref, offset, rows, stride, value):
    flat = ref.reshape(math.prod(ref.shape[:-1]), ref.shape[-1])
    flat[pl.ds(offset, rows, stride=stride), :] = value


def _load_head(ref, h):
    return _load_rows(ref, h, ref.shape[0], ref.shape[1])


def _store_head(ref, h, value):
    _store_rows(ref, h, ref.shape[0], ref.shape[1], value)


def _rmsnorm_kernel(x_ref, g_ref, o_ref):
    x = x_ref[...]
    ms = jnp.mean(x * x, axis=-1, keepdims=True)
    o_ref[...] = (x * lax.rsqrt(ms + EPS) * g_ref[...]).astype(o_ref.dtype)


def rmsnorm(x, g, out_dtype, tm=512):
    m, d = x.shape
    tm = min(tm, m)
    return pl.pallas_call(
        _rmsnorm_kernel,
        grid=(m // tm,),
        in_specs=[pl.BlockSpec((tm, d), lambda i: (i, 0)),
                  pl.BlockSpec((1, d), lambda i: (0, 0))],
        out_specs=pl.BlockSpec((tm, d), lambda i: (i, 0)),
        out_shape=jax.ShapeDtypeStruct((m, d), out_dtype),
        compiler_params=_params(1),
        name="rmsnorm",
    )(x, g.reshape(1, d))


def _bf16_weights(w_ref, wbf_ref):
    if wbf_ref is None:
        return w_ref

    @pl.when(pl.program_id(1) == 0)
    def _():
        wbf_ref[...] = w_ref[...].astype(BF16)

    return wbf_ref


CONV_COL_CHUNK = 256


def _tile_dot(a, w_ref, cols, w_transposed):
    if w_transposed:
        return _dot_nt(a, w_ref[cols, :])
    return jnp.dot(a, w_ref[:, cols], preferred_element_type=F32)


def _conv_silu_tiles(a_ref, w_ref, cw_ref, cb_ref, o_ref, tail_ref, pad_ref, *, seq_tiles, w_transposed):
    tm = a_ref.shape[0]
    first = SUBLANES - (CONV_WIDTH - 1)

    @pl.when(pl.program_id(1) % seq_tiles == 0)
    def _():
        pad_ref[0:SUBLANES, :] = jnp.zeros((SUBLANES, pad_ref.shape[1]), F32)

    a = a_ref[...]
    for c0 in range(0, o_ref.shape[1], CONV_COL_CHUNK):
        cols = slice(c0, c0 + CONV_COL_CHUNK)
        acc = _tile_dot(a, w_ref, cols, w_transposed)
        pad_ref[SUBLANES:SUBLANES + tm, cols] = acc
        conv = cb_ref[:, cols] + pad_ref[first:first + tm, cols] * cw_ref[0:1, cols]
        for i in range(1, CONV_WIDTH):
            conv = conv + pad_ref[first + i:first + i + tm, cols] * cw_ref[i:i + 1, cols]
        o_ref[:, cols] = _silu(conv)
        last_rows = acc[tm - SUBLANES:, :]
        pad_ref[0:SUBLANES, cols] = last_rows
        tail_ref[:, cols] = last_rows


def _write_epilogue(acc, epilogue, r_ref, o_ref):
    if epilogue == "residual":
        o_ref[...] = r_ref[...] + acc
    elif epilogue == "relu2":
        u = jnp.maximum(acc, 0.0)
        o_ref[...] = (u * u).astype(o_ref.dtype)
    elif epilogue == "heads":
        for j in range(o_ref.shape[1]):
            _store_head(o_ref, j, acc[:, j * LANES:(j + 1) * LANES])
    else:
        o_ref[...] = acc.astype(o_ref.dtype)


def _mm_kernel(*refs, epilogue, cast_w, seq_tiles, w_transposed, row_tiles, second):
    refs = list(refs)
    wbf_ref = pad_ref = None
    if epilogue == "conv_silu":
        pad_ref = refs.pop()
    if cast_w:
        wbf_ref = refs.pop()
    it = iter(refs)
    a_ref, w_ref = next(it), next(it)
    r_ref = next(it) if epilogue == "residual" else None
    cw_ref, cb_ref = (next(it), next(it)) if epilogue == "conv_silu" else (None, None)
    a2_ref = next(it) if second else None
    r2_ref = next(it) if second and epilogue == "residual" else None
    o_ref = next(it)
    tail_ref = next(it) if epilogue == "conv_silu" else None
    o2_ref = next(it) if second else None
    w_ref = _bf16_weights(w_ref, wbf_ref)

    def first_part():
        if epilogue == "conv_silu":
            _conv_silu_tiles(a_ref, w_ref, cw_ref, cb_ref, o_ref, tail_ref, pad_ref,
                             seq_tiles=seq_tiles, w_transposed=w_transposed)
        else:
            _write_epilogue(_tile_dot(a_ref[...], w_ref, slice(None), w_transposed), epilogue, r_ref, o_ref)

    def second_part():
        _write_epilogue(_tile_dot(a2_ref[...], w_ref, slice(None), w_transposed),
                        "none" if epilogue == "conv_silu" else epilogue, r2_ref, o2_ref)

    if second:
        pl.when(pl.program_id(1) < row_tiles)(first_part)
        pl.when(pl.program_id(1) == row_tiles)(second_part)
    else:
        first_part()


W_ROW_ALIGN = 64


def matmul(a, w, *, tm, tn, n=None, w_rows=None, out_dtype=F32, epilogue="none", residual=None,
           conv=None, seq=None, a2=None, residual2=None, weight_buffers=2, name="matmul"):
    m, k = a.shape
    n = w.shape[1] if n is None else n
    tm = min(tm, m)
    tn = min(tn, n)
    assert m % tm == 0 and n % tn == 0
    row_tiles = m // tm
    second = a2 is not None
    cast_w = w.dtype != BF16
    row = (lambda i: jnp.minimum(i, row_tiles - 1)) if second else (lambda i: i)
    w_mode = dict(pipeline_mode=pl.Buffered(weight_buffers)) if weight_buffers != 2 else {}
    if w_rows is None:
        w_spec = pl.BlockSpec((k, tn), lambda j, i: (0, j), **w_mode)
        w_tile = (k, tn)
    else:
        w_spec = pl.BlockSpec((pl.Element(tn), pl.Element(k)),
                              lambda j, i: (pl.multiple_of(w_rows(j), W_ROW_ALIGN), 0), **w_mode)
        w_tile = (tn, k)
    in_specs = [pl.BlockSpec((tm, k), lambda j, i: (row(i), 0)), w_spec]
    args = [a, w]
    scratch = [pltpu.VMEM(w_tile, BF16)] if cast_w else []
    heads = epilogue == "heads"

    def out_for(rows, tile_rows, row_index):
        if heads:
            return (pl.BlockSpec((tile_rows, tn // LANES, LANES), lambda j, i: (row_index(i), j, 0)),
                    jax.ShapeDtypeStruct((rows, n // LANES, LANES), out_dtype))
        return (pl.BlockSpec((tile_rows, tn), lambda j, i: (row_index(i), j)),
                jax.ShapeDtypeStruct((rows, n), out_dtype))

    spec, shape = out_for(m, tm, row)
    out_specs, out_shapes = [spec], [shape]
    seq_tiles = None
    if epilogue == "residual":
        in_specs.append(pl.BlockSpec((tm, tn), lambda j, i: (row(i), j)))
        args.append(residual)
    elif epilogue == "conv_silu":
        assert seq % tm == 0
        seq_tiles = seq // tm
        conv_w, conv_b = conv
        in_specs += [pl.BlockSpec((CONV_WIDTH, tn), lambda j, i: (0, j)),
                     pl.BlockSpec((1, tn), lambda j, i: (0, j))]
        args += [conv_w, conv_b.reshape(1, n)]
        out_specs.append(pl.BlockSpec((None, SUBLANES, tn), lambda j, i: (row(i) // seq_tiles, 0, j)))
        out_shapes.append(jax.ShapeDtypeStruct((m // seq, SUBLANES, n), F32))
        scratch.append(pltpu.VMEM((tm + SUBLANES, tn), F32))
    if second:
        m2 = a2.shape[0]
        in_specs.append(pl.BlockSpec((m2, k), lambda j, i: (0, 0)))
        args.append(a2)
        if epilogue == "residual":
            in_specs.append(pl.BlockSpec((m2, tn), lambda j, i: (0, j)))
            args.append(residual2)
        spec, shape = out_for(m2, m2, lambda i: 0)
        out_specs.append(spec)
        out_shapes.append(shape)
    outs = pl.pallas_call(
        functools.partial(_mm_kernel, epilogue=epilogue, cast_w=cast_w, seq_tiles=seq_tiles,
                          w_transposed=w_rows is not None, row_tiles=row_tiles, second=second),
        grid=(n // tn, row_tiles + (1 if second else 0)),
        in_specs=in_specs,
        out_specs=out_specs,
        out_shape=out_shapes,
        scratch_shapes=scratch,
        compiler_params=_params(2),
        name=name,
    )(*args)
    return outs[0] if len(outs) == 1 else tuple(outs)


def _merge_kernel(*refs, cast_w, row_tiles, second):
    refs = list(refs)
    wsbf_ref = wabf_ref = None
    if cast_w:
        wabf_ref = refs.pop()
        wsbf_ref = refs.pop()
    it = iter(refs)
    first_in = [next(it) for _ in range(4)]
    ws_ref, wa_ref = next(it), next(it)
    second_in = [next(it) for _ in range(4)] if second else None
    out_ref = next(it)
    out2_ref = next(it) if second else None
    ws_ref = _bf16_weights(ws_ref, wsbf_ref)
    wa_ref = _bf16_weights(wa_ref, wabf_ref)

    def part(y_ref, o_ref, gs_ref, ga_ref, dst_ref):
        ps = jnp.dot(y_ref[...], ws_ref[...], preferred_element_type=F32)
        pa = jnp.dot(o_ref[...], wa_ref[...], preferred_element_type=F32)
        dst_ref[...] = (_sigmoid(gs_ref[...].astype(F32)) * ps
                        + _sigmoid(ga_ref[...].astype(F32)) * pa).astype(dst_ref.dtype)

    if second:
        pl.when(pl.program_id(1) < row_tiles)(lambda: part(*first_in, out_ref))
        pl.when(pl.program_id(1) == row_tiles)(lambda: part(*second_in, out2_ref))
    else:
        part(*first_in, out_ref)


def merge_proj(y_ssm, o_att, gates, w_ssm, w_att, *, tm, tn, second=None):
    m = y_ssm.shape[0]
    n = w_ssm.shape[1]
    tm = min(tm, m)
    n_col = n // tn
    row_tiles = m // tm
    row = (lambda i: jnp.minimum(i, row_tiles - 1)) if second else (lambda i: i)
    cast_w = w_ssm.dtype != BF16
    scratch = []
    if cast_w:
        scratch = [pltpu.VMEM((w_ssm.shape[0], tn), BF16), pltpu.VMEM((w_att.shape[0], tn), BF16)]

    def operand_specs(tile_rows, row_index):
        return [pl.BlockSpec((tile_rows, y_ssm.shape[1]), lambda j, i: (row_index(i), 0)),
                pl.BlockSpec((tile_rows, o_att.shape[1]), lambda j, i: (row_index(i), 0)),
                pl.BlockSpec((tile_rows, tn), lambda j, i: (row_index(i), j)),
                pl.BlockSpec((tile_rows, tn), lambda j, i: (row_index(i), j + n_col))]

    in_specs = operand_specs(tm, row) + [pl.BlockSpec((w_ssm.shape[0], tn), lambda j, i: (0, j)),
                                        pl.BlockSpec((w_att.shape[0], tn), lambda j, i: (0, j))]
    args = [y_ssm, o_att, gates, gates, w_ssm, w_att]
    out_specs = [pl.BlockSpec((tm, tn), lambda j, i: (row(i), j))]
    out_shapes = [jax.ShapeDtypeStruct((m, n), BF16)]
    if second:
        y2, o2, g2 = second
        m2 = y2.shape[0]
        in_specs += operand_specs(m2, lambda i: 0)
        args += [y2, o2, g2, g2]
        out_specs.append(pl.BlockSpec((m2, tn), lambda j, i: (0, j)))
        out_shapes.append(jax.ShapeDtypeStruct((m2, n), BF16))
    outs = pl.pallas_call(
        functools.partial(_merge_kernel, cast_w=cast_w, row_tiles=row_tiles, second=bool(second)),
        grid=(n_col, row_tiles + (1 if second else 0)),
        in_specs=in_specs,
        out_specs=out_specs,
        out_shape=out_shapes,
        scratch_shapes=scratch,
        compiler_params=_params(2),
        name="merge_proj",
    )(*args)
    return outs[0] if len(outs) == 1 else tuple(outs)


def _conv_sample_kernel(x0_ref, x1_ref, x2_ref, x3_ref, w_ref, b_ref, o_ref):
    acc = b_ref[...] + x0_ref[...] * w_ref[0:1, :]
    for i, x_ref in enumerate((x1_ref, x2_ref, x3_ref), start=1):
        acc = acc + x_ref[...] * w_ref[i:i + 1, :]
    o_ref[...] = _silu(acc)


def conv_sample(xpad, conv_w, conv_b, *, steps, tc=512):
    bsz = xpad.shape[0]
    c = conv_w.shape[1]
    nct = c // tc
    x_specs = [pl.BlockSpec((bsz, tc), functools.partial(lambda t, j, i: (0, (t + i) * nct + j), i=i))
               for i in range(CONV_WIDTH)]
    return pl.pallas_call(
        _conv_sample_kernel,
        grid=(steps, nct),
        in_specs=x_specs + [pl.BlockSpec((CONV_WIDTH, tc), lambda t, j: (0, j)),
                            pl.BlockSpec((1, tc), lambda t, j: (0, j))],
        out_specs=pl.BlockSpec((bsz, tc), lambda t, j: (0, t * nct + j)),
        out_shape=jax.ShapeDtypeStruct((bsz, steps * c), F32),
        compiler_params=_params(2),
        name="conv_sample",
    )(xpad, xpad, xpad, xpad, conv_w, conv_b.reshape(1, c))


def _split3(a):
    hi = a.astype(BF16)
    rest = a - hi.astype(F32)
    mid = rest.astype(BF16)
    return hi, mid, (rest - mid.astype(F32)).astype(BF16)


def _ssd_group(gl, x_ref, b_ref, c_ref, dtc_ref, dtr_ref, z_ref, pc_ref, pr_ref, nrm_ref, y_ref, h_ref,
               ht_ref, *, q_len, valid_len, last_chunk):
    nh, hd = HEADS_PER_SSM_GROUP, SSM_HEAD_DIM
    pair_w = 2 * hd
    gw = SSM_GROUP_WIDTH
    x_cols = slice(gl * gw, (gl + 1) * gw)
    n_cols = slice(gl * SSM_STATE, (gl + 1) * SSM_STATE)

    pc = pc_ref[gl]
    pr = pr_ref[gl]
    dt_c = _softplus(dtc_ref[gl] + pc[0:1, :])
    dt_r = _softplus(dtr_ref[gl] + pr[:, 0:1])
    if valid_len < q_len:
        dt_c = jnp.where(lax.broadcasted_iota(jnp.int32, dt_c.shape, 0) < valid_len, dt_c, 0.0)
        dt_r = jnp.where(lax.broadcasted_iota(jnp.int32, dt_r.shape, 1) < valid_len, dt_r, 0.0)
    a_c = dt_c * (-jnp.exp(pc[1:2, :]))
    a_r = dt_r * (-jnp.exp(pr[:, 1:2]))
    ti = lax.broadcasted_iota(jnp.int32, (q_len, q_len), 0)
    tj = lax.broadcasted_iota(jnp.int32, (q_len, q_len), 1)
    causal = ti >= tj
    tril = causal.astype(BF16)
    triu = (ti <= tj).astype(BF16)
    acum_c = sum(jnp.dot(tril, part, preferred_element_type=F32) for part in _split3(a_c))
    acum_r = sum(jnp.dot(part, triu, preferred_element_type=F32) for part in _split3(a_r))
    end_c = acum_c[q_len - 1:q_len, :]
    dd_c = jnp.exp(end_c - acum_c) * dt_c
    src_r = acum_r - jnp.log(dt_r)

    x = x_ref[:, x_cols]
    cmat = c_ref[:, n_cols].astype(BF16)
    if ht_ref is None:
        bmat = b_ref[:, n_cols].astype(BF16)
        cb = _dot_nt(cmat, bmat)
        h_prev = h_ref[gl * nh:(gl + 1) * nh]
        y_off_all = _dot_nt(cmat, h_prev.reshape(nh * hd, SSM_STATE).astype(BF16))
    else:
        b_t = jnp.transpose(b_ref[:, n_cols]).astype(BF16)
        cb = jnp.dot(cmat, b_t, preferred_element_type=F32)
        ht_prev = ht_ref[gl]
        y_off_all = jnp.dot(cmat, ht_prev.astype(BF16), preferred_element_type=F32)

    lane = lax.broadcasted_iota(jnp.int32, (q_len, pair_w), 1)
    first_half = lane < hd

    def pair_bcast(cols, j0):
        return jnp.where(first_half,
                         jnp.broadcast_to(cols[:, j0:j0 + 1], (q_len, pair_w)),
                         jnp.broadcast_to(cols[:, j0 + 1:j0 + 2], (q_len, pair_w)))

    yield

    y_diags = []
    for i in range(nh // 2):
        j0 = 2 * i
        x_pair = x[:, i * pair_w:(i + 1) * pair_w]
        y_diag = jnp.zeros((q_len, pair_w), F32)
        for j in (j0, j0 + 1):
            seg = jnp.broadcast_to(acum_c[:, j:j + 1], (q_len, q_len)) - src_r[j:j + 1, :]
            w = (cb * jnp.exp(jnp.where(causal, seg, NEG_INF))).astype(BF16)
            own = first_half if j == j0 else jnp.logical_not(first_half)
            y_diag = y_diag + jnp.dot(w, jnp.where(own, x_pair, 0.0).astype(BF16),
                                      preferred_element_type=F32)
        y_diags.append(y_diag)
    yield

    ys = []
    xws = []
    sumsq = jnp.zeros((q_len, 1), F32)
    for i in range(nh // 2):
        j0 = 2 * i
        x_pair = x[:, i * pair_w:(i + 1) * pair_w]
        e_pair = jnp.exp(pair_bcast(acum_c, j0))
        d_pair = jnp.where(first_half[0:1, :], pc[2:3, j0:j0 + 1], pc[2:3, j0 + 1:j0 + 2])
        y = y_diags[i] + y_off_all[:, i * pair_w:(i + 1) * pair_w] * e_pair + d_pair * x_pair
        y = y * _silu(z_ref[:, gl * gw + i * pair_w:gl * gw + (i + 1) * pair_w].astype(F32))
        sumsq = sumsq + jnp.sum(y * y, axis=-1, keepdims=True)
        ys.append(y)
        xws.append((x_pair * pair_bcast(dd_c, j0)).astype(BF16))

    if ht_ref is None:
        for i in range(nh // 2):
            s_pair = _dot_tn(xws[i], bmat)
            for jj, j in enumerate((2 * i, 2 * i + 1)):
                chunk_decay = jnp.exp(acum_r[j:j + 1, q_len - 1:q_len])
                h_ref[gl * nh + j] = h_prev[j] * chunk_decay + s_pair[jj * hd:(jj + 1) * hd, :]
    else:
        end_decay = jnp.exp(end_c)
        decay_row = jnp.concatenate(
            [jnp.where(first_half[0:1, :], end_decay[:, 2 * i:2 * i + 1], end_decay[:, 2 * i + 1:2 * i + 2])
             for i in range(nh // 2)], axis=1)
        ht_new = ht_prev * decay_row + jnp.dot(b_t, jnp.concatenate(xws, axis=1), preferred_element_type=F32)
        ht_ref[gl] = ht_new

        @pl.when(last_chunk)
        def _():
            for i in range(nh // 2):
                both = jnp.transpose(ht_new[:, i * pair_w:(i + 1) * pair_w])
                h_ref[gl * nh + 2 * i] = both[:hd, :]
                h_ref[gl * nh + 2 * i + 1] = both[hd:, :]

    inv = lax.rsqrt(sumsq * (1.0 / (nh * hd)) + EPS)
    for i in range(nh // 2):
        sl = slice(gl * gw + i * pair_w, gl * gw + (i + 1) * pair_w)
        y_ref[:, sl] = (ys[i] * inv * nrm_ref[:, sl]).astype(y_ref.dtype)


def _ssd_kernel(*refs, q_len, valid_len, has_h0, groups_per_step):
    if has_h0:
        h0_ref = refs[9]
        refs = refs[:9] + refs[10:] + (None,)
    h_ref, ht_ref = refs[-2:]
    chunk = pl.program_id(2)

    @pl.when(chunk == 0)
    def _():
        if has_h0:
            h_ref[...] = h0_ref[...]
        else:
            ht_ref[...] = jnp.zeros(ht_ref.shape, F32)

    groups = [_ssd_group(gl, *refs, q_len=q_len, valid_len=valid_len,
                         last_chunk=chunk == pl.num_programs(2) - 1) for gl in range(groups_per_step)]
    if has_h0:
        while groups:
            groups = [g for g in groups if next(g, StopIteration) is not StopIteration]
    else:
        for g in groups:
            for _ in g:
                pass


def ssd(u, dt_raw, z, dt_bias, a_log, d_skip, ssm_norm, h0, *, batch, seq, q_len, valid_len, groups_per_step):
    nh, g, gps = HEADS_PER_SSM_GROUP, SSM_GROUPS, groups_per_step
    nc = seq // q_len
    m = batch * seq
    dt4 = dt_raw[:, :SSM_HEADS].reshape(batch * nc, q_len, g, nh)
    dtc = jnp.transpose(dt4, (0, 2, 1, 3))
    dtr = jnp.transpose(dt4, (0, 2, 3, 1))
    pcol = jnp.stack([dt_bias, a_log, d_skip]).reshape(3, g, nh).transpose(1, 0, 2)
    prow = jnp.transpose(pcol, (0, 2, 1))
    b_off = D_INNER // (gps * SSM_STATE)
    c_off = b_off + g // gps
    row = lambda b, gi, c: b * nc + c
    in_specs = [
        pl.BlockSpec((q_len, gps * SSM_GROUP_WIDTH), lambda b, gi, c: (row(b, gi, c), gi)),
        pl.BlockSpec((q_len, gps * SSM_STATE), lambda b, gi, c: (row(b, gi, c), b_off + gi)),
        pl.BlockSpec((q_len, gps * SSM_STATE), lambda b, gi, c: (row(b, gi, c), c_off + gi)),
        pl.BlockSpec((None, gps, q_len, nh), lambda b, gi, c: (row(b, gi, c), gi, 0, 0)),
        pl.BlockSpec((None, gps, nh, q_len), lambda b, gi, c: (row(b, gi, c), gi, 0, 0)),
        pl.BlockSpec((q_len, gps * SSM_GROUP_WIDTH), lambda b, gi, c: (row(b, gi, c), gi)),
        pl.BlockSpec((gps, 3, nh), lambda b, gi, c: (gi, 0, 0)),
        pl.BlockSpec((gps, nh, 3), lambda b, gi, c: (gi, 0, 0)),
        pl.BlockSpec((1, gps * SSM_GROUP_WIDTH), lambda b, gi, c: (0, gi)),
    ]
    args = [u, u, u, dtc, dtr, z, pcol, prow, ssm_norm.reshape(1, D_INNER)]
    h_spec = pl.BlockSpec((None, gps * nh, SSM_HEAD_DIM, SSM_STATE), lambda b, gi, c: (b, gi, 0, 0))
    scratch = []
    if h0 is not None:
        in_specs.append(h_spec)
        args.append(h0)
    else:
        scratch.append(pltpu.VMEM((gps, SSM_STATE, SSM_GROUP_WIDTH), F32))
    return pl.pallas_call(
        functools.partial(_ssd_kernel, q_len=q_len, valid_len=valid_len, has_h0=h0 is not None,
                          groups_per_step=gps),
        grid=(batch, g // gps, nc),
        in_specs=in_specs,
        out_specs=[pl.BlockSpec((q_len, gps * SSM_GROUP_WIDTH), lambda b, gi, c: (row(b, gi, c), gi)), h_spec],
        out_shape=[jax.ShapeDtypeStruct((m, D_INNER), BF16),
                   jax.ShapeDtypeStruct((batch, SSM_HEADS, SSM_HEAD_DIM, SSM_STATE), F32)],
        scratch_shapes=scratch,
        compiler_params=_params(3),
        name="ssd",
    )(*args)


def _attn_prompt_kernel(*refs, kb, has_prev):
    if has_prev:
        q_ref, kp_ref, kc_ref, vp_ref, vc_ref, mb_ref, o_ref, l_ref = refs
    else:
        q_ref, kc_ref, vc_ref, mb_ref, o_ref, l_ref = refs
    not_first = pl.program_id(2) > 0
    scale = ATT_HEAD_DIM ** -0.5
    heads = range(ATT_HEADS_PER_GROUP)
    scores = []
    for h in heads:
        q = (_load_head(q_ref, h) * scale).astype(BF16)
        s_c = _dot_nt(q, _load_head(kc_ref, h).astype(BF16)) + mb_ref[h, :, kb:]
        s_p = None
        if has_prev:
            s_p = _dot_nt(q, _load_head(kp_ref, h).astype(BF16)) + mb_ref[h, :, :kb]
            s_p = jnp.where(not_first, s_p, NEG_INF)
        scores.append((s_c, s_p))
    probs = []
    for s_c, s_p in scores:
        m = jnp.max(s_c, axis=-1, keepdims=True)
        if has_prev:
            m = jnp.maximum(m, jnp.max(s_p, axis=-1, keepdims=True))
        p_c = jnp.exp(s_c - m)
        den = jnp.sum(p_c, axis=-1, keepdims=True)
        p_p = None
        if has_prev:
            p_p = jnp.exp(s_p - m)
            den = den + jnp.sum(p_p, axis=-1, keepdims=True)
            p_p = p_p.astype(BF16)
        probs.append((p_c.astype(BF16), p_p, den, m + jnp.log(den)))
    for h, (p_c, p_p, den, lse) in zip(heads, probs):
        o = jnp.dot(p_c, _load_head(vc_ref, h).astype(BF16), preferred_element_type=F32)
        if has_prev:
            o = o + jnp.dot(p_p, _load_head(vp_ref, h).astype(BF16), preferred_element_type=F32)
        _store_head(o_ref, h, o / den)
        _store_head(l_ref, h, jnp.broadcast_to(lse, (kb, ATT_HEAD_DIM)))


def attn_prompt(q, kv, mbias, *, group, batch, seq, dil, kb):
    m_res = seq // dil
    nb = m_res // kb
    has_prev = nb > 1
    nh = ATT_HEADS_PER_GROUP
    q4 = q.reshape(batch * m_res, dil, q.shape[1], ATT_HEAD_DIM)
    kv4 = kv.reshape(batch * m_res, dil, 2 * nh, ATT_HEAD_DIM)

    def cur(head_block):
        return pl.BlockSpec((kb, None, nh, ATT_HEAD_DIM), lambda b, r, n: (b * nb + n, r, head_block, 0))

    def prev(head_block):
        return pl.BlockSpec((kb, None, nh, ATT_HEAD_DIM),
                            lambda b, r, n: (b * nb + jnp.maximum(n - 1, 0), r, head_block, 0))

    if has_prev:
        in_specs = [cur(group), prev(0), cur(0), prev(1), cur(1)]
        args = [q4, kv4, kv4, kv4, kv4]
    else:
        in_specs = [cur(group), cur(0), cur(1)]
        args = [q4, kv4, kv4]
    in_specs.append(pl.BlockSpec(mbias.shape, lambda b, r, n: (0, 0, 0)))
    args.append(mbias)
    o, lse = pl.pallas_call(
        functools.partial(_attn_prompt_kernel, kb=kb, has_prev=has_prev),
        grid=(batch, dil, nb),
        in_specs=in_specs,
        out_specs=[cur(0), cur(0)],
        out_shape=[jax.ShapeDtypeStruct((batch * m_res, dil, nh, ATT_HEAD_DIM), F32)] * 2,
        compiler_params=_params(3),
        name=f"attn_prompt_g{group}",
    )(*args)
    return o.reshape(batch * seq, nh, ATT_HEAD_DIM), lse.reshape(batch * seq, nh, ATT_HEAD_DIM)


def _attn_sample_kernel(*refs, n_sets, steps, q_rows):
    q_ref, new_ref = refs[:2]
    cache_refs = refs[2:2 + n_sets]
    bias_ref, o_ref, l_ref = refs[2 + n_sets:]
    gw = ATT_GROUP_WIDTH
    nh = ATT_HEADS_PER_GROUP
    hd = ATT_HEAD_DIM
    scale = ATT_HEAD_DIM ** -0.5
    bb = q_ref.shape[0]
    cache_rows = cache_refs[0].shape[1]
    new_rows = new_ref.shape[1]
    row_stride = 2 * nh
    per_set = steps // n_sets
    sub = lax.broadcasted_iota(jnp.int32, (nh, gw), 0)
    own = sub == lax.broadcasted_iota(jnp.int32, (nh, gw), 1) // hd
    sets = [(bi, ci) for bi in range(bb) for ci in range(n_sets)]

    def keys_or_values(bi, ci, first_head):
        base = bi * cache_rows * row_stride
        return jnp.concatenate(
            [jnp.concatenate([_load_rows(cache_refs[ci], base + first_head + h, cache_rows, row_stride),
                              _load_rows(new_ref, bi * new_rows * 2 * nh + first_head + h, new_rows, 2 * nh)], axis=0)
             for h in range(nh)], axis=1).astype(BF16)

    scores = []
    for bi, ci in sets:
        q_blocks = []
        for tt in range(per_set):
            q_t = q_ref[bi, ci * per_set + tt] * scale
            q_blocks.append(jnp.where(own, jnp.concatenate([q_t] * nh, axis=1), 0.0))
        if q_rows > per_set * nh:
            q_blocks.append(jnp.zeros((q_rows - per_set * nh, gw), F32))
        qbd = jnp.concatenate(q_blocks, axis=0).astype(BF16)
        scores.append(_dot_nt(qbd, keys_or_values(bi, ci, 0)) + bias_ref[ci])
    probs = []
    for s in scores:
        m = jnp.max(s, axis=-1, keepdims=True)
        p = jnp.exp(s - m)
        den = jnp.sum(p, axis=-1, keepdims=True)
        probs.append((p.astype(BF16), den, m + jnp.log(den)))
    for (bi, ci), (p, den, lse) in zip(sets, probs):
        o_all = jnp.dot(p, keys_or_values(bi, ci, nh), preferred_element_type=F32) / den
        for tt in range(per_set):
            t = ci * per_set + tt
            rows = slice(tt * nh, (tt + 1) * nh)
            o_t = jnp.zeros((nh, hd), F32)
            for h in range(nh):
                o_t = o_t + jnp.where(sub[:, :hd] == h, o_all[rows, h * hd:(h + 1) * hd], 0.0)
            o_ref[bi, t] = o_t
            l_ref[bi, t] = jnp.broadcast_to(lse[rows, :], (nh, hd))


def attn_sample(q, kv_new_pad, cache, bias, *, group, batch, steps, dil, batch_block):
    nh, hd = ATT_HEADS_PER_GROUP, ATT_HEAD_DIM
    bb = batch_block
    n_sets, q_rows, n_keys_tot = bias.shape
    new_rows = kv_new_pad.shape[1]
    cache_rows = n_keys_tot - new_rows
    q4 = q.reshape(batch, steps, q.shape[1], hd)
    if n_sets == 1:
        cache_view = cache.reshape(batch, cache_rows, 2 * nh, hd)
        cache_specs = [pl.BlockSpec((bb, cache_rows, 2 * nh, hd), lambda b: (b, 0, 0, 0))]
    else:
        cache_view = cache.reshape(batch, cache_rows, dil, 2 * nh, hd)
        cache_specs = [pl.BlockSpec((bb, cache_rows, None, 2 * nh, hd),
                                    functools.partial(lambda b, t: (b, 0, t, 0, 0), t=t))
                       for t in range(n_sets)]
    out_spec = pl.BlockSpec((bb, steps, nh, hd), lambda b: (b, 0, 0, 0))
    o, lse = pl.pallas_call(
        functools.partial(_attn_sample_kernel, n_sets=n_sets, steps=steps, q_rows=q_rows),
        grid=(batch // bb,),
        in_specs=[pl.BlockSpec((bb, steps, nh, hd), lambda b: (b, 0, group, 0)),
                  pl.BlockSpec((bb, new_rows, 2 * nh, hd), lambda b: (b, 0, 0, 0))]
                 + cache_specs
                 + [pl.BlockSpec(bias.shape, lambda b: (0, 0, 0))],
        out_specs=[out_spec, out_spec],
        out_shape=[jax.ShapeDtypeStruct((batch, steps, nh, hd), F32)] * 2,
        compiler_params=_params(1),
        name=f"attn_sample_g{group}",
    )(q4, kv_new_pad, *([cache_view] * n_sets), bias)
    return o.reshape(batch * steps, nh, hd), lse.reshape(batch * steps, nh, hd)


def _attn_combine_kernel(o0, o1, o2, l0, l1, l2, out_ref, mix_ref):
    ls = (l0[...], l1[...], l2[...])
    m = jnp.maximum(jnp.maximum(ls[0], ls[1]), ls[2])
    ws = [jnp.exp(l - m) for l in ls]
    num = ws[0] * o0[...] + ws[1] * o1[...] + ws[2] * o2[...]
    mix_ref[...] = num / (ws[0] + ws[1] + ws[2])
    for h in range(ATT_HEADS_PER_GROUP):
        out_ref[:, h * ATT_HEAD_DIM:(h + 1) * ATT_HEAD_DIM] = _load_head(mix_ref, h).astype(out_ref.dtype)


def attn_combine(outs, lses, tm=512):
    m, nh, hd = outs[0].shape
    tm = min(tm, m)
    spec = pl.BlockSpec((tm, nh, hd), lambda i: (i, 0, 0))
    return pl.pallas_call(
        _attn_combine_kernel,
        grid=(m // tm,),
        in_specs=[spec] * 6,
        out_specs=pl.BlockSpec((tm, nh * hd), lambda i: (i, 0)),
        out_shape=jax.ShapeDtypeStruct((m, nh * hd), BF16),
        scratch_shapes=[pltpu.VMEM((tm, nh, hd), F32)],
        compiler_params=_params(1),
        name="attn_combine",
    )(*outs, *lses)


def _t5_bucket(dist):
    max_exact = N_BUCKETS // 2
    d = np.maximum(dist, max_exact).astype(np.float32)
    large = max_exact + (np.log(d / max_exact) / np.log(MAX_DISTANCE / max_exact)
                         * (N_BUCKETS - max_exact)).astype(np.int32)
    large = np.minimum(large, N_BUCKETS - 1)
    return np.where(dist < max_exact, dist, large).astype(np.int32)


def _group_bias(rel_bias, g):
    win, dil = ATT_PATTERNS[g]
    buckets = _t5_bucket(np.arange(win // dil + 1) * dil)
    return rel_bias[buckets][:, g * ATT_HEADS_PER_GROUP:(g + 1) * ATT_HEADS_PER_GROUP]


def _prompt_mask_bias(bias_g, n_keys, kb):
    period = 3 * kb
    offs = np.arange(period)
    offs = np.where(offs >= 2 * kb, offs - period, offs)
    j_of = np.clip(kb - offs, 0, n_keys)
    a = jnp.transpose(bias_g[j_of]).astype(F32)
    nh = a.shape[0]
    tiled = jnp.tile(a, (1, kb))[:, :kb * (period - 1)].reshape(nh, kb, period - 1)
    rel = np.arange(kb)[:, None] + kb - np.arange(2 * kb)[None, :]
    valid = (rel >= 0) & (rel <= n_keys)
    return jnp.where(valid[None], tiled[:, :, :2 * kb], NEG_INF)


def _sample_mask_bias(bias_g, *, cache_len, dil, n_keys, steps, new_rows, shared_cache, q_rows):
    nh = ATT_HEADS_PER_GROUP
    n_cache = 1 if shared_cache else steps
    per_cache = steps // n_cache
    rows_per_cache = cache_len if shared_cache else cache_len // dil
    blocks = []
    for ci in range(n_cache):
        if shared_cache:
            cache_pos = np.arange(rows_per_cache)
        else:
            cache_pos = ci + dil * np.arange(rows_per_cache)
        key_pos = np.concatenate([cache_pos, cache_len + np.arange(new_rows)])
        key_live = np.concatenate([np.ones(rows_per_cache, bool), np.arange(new_rows) < steps])
        rows = []
        for tt in range(per_cache):
            t = ci * per_cache + tt
            diff = cache_len + t - key_pos
            valid = key_live & (diff >= 0) & (diff % dil == 0) & (diff // dil <= n_keys)
            j = np.clip(diff // dil, 0, n_keys)
            rows.append(jnp.where(valid[None, :], bias_g[j].T.astype(F32), NEG_INF))
        blk = jnp.concatenate(rows, axis=0)
        pad = q_rows - per_cache * nh
        if pad:
            blk = jnp.concatenate([blk, jnp.zeros((pad, blk.shape[1]), F32)], axis=0)
        blocks.append(blk)
    return jnp.stack(blocks)


SAMPLE_NEW_ROWS = 16
SAMPLE_Q_ROWS_MIN = 16
SSD_SAMPLE_Q = 16
SAMPLE_SHARED_BATCH_BLOCK = 4
SAMPLE_DILATED_BATCH_BLOCK = 4
SSD_PROMPT_GROUPS_PER_STEP = 4


IN_SIZES = (D_INNER, CONV_DIM, SSM_HEADS, ATT_WIDTH, ATT_WIDTH, ATT_WIDTH, D_MODEL, D_MODEL)
IN_OFFSETS = tuple(int(o) for o in np.concatenate([[0], np.cumsum(IN_SIZES)]))


IN_TILE = 1024
assert all(o % W_ROW_ALIGN == 0 for o in IN_OFFSETS)


def _in_rows(seg, stride=IN_TILE, first=0):
    return lambda j: IN_OFFSETS[seg] + first + j * stride


PROMPT_ROW_TILE = 1024


def _ssd_branches(u_p, u_s, dt_p, dt_s, z_p, z_s, w, conv_state, ssm_state, *, bp, sp, bs, ss):
    params = (w["dt_bias"], w["a_log"], w["d_skip"], w["ssm_norm"])
    y_p, ssm_p = ssd(u_p, dt_p, z_p, *params, None, batch=bp, seq=sp, q_len=SSD_CHUNK,
                     valid_len=SSD_CHUNK, groups_per_step=SSD_PROMPT_GROUPS_PER_STEP)
    qp = SSD_SAMPLE_Q
    pad_rows = lambda a: jnp.pad(a.reshape(bs, ss, a.shape[-1]),
                                 ((0, 0), (0, qp - ss), (0, 0))).reshape(bs * qp, a.shape[-1])
    y_pad, ssm_s = ssd(pad_rows(u_s), pad_rows(dt_s), pad_rows(z_s), *params, ssm_state,
                       batch=bs, seq=qp, q_len=qp, valid_len=ss, groups_per_step=SSM_GROUPS)
    y_s = y_pad.reshape(bs, qp, D_INNER)[:, :ss].reshape(bs * ss, D_INNER)
    return y_p, y_s, ssm_p, ssm_s


def _attention_branches(q_p, q_s, kvs_p, kvs_s, caches, rel_bias, *, bp, sp, bs, ss):
    nh, hd = ATT_HEADS_PER_GROUP, ATT_HEAD_DIM
    outs_p, lses_p, outs_s, lses_s = [], [], [], []
    for g, (win, dil) in enumerate(ATT_PATTERNS):
        n_keys = win // dil
        bias_g = _group_bias(rel_bias, g)
        mb = _prompt_mask_bias(bias_g, n_keys, n_keys)
        o, lse = attn_prompt(q_p, kvs_p[g], mb, group=g, batch=bp, seq=sp, dil=dil, kb=n_keys)
        outs_p.append(o)
        lses_p.append(lse)

        cache = caches[g]
        cache_len = cache.shape[1]
        shared = dil == 1
        q_rows = max(SAMPLE_Q_ROWS_MIN, (ss if shared else 1) * nh)
        sb = _sample_mask_bias(bias_g, cache_len=cache_len, dil=dil, n_keys=n_keys, steps=ss,
                               new_rows=SAMPLE_NEW_ROWS, shared_cache=shared, q_rows=q_rows)
        kv_new = jnp.pad(kvs_s[g].reshape(bs, ss, 2 * nh, hd),
                         ((0, 0), (0, SAMPLE_NEW_ROWS - ss), (0, 0), (0, 0)))
        o, lse = attn_sample(q_s, kv_new, cache, sb, group=g, batch=bs, steps=ss, dil=dil,
                             batch_block=SAMPLE_SHARED_BATCH_BLOCK if shared else SAMPLE_DILATED_BATCH_BLOCK)
        outs_s.append(o)
        lses_s.append(lse)
    return attn_combine(outs_p, lses_p), attn_combine(outs_s, lses_s)


def _forward(x_p, x_s, w, caches, conv_state, ssm_state, rel_bias, *, bp, sp, bs, ss):
    nh, hd = ATT_HEADS_PER_GROUP, ATT_HEAD_DIM
    tm = PROMPT_ROW_TILE
    h_p = rmsnorm(x_p, w["norm_mix"], BF16)
    h_s = rmsnorm(x_s, w["norm_mix"], BF16)
    proj = lambda name, **kw: matmul(h_p, w["in_t"], a2=h_s, tm=tm, tn=IN_TILE, name=name, **kw)
    z_p, z_s = proj("in_z", n=IN_SIZES[0], w_rows=_in_rows(0), out_dtype=BF16)
    dt_p, dt_s = matmul(h_p, w["in_t"], a2=h_s, tm=tm, tn=LANES, n=LANES, w_rows=_in_rows(2), name="in_dt")
    q_p, q_s = proj("in_q", n=IN_SIZES[3], w_rows=_in_rows(3), epilogue="heads")
    kvs = [proj(f"in_kv{g}", n=2 * ATT_GROUP_WIDTH, epilogue="heads",
                w_rows=_in_rows(4, stride=IN_OFFSETS[5] - IN_OFFSETS[4], first=g * ATT_GROUP_WIDTH))
           for g in range(ATT_N_GROUPS)]
    kvs_p, kvs_s = [kv[0] for kv in kvs], [kv[1] for kv in kvs]
    gates_p, gates_s = proj("in_gates", n=IN_SIZES[6] + IN_SIZES[7], w_rows=_in_rows(6), out_dtype=BF16)
    u_p, xbc_tail = matmul(h_p, w["in_t"], tm=tm, tn=IN_TILE, n=IN_SIZES[1], w_rows=_in_rows(1), name="in_xbc_conv",
                           epilogue="conv_silu", conv=(w["conv_w"], w["conv_b"]), seq=sp)
    xbc_s = matmul(h_s, w["in_t"], tm=tm, tn=IN_TILE, n=IN_SIZES[1], w_rows=_in_rows(1), name="in_xbc")
    conv_p = xbc_tail[:, SUBLANES - (CONV_WIDTH - 1):]
    xpad = jnp.concatenate([conv_state.reshape(bs, (CONV_WIDTH - 1) * CONV_DIM),
                            xbc_s.reshape(bs, ss * CONV_DIM)], axis=1)
    u_s = conv_sample(xpad, w["conv_w"], w["conv_b"], steps=ss).reshape(bs * ss, CONV_DIM)
    conv_s = xpad.reshape(bs, CONV_WIDTH - 1 + ss, CONV_DIM)[:, ss:]

    y_p, y_s, ssm_p, ssm_s = _ssd_branches(u_p, u_s, dt_p, dt_s, z_p, z_s, w, conv_state, ssm_state,
                                           bp=bp, sp=sp, bs=bs, ss=ss)
    o_p, o_s = _attention_branches(q_p, q_s, kvs_p, kvs_s, caches, rel_bias, bp=bp, sp=sp, bs=bs, ss=ss)

    merged_p = merge_proj(y_p, o_p, gates_p, w["ssm_proj"], w["att_proj"], tm=512, tn=1024)
    merged_s = merge_proj(y_s, o_s, gates_s, w["ssm_proj"], w["att_proj"], tm=512, tn=1024)
    out_proj = lambda a, res: matmul(a, w["out"], tm=tm, tn=1024, epilogue="residual", residual=res, name="out_proj")
    x1_p, x1_s = out_proj(merged_p, x_p), out_proj(merged_s, x_s)
    h2_p = rmsnorm(x1_p, w["norm_mlp"], BF16)
    h2_s = rmsnorm(x1_s, w["norm_mlp"], BF16)
    mlp_up = lambda a: matmul(a, w["up"], tm=tm, tn=1024, out_dtype=BF16, epilogue="relu2", name="mlp_up")
    up_p, up_s = mlp_up(h2_p), mlp_up(h2_s)
    mlp_down = lambda a, res: matmul(a, w["down"], tm=512, tn=1024, weight_buffers=1, epilogue="residual",
                                     residual=res, name="mlp_down")
    x2_p, x2_s = mlp_down(up_p, x1_p), mlp_down(up_s, x1_s)
    y_out_p = rmsnorm(x2_p, w["norm_final"], F32)
    y_out_s = rmsnorm(x2_s, w["norm_final"], F32)

    rows_p = [kv.reshape(bp, sp, 2, nh, hd)[:, sp - min(win, sp):] for kv, (win, _) in zip(kvs_p, ATT_PATTERNS)]
    rows_s = [kv.reshape(bs, ss, 2, nh, hd) for kv in kvs_s]
    return (y_out_p, rows_p, conv_p, ssm_p), (y_out_s, rows_s, conv_s, ssm_s)


def kernel(x_prompt, x_sample, cache_win128, cache_win512, cache_win2048, state_conv, state_ssm, w_in, conv_w, conv_b, dt_bias, a_log, d_skip, ssm_norm, w_ssm_proj, w_att_proj, w_out, norm_mix, w_up, w_down, norm_mlp, rel_bias, norm_final):
    assert w_in.shape[0] == 1, "single-layer model"
    bp, sp, _ = x_prompt.shape
    bs, ss, _ = x_sample.shape
    w = dict(
        in_t=jnp.swapaxes(w_in[0], 0, 1),
        conv_w=conv_w[0], conv_b=conv_b[0], dt_bias=dt_bias[0], a_log=a_log[0], d_skip=d_skip[0],
        ssm_norm=ssm_norm[0], ssm_proj=w_ssm_proj[0].astype(BF16), att_proj=w_att_proj[0].astype(BF16),
        out=w_out[0], norm_mix=norm_mix[0], up=w_up[0],
        down=w_down[0].astype(BF16), norm_mlp=norm_mlp[0], norm_final=norm_final)

    (yp, rows_p, conv_p, ssm_p), (ys, rows_s, conv_s, ssm_s) = _forward(
        x_prompt.reshape(bp * sp, D_MODEL), x_sample.reshape(bs * ss, D_MODEL), w,
        (cache_win128[0], cache_win512[0], cache_win2048[0]), state_conv[0], state_ssm[0], rel_bias,
        bp=bp, sp=sp, bs=bs, ss=ss)

    return (yp.reshape(bp, sp, D_MODEL), ys.reshape(bs, ss, D_MODEL),
            rows_p[0][None], rows_p[1][None], rows_p[2][None], conv_p[None], ssm_p[None],
            rows_s[0][None], rows_s[1][None], rows_s[2][None], conv_s[None], ssm_s[None])
```

```python
import functools
import math

import numpy as np
import jax
import jax.numpy as jnp
from jax import lax
from jax.experimental import pallas as pl
from jax.experimental.pallas import tpu as pltpu

D_MODEL = 2048
D_INNER = 2 * D_MODEL
SSM_HEAD_DIM = 64
SSM_HEADS = D_INNER // SSM_HEAD_DIM
SSM_GROUPS = 8
HEADS_PER_SSM_GROUP = SSM_HEADS // SSM_GROUPS
SSM_STATE = 128
SSM_GROUP_WIDTH = D_INNER // SSM_GROUPS
CONV_WIDTH = 4
CONV_DIM = D_INNER + 2 * SSM_GROUPS * SSM_STATE
SSD_CHUNK = 128
ATT_HEAD_DIM = 128
ATT_HEADS_PER_GROUP = 8
ATT_PATTERNS = ((128, 1), (512, 4), (2048, 16))
ATT_N_GROUPS = len(ATT_PATTERNS)
ATT_GROUP_WIDTH = ATT_HEADS_PER_GROUP * ATT_HEAD_DIM
ATT_WIDTH = ATT_N_GROUPS * ATT_GROUP_WIDTH
N_BUCKETS = 32
MAX_DISTANCE = max(w for w, _ in ATT_PATTERNS)
D_FF = 4 * D_MODEL
EPS = 1e-6

LANES = 128
SUBLANES = 8
VMEM_LIMIT_BYTES = 56 * 1024 * 1024

F32 = jnp.float32
BF16 = jnp.bfloat16
NEG_INF = float("-inf")


def _params(n_grid_dims):
    return pltpu.CompilerParams(
        dimension_semantics=("arbitrary",) * n_grid_dims,
        vmem_limit_bytes=VMEM_LIMIT_BYTES)


def _sigmoid(x):
    return 0.5 * jnp.tanh(0.5 * x) + 0.5


def _silu(x):
    return x * _sigmoid(x)


def _softplus(x):
    return jnp.maximum(x, 0.0) + jnp.log1p(jnp.exp(-jnp.abs(x)))


def _dot_nt(a, b):
    return lax.dot_general(a, b, (((1,), (1,)), ((), ())), preferred_element_type=F32)


def _dot_tn(a, b):
    return lax.dot_general(a, b, (((0,), (0,)), ((), ())), preferred_element_type=F32)


def _load_rows(ref, offset, rows, stride):
    flat = ref.reshape(math.prod(ref.shape[:-1]), ref.shape[-1])
    return flat[pl.ds(offset, rows, stride=stride), :]


def _store_rows(ref, offset, rows, stride, value):
    flat = ref.reshape(math.prod(ref.shape[:-1]), ref.shape[-1])
    flat[pl.ds(offset, rows, stride=stride), :] = value


def _load_head(ref, h):
    return _load_rows(ref, h, ref.shape[0], ref.shape[1])


def _store_head(ref, h, value):
    _store_rows(ref, h, ref.shape[0], ref.shape[1], value)


def _rmsnorm_kernel(x_ref, g_ref, o_ref):
    x = x_ref[...]
    ms = jnp.mean(x * x, axis=-1, keepdims=True)
    o_ref[...] = (x * lax.rsqrt(ms + EPS) * g_ref[...]).astype(o_ref.dtype)


def rmsnorm(x, g, out_dtype, tm=512):
    m, d = x.shape
    tm = min(tm, m)
    return pl.pallas_call(
        _rmsnorm_kernel,
        grid=(m // tm,),
        in_specs=[pl.BlockSpec((tm, d), lambda i: (i, 0)),
                  pl.BlockSpec((1, d), lambda i: (0, 0))],
        out_specs=pl.BlockSpec((tm, d), lambda i: (i, 0)),
        out_shape=jax.ShapeDtypeStruct((m, d), out_dtype),
        compiler_params=_params(1),
        name="rmsnorm",
    )(x, g.reshape(1, d))


def _bf16_weights(w_ref, wbf_ref):
    if wbf_ref is None:
        return w_ref

    @pl.when(pl.program_id(1) == 0)
    def _():
        wbf_ref[...] = w_ref[...].astype(BF16)

    return wbf_ref


CONV_COL_CHUNK = 256


def _tile_dot(a, w_ref, cols, w_transposed):
    if w_transposed:
        return _dot_nt(a, w_ref[cols, :])
    return jnp.dot(a, w_ref[:, cols], preferred_element_type=F32)


def _conv_silu_tiles(a_ref, w_ref, cw_ref, cb_ref, o_ref, tail_ref, pad_ref, *, seq_tiles, w_transposed):
    tm = a_ref.shape[0]
    first = SUBLANES - (CONV_WIDTH - 1)

    @pl.when(pl.program_id(1) % seq_tiles == 0)
    def _():
        pad_ref[0:SUBLANES, :] = jnp.zeros((SUBLANES, pad_ref.shape[1]), F32)

    a = a_ref[...]
    for c0 in range(0, o_ref.shape[1], CONV_COL_CHUNK):
        cols = slice(c0, c0 + CONV_COL_CHUNK)
        acc = _tile_dot(a, w_ref, cols, w_transposed)
        pad_ref[SUBLANES:SUBLANES + tm, cols] = acc
        conv = cb_ref[:, cols] + pad_ref[first:first + tm, cols] * cw_ref[0:1, cols]
        for i in range(1, CONV_WIDTH):
            conv = conv + pad_ref[first + i:first + i + tm, cols] * cw_ref[i:i + 1, cols]
        o_ref[:, cols] = _silu(conv)
        last_rows = acc[tm - SUBLANES:, :]
        pad_ref[0:SUBLANES, cols] = last_rows
        tail_ref[:, cols] = last_rows


def _write_epilogue(acc, epilogue, r_ref, o_ref):
    if epilogue == "residual":
        o_ref[...] = r_ref[...] + acc
    elif epilogue == "relu2":
        u = jnp.maximum(acc, 0.0)
        o_ref[...] = (u * u).astype(o_ref.dtype)
    elif epilogue == "heads":
        for j in range(o_ref.shape[1]):
            _store_head(o_ref, j, acc[:, j * LANES:(j + 1) * LANES])
    else:
        o_ref[...] = acc.astype(o_ref.dtype)


def _mm_kernel(*refs, epilogue, cast_w, seq_tiles, w_transposed, row_tiles, second):
    refs = list(refs)
    wbf_ref = pad_ref = None
    if epilogue == "conv_silu":
        pad_ref = refs.pop()
    if cast_w:
        wbf_ref = refs.pop()
    it = iter(refs)
    a_ref, w_ref = next(it), next(it)
    r_ref = next(it) if epilogue == "residual" else None
    cw_ref, cb_ref = (next(it), next(it)) if epilogue == "conv_silu" else (None, None)
    a2_ref = next(it) if second else None
    r2_ref = next(it) if second and epilogue == "residual" else None
    o_ref = next(it)
    tail_ref = next(it) if epilogue == "conv_silu" else None
    o2_ref = next(it) if second else None
    w_ref = _bf16_weights(w_ref, wbf_ref)

    def first_part():
        if epilogue == "conv_silu":
            _conv_silu_tiles(a_ref, w_ref, cw_ref, cb_ref, o_ref, tail_ref, pad_ref,
                             seq_tiles=seq_tiles, w_transposed=w_transposed)
        else:
            _write_epilogue(_tile_dot(a_ref[...], w_ref, slice(None), w_transposed), epilogue, r_ref, o_ref)

    def second_part():
        _write_epilogue(_tile_dot(a2_ref[...], w_ref, slice(None), w_transposed),
                        "none" if epilogue == "conv_silu" else epilogue, r2_ref, o2_ref)

    if second:
        pl.when(pl.program_id(1) < row_tiles)(first_part)
        pl.when(pl.program_id(1) == row_tiles)(second_part)
    else:
        first_part()


W_ROW_ALIGN = 64


def matmul(a, w, *, tm, tn, n=None, w_rows=None, out_dtype=F32, epilogue="none", residual=None,
           conv=None, seq=None, a2=None, residual2=None, name="matmul"):
    m, k = a.shape
    n = w.shape[1] if n is None else n
    tm = min(tm, m)
    tn = min(tn, n)
    assert m % tm == 0 and n % tn == 0
    row_tiles = m // tm
    second = a2 is not None
    cast_w = w.dtype != BF16
    row = (lambda i: jnp.minimum(i, row_tiles - 1)) if second else (lambda i: i)
    if w_rows is None:
        w_spec = pl.BlockSpec((k, tn), lambda j, i: (0, j))
        w_tile = (k, tn)
    else:
        w_spec = pl.BlockSpec((pl.Element(tn), pl.Element(k)),
                              lambda j, i: (pl.multiple_of(w_rows(j), W_ROW_ALIGN), 0))
        w_tile = (tn, k)
    in_specs = [pl.BlockSpec((tm, k), lambda j, i: (row(i), 0)), w_spec]
    args = [a, w]
    scratch = [pltpu.VMEM(w_tile, BF16)] if cast_w else []
    heads = epilogue == "heads"

    def out_for(rows, tile_rows, row_index):
        if heads:
            return (pl.BlockSpec((tile_rows, tn // LANES, LANES), lambda j, i: (row_index(i), j, 0)),
                    jax.ShapeDtypeStruct((rows, n // LANES, LANES), out_dtype))
        return (pl.BlockSpec((tile_rows, tn), lambda j, i: (row_index(i), j)),
                jax.ShapeDtypeStruct((rows, n), out_dtype))

    spec, shape = out_for(m, tm, row)
    out_specs, out_shapes = [spec], [shape]
    seq_tiles = None
    if epilogue == "residual":
        in_specs.append(pl.BlockSpec((tm, tn), lambda j, i: (row(i), j)))
        args.append(residual)
    elif epilogue == "conv_silu":
        assert seq % tm == 0
        seq_tiles = seq // tm
        conv_w, conv_b = conv
        in_specs += [pl.BlockSpec((CONV_WIDTH, tn), lambda j, i: (0, j)),
                     pl.BlockSpec((1, tn), lambda j, i: (0, j))]
        args += [conv_w, conv_b.reshape(1, n)]
        out_specs.append(pl.BlockSpec((None, SUBLANES, tn), lambda j, i: (row(i) // seq_tiles, 0, j)))
        out_shapes.append(jax.ShapeDtypeStruct((m // seq, SUBLANES, n), F32))
        scratch.append(pltpu.VMEM((tm + SUBLANES, tn), F32))
    if second:
        m2 = a2.shape[0]
        in_specs.append(pl.BlockSpec((m2, k), lambda j, i: (0, 0)))
        args.append(a2)
        if epilogue == "residual":
            in_specs.append(pl.BlockSpec((m2, tn), lambda j, i: (0, j)))
            args.append(residual2)
        spec, shape = out_for(m2, m2, lambda i: 0)
        out_specs.append(spec)
        out_shapes.append(shape)
    outs = pl.pallas_call(
        functools.partial(_mm_kernel, epilogue=epilogue, cast_w=cast_w, seq_tiles=seq_tiles,
                          w_transposed=w_rows is not None, row_tiles=row_tiles, second=second),
        grid=(n // tn, row_tiles + (1 if second else 0)),
        in_specs=in_specs,
        out_specs=out_specs,
        out_shape=out_shapes,
        scratch_shapes=scratch,
        compiler_params=_params(2),
        name=name,
    )(*args)
    return outs[0] if len(outs) == 1 else tuple(outs)


def _merge_kernel(*refs, cast_w, row_tiles, second):
    refs = list(refs)
    wsbf_ref = wabf_ref = None
    if cast_w:
        wabf_ref = refs.pop()
        wsbf_ref = refs.pop()
    it = iter(refs)
    first_in = [next(it) for _ in range(4)]
    ws_ref, wa_ref = next(it), next(it)
    second_in = [next(it) for _ in range(4)] if second else None
    out_ref = next(it)
    out2_ref = next(it) if second else None
    ws_ref = _bf16_weights(ws_ref, wsbf_ref)
    wa_ref = _bf16_weights(wa_ref, wabf_ref)

    def part(y_ref, o_ref, gs_ref, ga_ref, dst_ref):
        ps = jnp.dot(y_ref[...], ws_ref[...], preferred_element_type=F32)
        pa = jnp.dot(o_ref[...], wa_ref[...], preferred_element_type=F32)
        dst_ref[...] = (_sigmoid(gs_ref[...]) * ps + _sigmoid(ga_ref[...]) * pa).astype(dst_ref.dtype)

    if second:
        pl.when(pl.program_id(1) < row_tiles)(lambda: part(*first_in, out_ref))
        pl.when(pl.program_id(1) == row_tiles)(lambda: part(*second_in, out2_ref))
    else:
        part(*first_in, out_ref)


def merge_proj(y_ssm, o_att, gates, w_ssm, w_att, *, tm, tn, second=None):
    m = y_ssm.shape[0]
    n = w_ssm.shape[1]
    tm = min(tm, m)
    n_col = n // tn
    row_tiles = m // tm
    row = (lambda i: jnp.minimum(i, row_tiles - 1)) if second else (lambda i: i)
    cast_w = w_ssm.dtype != BF16
    scratch = []
    if cast_w:
        scratch = [pltpu.VMEM((w_ssm.shape[0], tn), BF16), pltpu.VMEM((w_att.shape[0], tn), BF16)]

    def operand_specs(tile_rows, row_index):
        return [pl.BlockSpec((tile_rows, y_ssm.shape[1]), lambda j, i: (row_index(i), 0)),
                pl.BlockSpec((tile_rows, o_att.shape[1]), lambda j, i: (row_index(i), 0)),
                pl.BlockSpec((tile_rows, tn), lambda j, i: (row_index(i), j)),
                pl.BlockSpec((tile_rows, tn), lambda j, i: (row_index(i), j + n_col))]

    in_specs = operand_specs(tm, row) + [pl.BlockSpec((w_ssm.shape[0], tn), lambda j, i: (0, j)),
                                        pl.BlockSpec((w_att.shape[0], tn), lambda j, i: (0, j))]
    args = [y_ssm, o_att, gates, gates, w_ssm, w_att]
    out_specs = [pl.BlockSpec((tm, tn), lambda j, i: (row(i), j))]
    out_shapes = [jax.ShapeDtypeStruct((m, n), BF16)]
    if second:
        y2, o2, g2 = second
        m2 = y2.shape[0]
        in_specs += operand_specs(m2, lambda i: 0)
        args += [y2, o2, g2, g2]
        out_specs.append(pl.BlockSpec((m2, tn), lambda j, i: (0, j)))
        out_shapes.append(jax.ShapeDtypeStruct((m2, n), BF16))
    outs = pl.pallas_call(
        functools.partial(_merge_kernel, cast_w=cast_w, row_tiles=row_tiles, second=bool(second)),
        grid=(n_col, row_tiles + (1 if second else 0)),
        in_specs=in_specs,
        out_specs=out_specs,
        out_shape=out_shapes,
        scratch_shapes=scratch,
        compiler_params=_params(2),
        name="merge_proj",
    )(*args)
    return outs[0] if len(outs) == 1 else tuple(outs)


def _conv_sample_kernel(x0_ref, x1_ref, x2_ref, x3_ref, w_ref, b_ref, o_ref):
    acc = b_ref[...] + x0_ref[...] * w_ref[0:1, :]
    for i, x_ref in enumerate((x1_ref, x2_ref, x3_ref), start=1):
        acc = acc + x_ref[...] * w_ref[i:i + 1, :]
    o_ref[...] = _silu(acc)


def conv_sample(xpad, conv_w, conv_b, *, steps, tc=512):
    bsz = xpad.shape[0]
    c = conv_w.shape[1]
    nct = c // tc
    x_specs = [pl.BlockSpec((bsz, tc), functools.partial(lambda t, j, i: (0, (t + i) * nct + j), i=i))
               for i in range(CONV_WIDTH)]
    return pl.pallas_call(
        _conv_sample_kernel,
        grid=(steps, nct),
        in_specs=x_specs + [pl.BlockSpec((CONV_WIDTH, tc), lambda t, j: (0, j)),
                            pl.BlockSpec((1, tc), lambda t, j: (0, j))],
        out_specs=pl.BlockSpec((bsz, tc), lambda t, j: (0, t * nct + j)),
        out_shape=jax.ShapeDtypeStruct((bsz, steps * c), F32),
        compiler_params=_params(2),
        name="conv_sample",
    )(xpad, xpad, xpad, xpad, conv_w, conv_b.reshape(1, c))


def _split3(a):
    hi = a.astype(BF16)
    rest = a - hi.astype(F32)
    mid = rest.astype(BF16)
    return hi, mid, (rest - mid.astype(F32)).astype(BF16)


def _ssd_group(gl, x_ref, b_ref, c_ref, dtc_ref, dtr_ref, z_ref, pc_ref, pr_ref, nrm_ref, y_ref, h_ref,
               ht_ref, *, q_len, valid_len, last_chunk):
    nh, hd = HEADS_PER_SSM_GROUP, SSM_HEAD_DIM
    pair_w = 2 * hd
    gw = SSM_GROUP_WIDTH
    x_cols = slice(gl * gw, (gl + 1) * gw)
    n_cols = slice(gl * SSM_STATE, (gl + 1) * SSM_STATE)

    pc = pc_ref[gl]
    pr = pr_ref[gl]
    dt_c = _softplus(dtc_ref[gl] + pc[0:1, :])
    dt_r = _softplus(dtr_ref[gl] + pr[:, 0:1])
    if valid_len < q_len:
        dt_c = jnp.where(lax.broadcasted_iota(jnp.int32, dt_c.shape, 0) < valid_len, dt_c, 0.0)
        dt_r = jnp.where(lax.broadcasted_iota(jnp.int32, dt_r.shape, 1) < valid_len, dt_r, 0.0)
    a_c = dt_c * (-jnp.exp(pc[1:2, :]))
    a_r = dt_r * (-jnp.exp(pr[:, 1:2]))
    ti = lax.broadcasted_iota(jnp.int32, (q_len, q_len), 0)
    tj = lax.broadcasted_iota(jnp.int32, (q_len, q_len), 1)
    causal = ti >= tj
    tril = causal.astype(BF16)
    triu = (ti <= tj).astype(BF16)
    acum_c = sum(jnp.dot(tril, part, preferred_element_type=F32) for part in _split3(a_c))
    acum_r = sum(jnp.dot(part, triu, preferred_element_type=F32) for part in _split3(a_r))
    end_c = acum_c[q_len - 1:q_len, :]
    dd_c = jnp.exp(end_c - acum_c) * dt_c
    src_r = acum_r - jnp.log(dt_r)

    x = x_ref[:, x_cols]
    cmat = c_ref[:, n_cols].astype(BF16)
    if ht_ref is None:
        bmat = b_ref[:, n_cols].astype(BF16)
        cb = _dot_nt(cmat, bmat)
        h_prev = h_ref[gl * nh:(gl + 1) * nh]
        y_off_all = _dot_nt(cmat, h_prev.reshape(nh * hd, SSM_STATE).astype(BF16))
    else:
        b_t = jnp.transpose(b_ref[:, n_cols]).astype(BF16)
        cb = jnp.dot(cmat, b_t, preferred_element_type=F32)
        ht_prev = ht_ref[gl]
        y_off_all = jnp.dot(cmat, ht_prev.astype(BF16), preferred_element_type=F32)

    lane = lax.broadcasted_iota(jnp.int32, (q_len, pair_w), 1)
    first_half = lane < hd

    def pair_bcast(cols, j0):
        return jnp.where(first_half,
                         jnp.broadcast_to(cols[:, j0:j0 + 1], (q_len, pair_w)),
                         jnp.broadcast_to(cols[:, j0 + 1:j0 + 2], (q_len, pair_w)))

    yield

    y_diags = []
    for i in range(nh // 2):
        j0 = 2 * i
        x_pair = x[:, i * pair_w:(i + 1) * pair_w]
        y_diag = jnp.zeros((q_len, pair_w), F32)
        for j in (j0, j0 + 1):
            seg = jnp.broadcast_to(acum_c[:, j:j + 1], (q_len, q_len)) - src_r[j:j + 1, :]
            w = (cb * jnp.exp(jnp.where(causal, seg, NEG_INF))).astype(BF16)
            own = first_half if j == j0 else jnp.logical_not(first_half)
            y_diag = y_diag + jnp.dot(w, jnp.where(own, x_pair, 0.0).astype(BF16),
                                      preferred_element_type=F32)
        y_diags.append(y_diag)
    yield

    ys = []
    xws = []
    sumsq = jnp.zeros((q_len, 1), F32)
    for i in range(nh // 2):
        j0 = 2 * i
        x_pair = x[:, i * pair_w:(i + 1) * pair_w]
        e_pair = jnp.exp(pair_bcast(acum_c, j0))
        d_pair = jnp.where(first_half[0:1, :], pc[2:3, j0:j0 + 1], pc[2:3, j0 + 1:j0 + 2])
        y = y_diags[i] + y_off_all[:, i * pair_w:(i + 1) * pair_w] * e_pair + d_pair * x_pair
        y = y * _silu(z_ref[:, gl * gw + i * pair_w:gl * gw + (i + 1) * pair_w])
        sumsq = sumsq + jnp.sum(y * y, axis=-1, keepdims=True)
        ys.append(y)
        xws.append((x_pair * pair_bcast(dd_c, j0)).astype(BF16))

    if ht_ref is None:
        for i in range(nh // 2):
            s_pair = _dot_tn(xws[i], bmat)
            for jj, j in enumerate((2 * i, 2 * i + 1)):
                chunk_decay = jnp.exp(acum_r[j:j + 1, q_len - 1:q_len])
                h_ref[gl * nh + j] = h_prev[j] * chunk_decay + s_pair[jj * hd:(jj + 1) * hd, :]
    else:
        end_decay = jnp.exp(end_c)
        decay_row = jnp.concatenate(
            [jnp.where(first_half[0:1, :], end_decay[:, 2 * i:2 * i + 1], end_decay[:, 2 * i + 1:2 * i + 2])
             for i in range(nh // 2)], axis=1)
        ht_new = ht_prev * decay_row + jnp.dot(b_t, jnp.concatenate(xws, axis=1), preferred_element_type=F32)
        ht_ref[gl] = ht_new

        @pl.when(last_chunk)
        def _():
            for i in range(nh // 2):
                both = jnp.transpose(ht_new[:, i * pair_w:(i + 1) * pair_w])
                h_ref[gl * nh + 2 * i] = both[:hd, :]
                h_ref[gl * nh + 2 * i + 1] = both[hd:, :]

    inv = lax.rsqrt(sumsq * (1.0 / (nh * hd)) + EPS)
    for i in range(nh // 2):
        sl = slice(gl * gw + i * pair_w, gl * gw + (i + 1) * pair_w)
        y_ref[:, sl] = (ys[i] * inv * nrm_ref[:, sl]).astype(y_ref.dtype)


def _ssd_kernel(*refs, q_len, valid_len, has_h0, groups_per_step):
    if has_h0:
        h0_ref = refs[9]
        refs = refs[:9] + refs[10:] + (None,)
    h_ref, ht_ref = refs[-2:]
    chunk = pl.program_id(2)

    @pl.when(chunk == 0)
    def _():
        if has_h0:
            h_ref[...] = h0_ref[...]
        else:
            ht_ref[...] = jnp.zeros(ht_ref.shape, F32)

    groups = [_ssd_group(gl, *refs, q_len=q_len, valid_len=valid_len,
                         last_chunk=chunk == pl.num_programs(2) - 1) for gl in range(groups_per_step)]
    if has_h0:
        while groups:
            groups = [g for g in groups if next(g, StopIteration) is not StopIteration]
    else:
        for g in groups:
            for _ in g:
                pass


def ssd(u, dt_raw, z, dt_bias, a_log, d_skip, ssm_norm, h0, *, batch, seq, q_len, valid_len, groups_per_step):
    nh, g, gps = HEADS_PER_SSM_GROUP, SSM_GROUPS, groups_per_step
    nc = seq // q_len
    m = batch * seq
    dt4 = dt_raw[:, :SSM_HEADS].reshape(batch * nc, q_len, g, nh)
    dtc = jnp.transpose(dt4, (0, 2, 1, 3))
    dtr = jnp.transpose(dt4, (0, 2, 3, 1))
    pcol = jnp.stack([dt_bias, a_log, d_skip]).reshape(3, g, nh).transpose(1, 0, 2)
    prow = jnp.transpose(pcol, (0, 2, 1))
    b_off = D_INNER // (gps * SSM_STATE)
    c_off = b_off + g // gps
    row = lambda b, gi, c: b * nc + c
    in_specs = [
        pl.BlockSpec((q_len, gps * SSM_GROUP_WIDTH), lambda b, gi, c: (row(b, gi, c), gi)),
        pl.BlockSpec((q_len, gps * SSM_STATE), lambda b, gi, c: (row(b, gi, c), b_off + gi)),
        pl.BlockSpec((q_len, gps * SSM_STATE), lambda b, gi, c: (row(b, gi, c), c_off + gi)),
        pl.BlockSpec((None, gps, q_len, nh), lambda b, gi, c: (row(b, gi, c), gi, 0, 0)),
        pl.BlockSpec((None, gps, nh, q_len), lambda b, gi, c: (row(b, gi, c), gi, 0, 0)),
        pl.BlockSpec((q_len, gps * SSM_GROUP_WIDTH), lambda b, gi, c: (row(b, gi, c), gi)),
        pl.BlockSpec((gps, 3, nh), lambda b, gi, c: (gi, 0, 0)),
        pl.BlockSpec((gps, nh, 3), lambda b, gi, c: (gi, 0, 0)),
        pl.BlockSpec((1, gps * SSM_GROUP_WIDTH), lambda b, gi, c: (0, gi)),
    ]
    args = [u, u, u, dtc, dtr, z, pcol, prow, ssm_norm.reshape(1, D_INNER)]
    h_spec = pl.BlockSpec((None, gps * nh, SSM_HEAD_DIM, SSM_STATE), lambda b, gi, c: (b, gi, 0, 0))
    scratch = []
    if h0 is not None:
        in_specs.append(h_spec)
        args.append(h0)
    else:
        scratch.append(pltpu.VMEM((gps, SSM_STATE, SSM_GROUP_WIDTH), F32))
    return pl.pallas_call(
        functools.partial(_ssd_kernel, q_len=q_len, valid_len=valid_len, has_h0=h0 is not None,
                          groups_per_step=gps),
        grid=(batch, g // gps, nc),
        in_specs=in_specs,
        out_specs=[pl.BlockSpec((q_len, gps * SSM_GROUP_WIDTH), lambda b, gi, c: (row(b, gi, c), gi)), h_spec],
        out_shape=[jax.ShapeDtypeStruct((m, D_INNER), BF16),
                   jax.ShapeDtypeStruct((batch, SSM_HEADS, SSM_HEAD_DIM, SSM_STATE), F32)],
        scratch_shapes=scratch,
        compiler_params=_params(3),
        name="ssd",
    )(*args)


def _attn_prompt_kernel(*refs, kb, has_prev):
    if has_prev:
        q_ref, kc_ref, vc_ref, mb_ref, o_ref, l_ref, kp_ref, vp_ref = refs
    else:
        q_ref, kc_ref, vc_ref, mb_ref, o_ref, l_ref = refs
    not_first = pl.program_id(2) > 0
    scale = ATT_HEAD_DIM ** -0.5
    heads = range(ATT_HEADS_PER_GROUP)
    if has_prev:
        @pl.when(pl.program_id(2) == 0)
        def _():
            kp_ref[...] = jnp.zeros(kp_ref.shape, BF16)
            vp_ref[...] = jnp.zeros(vp_ref.shape, BF16)

    scores = []
    for h in heads:
        q = (_load_head(q_ref, h) * scale).astype(BF16)
        k_cur = _load_head(kc_ref, h).astype(BF16)
        s_c = _dot_nt(q, k_cur) + mb_ref[h, :, kb:]
        s_p = None
        if has_prev:
            s_p = _dot_nt(q, kp_ref[h]) + mb_ref[h, :, :kb]
            s_p = jnp.where(not_first, s_p, NEG_INF)
            kp_ref[h] = k_cur
        scores.append((s_c, s_p))
    probs = []
    for s_c, s_p in scores:
        m = jnp.max(s_c, axis=-1, keepdims=True)
        if has_prev:
            m = jnp.maximum(m, jnp.max(s_p, axis=-1, keepdims=True))
        p_c = jnp.exp(s_c - m)
        den = jnp.sum(p_c, axis=-1, keepdims=True)
        p_p = None
        if has_prev:
            p_p = jnp.exp(s_p - m)
            den = den + jnp.sum(p_p, axis=-1, keepdims=True)
            p_p = p_p.astype(BF16)
        probs.append((p_c.astype(BF16), p_p, den, m + jnp.log(den)))
    for h, (p_c, p_p, den, lse) in zip(heads, probs):
        v_cur = _load_head(vc_ref, h).astype(BF16)
        o = jnp.dot(p_c, v_cur, preferred_element_type=F32)
        if has_prev:
            o = o + jnp.dot(p_p, vp_ref[h], preferred_element_type=F32)
            vp_ref[h] = v_cur
        _store_head(o_ref, h, o / den)
        _store_head(l_ref, h, jnp.broadcast_to(lse, (kb, ATT_HEAD_DIM)))


def attn_prompt(q, kv, mbias, *, group, batch, seq, dil, kb):
    m_res = seq // dil
    nb = m_res // kb
    has_prev = nb > 1
    nh = ATT_HEADS_PER_GROUP
    q4 = q.reshape(batch * m_res, dil, q.shape[1], ATT_HEAD_DIM)
    kv4 = kv.reshape(batch * m_res, dil, 2 * nh, ATT_HEAD_DIM)

    def cur(head_block):
        return pl.BlockSpec((kb, None, nh, ATT_HEAD_DIM), lambda b, r, n: (b * nb + n, r, head_block, 0))

    carry = [pltpu.VMEM((nh, kb, ATT_HEAD_DIM), BF16)] * 2 if has_prev else []
    o, lse = pl.pallas_call(
        functools.partial(_attn_prompt_kernel, kb=kb, has_prev=has_prev),
        grid=(batch, dil, nb),
        in_specs=[cur(group), cur(0), cur(1), pl.BlockSpec(mbias.shape, lambda b, r, n: (0, 0, 0))],
        out_specs=[cur(0), cur(0)],
        out_shape=[jax.ShapeDtypeStruct((batch * m_res, dil, nh, ATT_HEAD_DIM), F32)] * 2,
        scratch_shapes=carry,
        compiler_params=_params(3),
        name=f"attn_prompt_g{group}",
    )(q4, kv4, kv4, mbias)
    return o.reshape(batch * seq, nh, ATT_HEAD_DIM), lse.reshape(batch * seq, nh, ATT_HEAD_DIM)


def _attn_sample_kernel(*refs, n_sets, steps, q_rows):
    q_ref, new_ref = refs[:2]
    cache_refs = refs[2:2 + n_sets]
    bias_ref, o_ref, l_ref = refs[2 + n_sets:]
    gw = ATT_GROUP_WIDTH
    nh = ATT_HEADS_PER_GROUP
    hd = ATT_HEAD_DIM
    scale = ATT_HEAD_DIM ** -0.5
    bb = q_ref.shape[0]
    cache_rows = cache_refs[0].shape[1]
    new_rows = new_ref.shape[1]
    row_stride = 2 * nh
    per_set = steps // n_sets
    sub = lax.broadcasted_iota(jnp.int32, (nh, gw), 0)
    own = sub == lax.broadcasted_iota(jnp.int32, (nh, gw), 1) // hd
    sets = [(bi, ci) for bi in range(bb) for ci in range(n_sets)]

    def keys_or_values(bi, ci, first_head):
        base = bi * cache_rows * row_stride
        return jnp.concatenate(
            [jnp.concatenate([_load_rows(cache_refs[ci], base + first_head + h, cache_rows, row_stride),
                              _load_rows(new_ref, bi * new_rows * 2 * nh + first_head + h, new_rows, 2 * nh)], axis=0)
             for h in range(nh)], axis=1).astype(BF16)

    scores = []
    for bi, ci in sets:
        q_blocks = []
        for tt in range(per_set):
            q_t = q_ref[bi, ci * per_set + tt] * scale
            q_blocks.append(jnp.where(own, jnp.concatenate([q_t] * nh, axis=1), 0.0))
        if q_rows > per_set * nh:
            q_blocks.append(jnp.zeros((q_rows - per_set * nh, gw), F32))
        qbd = jnp.concatenate(q_blocks, axis=0).astype(BF16)
        scores.append(_dot_nt(qbd, keys_or_values(bi, ci, 0)) + bias_ref[ci])
    probs = []
    for s in scores:
        m = jnp.max(s, axis=-1, keepdims=True)
        p = jnp.exp(s - m)
        den = jnp.sum(p, axis=-1, keepdims=True)
        probs.append((p.astype(BF16), den, m + jnp.log(den)))
    for (bi, ci), (p, den, lse) in zip(sets, probs):
        o_all = jnp.dot(p, keys_or_values(bi, ci, nh), preferred_element_type=F32) / den
        for tt in range(per_set):
            t = ci * per_set + tt
            rows = slice(tt * nh, (tt + 1) * nh)
            o_t = jnp.zeros((nh, hd), F32)
            for h in range(nh):
                o_t = o_t + jnp.where(sub[:, :hd] == h, o_all[rows, h * hd:(h + 1) * hd], 0.0)
            o_ref[bi, t] = o_t
            l_ref[bi, t] = jnp.broadcast_to(lse[rows, :], (nh, hd))


def attn_sample(q, kv_new_pad, cache, bias, *, group, batch, steps, dil, batch_block):
    nh, hd = ATT_HEADS_PER_GROUP, ATT_HEAD_DIM
    bb = batch_block
    n_sets, q_rows, n_keys_tot = bias.shape
    new_rows = kv_new_pad.shape[1]
    cache_rows = n_keys_tot - new_rows
    q4 = q.reshape(batch, steps, q.shape[1], hd)
    if n_sets == 1:
        cache_view = cache.reshape(batch, cache_rows, 2 * nh, hd)
        cache_specs = [pl.BlockSpec((bb, cache_rows, 2 * nh, hd), lambda b: (b, 0, 0, 0))]
    else:
        cache_view = cache.reshape(batch, cache_rows, dil, 2 * nh, hd)
        cache_specs = [pl.BlockSpec((bb, cache_rows, None, 2 * nh, hd),
                                    functools.partial(lambda b, t: (b, 0, t, 0, 0), t=t))
                       for t in range(n_sets)]
    out_spec = pl.BlockSpec((bb, steps, nh, hd), lambda b: (b, 0, 0, 0))
    o, lse = pl.pallas_call(
        functools.partial(_attn_sample_kernel, n_sets=n_sets, steps=steps, q_rows=q_rows),
        grid=(batch // bb,),
        in_specs=[pl.BlockSpec((bb, steps, nh, hd), lambda b: (b, 0, group, 0)),
                  pl.BlockSpec((bb, new_rows, 2 * nh, hd), lambda b: (b, 0, 0, 0))]
                 + cache_specs
                 + [pl.BlockSpec(bias.shape, lambda b: (0, 0, 0))],
        out_specs=[out_spec, out_spec],
        out_shape=[jax.ShapeDtypeStruct((batch, steps, nh, hd), F32)] * 2,
        compiler_params=_params(1),
        name=f"attn_sample_g{group}",
    )(q4, kv_new_pad, *([cache_view] * n_sets), bias)
    return o.reshape(batch * steps, nh, hd), lse.reshape(batch * steps, nh, hd)


def _attn_combine_kernel(o0, o1, o2, l0, l1, l2, out_ref, mix_ref):
    ls = (l0[...], l1[...], l2[...])
    m = jnp.maximum(jnp.maximum(ls[0], ls[1]), ls[2])
    ws = [jnp.exp(l - m) for l in ls]
    num = ws[0] * o0[...] + ws[1] * o1[...] + ws[2] * o2[...]
    mix_ref[...] = num / (ws[0] + ws[1] + ws[2])
    for h in range(ATT_HEADS_PER_GROUP):
        out_ref[:, h * ATT_HEAD_DIM:(h + 1) * ATT_HEAD_DIM] = _load_head(mix_ref, h).astype(out_ref.dtype)


def attn_combine(outs, lses, tm=512):
    m, nh, hd = outs[0].shape
    tm = min(tm, m)
    spec = pl.BlockSpec((tm, nh, hd), lambda i: (i, 0, 0))
    return pl.pallas_call(
        _attn_combine_kernel,
        grid=(m // tm,),
        in_specs=[spec] * 6,
        out_specs=pl.BlockSpec((tm, nh * hd), lambda i: (i, 0)),
        out_shape=jax.ShapeDtypeStruct((m, nh * hd), BF16),
        scratch_shapes=[pltpu.VMEM((tm, nh, hd), F32)],
        compiler_params=_params(1),
        name="attn_combine",
    )(*outs, *lses)


def _t5_bucket(dist):
    max_exact = N_BUCKETS // 2
    d = np.maximum(dist, max_exact).astype(np.float32)
    large = max_exact + (np.log(d / max_exact) / np.log(MAX_DISTANCE / max_exact)
                         * (N_BUCKETS - max_exact)).astype(np.int32)
    large = np.minimum(large, N_BUCKETS - 1)
    return np.where(dist < max_exact, dist, large).astype(np.int32)


def _group_bias(rel_bias, g):
    win, dil = ATT_PATTERNS[g]
    buckets = _t5_bucket(np.arange(win // dil + 1) * dil)
    return rel_bias[buckets][:, g * ATT_HEADS_PER_GROUP:(g + 1) * ATT_HEADS_PER_GROUP]


def _prompt_mask_bias(bias_g, n_keys, kb):
    period = 3 * kb
    offs = np.arange(period)
    offs = np.where(offs >= 2 * kb, offs - period, offs)
    j_of = np.clip(kb - offs, 0, n_keys)
    a = jnp.transpose(bias_g[j_of]).astype(F32)
    nh = a.shape[0]
    tiled = jnp.tile(a, (1, kb))[:, :kb * (period - 1)].reshape(nh, kb, period - 1)
    rel = np.arange(kb)[:, None] + kb - np.arange(2 * kb)[None, :]
    valid = (rel >= 0) & (rel <= n_keys)
    return jnp.where(valid[None], tiled[:, :, :2 * kb], NEG_INF)


def _sample_mask_bias(bias_g, *, cache_len, dil, n_keys, steps, new_rows, shared_cache, q_rows):
    nh = ATT_HEADS_PER_GROUP
    n_cache = 1 if shared_cache else steps
    per_cache = steps // n_cache
    rows_per_cache = cache_len if shared_cache else cache_len // dil
    blocks = []
    for ci in range(n_cache):
        if shared_cache:
            cache_pos = np.arange(rows_per_cache)
        else:
            cache_pos = ci + dil * np.arange(rows_per_cache)
        key_pos = np.concatenate([cache_pos, cache_len + np.arange(new_rows)])
        key_live = np.concatenate([np.ones(rows_per_cache, bool), np.arange(new_rows) < steps])
        rows = []
        for tt in range(per_cache):
            t = ci * per_cache + tt
            diff = cache_len + t - key_pos
            valid = key_live & (diff >= 0) & (diff % dil == 0) & (diff // dil <= n_keys)
            j = np.clip(diff // dil, 0, n_keys)
            rows.append(jnp.where(valid[None, :], bias_g[j].T.astype(F32), NEG_INF))
        blk = jnp.concatenate(rows, axis=0)
        pad = q_rows - per_cache * nh
        if pad:
            blk = jnp.concatenate([blk, jnp.zeros((pad, blk.shape[1]), F32)], axis=0)
        blocks.append(blk)
    return jnp.stack(blocks)


SAMPLE_NEW_ROWS = 16
SAMPLE_Q_ROWS_MIN = 16
SSD_SAMPLE_Q = 16
SAMPLE_SHARED_BATCH_BLOCK = 4
SAMPLE_DILATED_BATCH_BLOCK = 4
SSD_PROMPT_GROUPS_PER_STEP = 8


IN_SIZES = (D_INNER, CONV_DIM, SSM_HEADS, ATT_WIDTH, ATT_WIDTH, ATT_WIDTH, D_MODEL, D_MODEL)
IN_OFFSETS = tuple(int(o) for o in np.concatenate([[0], np.cumsum(IN_SIZES)]))


IN_TILE = 1024
assert all(o % W_ROW_ALIGN == 0 for o in IN_OFFSETS)


def _in_rows(seg, stride=IN_TILE, first=0):
    return lambda j: IN_OFFSETS[seg] + first + j * stride


PROMPT_ROW_TILE = 1024


def _ssd_branches(u_p, u_s, dt_p, dt_s, z_p, z_s, w, conv_state, ssm_state, *, bp, sp, bs, ss):
    params = (w["dt_bias"], w["a_log"], w["d_skip"], w["ssm_norm"])
    y_p, ssm_p = ssd(u_p, dt_p, z_p, *params, None, batch=bp, seq=sp, q_len=SSD_CHUNK,
                     valid_len=SSD_CHUNK, groups_per_step=SSD_PROMPT_GROUPS_PER_STEP)
    qp = SSD_SAMPLE_Q
    pad_rows = lambda a: jnp.pad(a.reshape(bs, ss, a.shape[-1]),
                                 ((0, 0), (0, qp - ss), (0, 0))).reshape(bs * qp, a.shape[-1])
    y_pad, ssm_s = ssd(pad_rows(u_s), pad_rows(dt_s), pad_rows(z_s), *params, ssm_state,
                       batch=bs, seq=qp, q_len=qp, valid_len=ss, groups_per_step=SSM_GROUPS)
    y_s = y_pad.reshape(bs, qp, D_INNER)[:, :ss].reshape(bs * ss, D_INNER)
    return y_p, y_s, ssm_p, ssm_s


def _attention_branches(q_p, q_s, kvs_p, kvs_s, caches, rel_bias, *, bp, sp, bs, ss):
    nh, hd = ATT_HEADS_PER_GROUP, ATT_HEAD_DIM
    outs_p, lses_p, outs_s, lses_s = [], [], [], []
    for g, (win, dil) in enumerate(ATT_PATTERNS):
        n_keys = win // dil
        bias_g = _group_bias(rel_bias, g)
        mb = _prompt_mask_bias(bias_g, n_keys, n_keys)
        o, lse = attn_prompt(q_p, kvs_p[g], mb, group=g, batch=bp, seq=sp, dil=dil, kb=n_keys)
        outs_p.append(o)
        lses_p.append(lse)

        cache = caches[g]
        cache_len = cache.shape[1]
        shared = dil == 1
        q_rows = max(SAMPLE_Q_ROWS_MIN, (ss if shared else 1) * nh)
        sb = _sample_mask_bias(bias_g, cache_len=cache_len, dil=dil, n_keys=n_keys, steps=ss,
                               new_rows=SAMPLE_NEW_ROWS, shared_cache=shared, q_rows=q_rows)
        kv_new = jnp.pad(kvs_s[g].reshape(bs, ss, 2 * nh, hd),
                         ((0, 0), (0, SAMPLE_NEW_ROWS - ss), (0, 0), (0, 0)))
        o, lse = attn_sample(q_s, kv_new, cache, sb, group=g, batch=bs, steps=ss, dil=dil,
                             batch_block=SAMPLE_SHARED_BATCH_BLOCK if shared else SAMPLE_DILATED_BATCH_BLOCK)
        outs_s.append(o)
        lses_s.append(lse)
    return attn_combine(outs_p, lses_p), attn_combine(outs_s, lses_s)


def _forward(x_p, x_s, w, caches, conv_state, ssm_state, rel_bias, *, bp, sp, bs, ss):
    nh, hd = ATT_HEADS_PER_GROUP, ATT_HEAD_DIM
    tm = PROMPT_ROW_TILE
    h_p = rmsnorm(x_p, w["norm_mix"], BF16)
    h_s = rmsnorm(x_s, w["norm_mix"], BF16)
    proj = lambda name, **kw: matmul(h_p, w["in_t"], a2=h_s, tm=tm, tn=IN_TILE, name=name, **kw)
    z_p, z_s = proj("in_z", n=IN_SIZES[0], w_rows=_in_rows(0))
    dt_p, dt_s = matmul(h_p, w["in_t"], a2=h_s, tm=tm, tn=LANES, n=LANES, w_rows=_in_rows(2), name="in_dt")
    q_p, q_s = proj("in_q", n=IN_SIZES[3], w_rows=_in_rows(3), epilogue="heads")
    kvs = [proj(f"in_kv{g}", n=2 * ATT_GROUP_WIDTH, epilogue="heads",
                w_rows=_in_rows(4, stride=IN_OFFSETS[5] - IN_OFFSETS[4], first=g * ATT_GROUP_WIDTH))
           for g in range(ATT_N_GROUPS)]
    kvs_p, kvs_s = [kv[0] for kv in kvs], [kv[1] for kv in kvs]
    gates_p, gates_s = proj("in_gates", n=IN_SIZES[6] + IN_SIZES[7], w_rows=_in_rows(6))
    u_p, xbc_tail = matmul(h_p, w["in_t"], tm=tm, tn=IN_TILE, n=IN_SIZES[1], w_rows=_in_rows(1), name="in_xbc_conv",
                           epilogue="conv_silu", conv=(w["conv_w"], w["conv_b"]), seq=sp)
    xbc_s = matmul(h_s, w["in_t"], tm=tm, tn=IN_TILE, n=IN_SIZES[1], w_rows=_in_rows(1), name="in_xbc")
    conv_p = xbc_tail[:, SUBLANES - (CONV_WIDTH - 1):]
    xpad = jnp.concatenate([conv_state.reshape(bs, (CONV_WIDTH - 1) * CONV_DIM),
                            xbc_s.reshape(bs, ss * CONV_DIM)], axis=1)
    u_s = conv_sample(xpad, w["conv_w"], w["conv_b"], steps=ss).reshape(bs * ss, CONV_DIM)
    conv_s = xpad.reshape(bs, CONV_WIDTH - 1 + ss, CONV_DIM)[:, ss:]

    y_p, y_s, ssm_p, ssm_s = _ssd_branches(u_p, u_s, dt_p, dt_s, z_p, z_s, w, conv_state, ssm_state,
                                           bp=bp, sp=sp, bs=bs, ss=ss)
    o_p, o_s = _attention_branches(q_p, q_s, kvs_p, kvs_s, caches, rel_bias, bp=bp, sp=sp, bs=bs, ss=ss)

    merged_p = merge_proj(y_p, o_p, gates_p, w["ssm_proj"], w["att_proj"], tm=tm, tn=512)
    merged_s = merge_proj(y_s, o_s, gates_s, w["ssm_proj"], w["att_proj"], tm=tm, tn=512)
    out_proj = lambda a, res: matmul(a, w["out"], tm=tm, tn=1024, epilogue="residual", residual=res, name="out_proj")
    x1_p, x1_s = out_proj(merged_p, x_p), out_proj(merged_s, x_s)
    h2_p = rmsnorm(x1_p, w["norm_mlp"], BF16)
    h2_s = rmsnorm(x1_s, w["norm_mlp"], BF16)
    mlp_up = lambda a: matmul(a, w["up"], tm=tm, tn=1024, out_dtype=BF16, epilogue="relu2", name="mlp_up")
    up_p, up_s = mlp_up(h2_p), mlp_up(h2_s)
    mlp_down = lambda a, res: matmul(a, w["down"], tm=512, tn=512, epilogue="residual", residual=res,
                                     name="mlp_down")
    x2_p, x2_s = mlp_down(up_p, x1_p), mlp_down(up_s, x1_s)
    y_out_p = rmsnorm(x2_p, w["norm_final"], F32)
    y_out_s = rmsnorm(x2_s, w["norm_final"], F32)

    rows_p = [kv.reshape(bp, sp, 2, nh, hd)[:, sp - min(win, sp):] for kv, (win, _) in zip(kvs_p, ATT_PATTERNS)]
    rows_s = [kv.reshape(bs, ss, 2, nh, hd) for kv in kvs_s]
    return (y_out_p, rows_p, conv_p, ssm_p), (y_out_s, rows_s, conv_s, ssm_s)


def kernel(x_prompt, x_sample, cache_win128, cache_win512, cache_win2048, state_conv, state_ssm, w_in, conv_w, conv_b, dt_bias, a_log, d_skip, ssm_norm, w_ssm_proj, w_att_proj, w_out, norm_mix, w_up, w_down, norm_mlp, rel_bias, norm_final):
    assert w_in.shape[0] == 1, "single-layer model"
    bp, sp, _ = x_prompt.shape
    bs, ss, _ = x_sample.shape
    w = dict(
        in_t=jnp.swapaxes(w_in[0], 0, 1),
        conv_w=conv_w[0], conv_b=conv_b[0], dt_bias=dt_bias[0], a_log=a_log[0], d_skip=d_skip[0],
        ssm_norm=ssm_norm[0], ssm_proj=w_ssm_proj[0].astype(BF16), att_proj=w_att_proj[0].astype(BF16),
        out=w_out[0], norm_mix=norm_mix[0], up=w_up[0],
        down=w_down[0].astype(BF16), norm_mlp=norm_mlp[0], norm_final=norm_final)

    (yp, rows_p, conv_p, ssm_p), (ys, rows_s, conv_s, ssm_s) = _forward(
        x_prompt.reshape(bp * sp, D_MODEL), x_sample.reshape(bs * ss, D_MODEL), w,
        (cache_win128[0], cache_win512[0], cache_win2048[0]), state_conv[0], state_ssm[0], rel_bias,
        bp=bp, sp=sp, bs=bs, ss=ss)

    return (yp.reshape(bp, sp, D_MODEL), ys.reshape(bs, ss, D_MODEL),
            rows_p[0][None], rows_p[1][None], rows_p[2][None], conv_p[None], ssm_p[None],
            rows_s[0][None], rows_s[1][None], rows_s[2][None], conv_s[None], ssm_s[None])
```

```python
import functools
import math

import numpy as np
import jax
import jax.numpy as jnp
from jax import lax
from jax.experimental import pallas as pl
from jax.experimental.pallas import tpu as pltpu

D_MODEL = 2048
D_INNER = 2 * D_MODEL
SSM_HEAD_DIM = 64
SSM_HEADS = D_INNER // SSM_HEAD_DIM
SSM_GROUPS = 8
HEADS_PER_SSM_GROUP = SSM_HEADS // SSM_GROUPS
SSM_STATE = 128
SSM_GROUP_WIDTH = D_INNER // SSM_GROUPS
CONV_WIDTH = 4
CONV_DIM = D_INNER + 2 * SSM_GROUPS * SSM_STATE
SSD_CHUNK = 128
ATT_HEAD_DIM = 128
ATT_HEADS_PER_GROUP = 8
ATT_PATTERNS = ((128, 1), (512, 4), (2048, 16))
ATT_N_GROUPS = len(ATT_PATTERNS)
ATT_GROUP_WIDTH = ATT_HEADS_PER_GROUP * ATT_HEAD_DIM
ATT_WIDTH = ATT_N_GROUPS * ATT_GROUP_WIDTH
N_BUCKETS = 32
MAX_DISTANCE = max(w for w, _ in ATT_PATTERNS)
EPS = 1e-6

LANES = 128
SUBLANES = 8
VMEM_LIMIT_BYTES = 56 * 1024 * 1024

F32 = jnp.float32
BF16 = jnp.bfloat16
NEG_INF = float("-inf")


def _params(n_grid_dims):
    return pltpu.CompilerParams(
        dimension_semantics=("arbitrary",) * n_grid_dims,
        vmem_limit_bytes=VMEM_LIMIT_BYTES)


def _sigmoid(x):
    return 0.5 * jnp.tanh(0.5 * x) + 0.5


def _silu(x):
    return x * _sigmoid(x)


def _softplus(x):
    return jnp.maximum(x, 0.0) + jnp.log1p(jnp.exp(-jnp.abs(x)))


def _dot_nt(a, b):
    return lax.dot_general(a, b, (((1,), (1,)), ((), ())), preferred_element_type=F32)


def _dot_tn(a, b):
    return lax.dot_general(a, b, (((0,), (0,)), ((), ())), preferred_element_type=F32)


def _load_rows(ref, offset, rows, stride):
    flat = ref.reshape(math.prod(ref.shape[:-1]), ref.shape[-1])
    return flat[pl.ds(offset, rows, stride=stride), :]


def _store_rows(ref, offset, rows, stride, value):
    flat = ref.reshape(math.prod(ref.shape[:-1]), ref.shape[-1])
    flat[pl.ds(offset, rows, stride=stride), :] = value


def _load_head(ref, h):
    return _load_rows(ref, h, ref.shape[0], ref.shape[1])


def _store_head(ref, h, value):
    _store_rows(ref, h, ref.shape[0], ref.shape[1], value)


def _rmsnorm_kernel(x_ref, g_ref, o_ref):
    x = x_ref[...]
    ms = jnp.mean(x * x, axis=-1, keepdims=True)
    o_ref[...] = (x * lax.rsqrt(ms + EPS) * g_ref[...]).astype(o_ref.dtype)


def rmsnorm(x, g, out_dtype, tm=512):
    m, d = x.shape
    tm = min(tm, m)
    return pl.pallas_call(
        _rmsnorm_kernel,
        grid=(m // tm,),
        in_specs=[pl.BlockSpec((tm, d), lambda i: (i, 0)),
                  pl.BlockSpec((1, d), lambda i: (0, 0))],
        out_specs=pl.BlockSpec((tm, d), lambda i: (i, 0)),
        out_shape=jax.ShapeDtypeStruct((m, d), out_dtype),
        compiler_params=_params(1),
        name="rmsnorm",
    )(x, g.reshape(1, d))


def _bf16_weights(w_ref, wbf_ref):
    if wbf_ref is None:
        return w_ref

    @pl.when(pl.program_id(1) == 0)
    def _():
        wbf_ref[...] = w_ref[...].astype(BF16)

    return wbf_ref


CONV_COL_CHUNK = 256


def _tile_dot(a, w_ref, cols, w_transposed):
    if w_transposed:
        return _dot_nt(a, w_ref[cols, :])
    return jnp.dot(a, w_ref[:, cols], preferred_element_type=F32)


def _conv_silu_tiles(a_ref, w_ref, cw_ref, cb_ref, o_ref, tail_ref, pad_ref, *, seq_tiles, w_transposed):
    tm = a_ref.shape[0]
    first = SUBLANES - (CONV_WIDTH - 1)

    @pl.when(pl.program_id(1) % seq_tiles == 0)
    def _():
        pad_ref[0:SUBLANES, :] = jnp.zeros((SUBLANES, pad_ref.shape[1]), F32)

    a = a_ref[...]
    for c0 in range(0, o_ref.shape[1], CONV_COL_CHUNK):
        cols = slice(c0, c0 + CONV_COL_CHUNK)
        acc = _tile_dot(a, w_ref, cols, w_transposed)
        pad_ref[SUBLANES:SUBLANES + tm, cols] = acc
        conv = cb_ref[:, cols] + pad_ref[first:first + tm, cols] * cw_ref[0:1, cols]
        for i in range(1, CONV_WIDTH):
            conv = conv + pad_ref[first + i:first + i + tm, cols] * cw_ref[i:i + 1, cols]
        o_ref[:, cols] = _silu(conv)
        last_rows = acc[tm - SUBLANES:, :]
        pad_ref[0:SUBLANES, cols] = last_rows
        tail_ref[:, cols] = last_rows


def _write_epilogue(acc, epilogue, r_ref, o_ref):
    if epilogue == "residual":
        o_ref[...] = r_ref[...] + acc
    elif epilogue == "relu2":
        u = jnp.maximum(acc, 0.0)
        o_ref[...] = (u * u).astype(o_ref.dtype)
    elif epilogue == "heads":
        for j in range(o_ref.shape[1]):
            _store_head(o_ref, j, acc[:, j * LANES:(j + 1) * LANES])
    else:
        o_ref[...] = acc.astype(o_ref.dtype)


def _mm_kernel(*refs, epilogue, cast_w, seq_tiles, w_transposed, row_tiles, second):
    refs = list(refs)
    wbf_ref = pad_ref = None
    if epilogue == "conv_silu":
        pad_ref = refs.pop()
    if cast_w:
        wbf_ref = refs.pop()
    it = iter(refs)
    a_ref, w_ref = next(it), next(it)
    r_ref = next(it) if epilogue == "residual" else None
    cw_ref, cb_ref = (next(it), next(it)) if epilogue == "conv_silu" else (None, None)
    a2_ref = next(it) if second else None
    o_ref = next(it)
    tail_ref = next(it) if epilogue == "conv_silu" else None
    o2_ref = next(it) if second else None
    w_ref = _bf16_weights(w_ref, wbf_ref)

    def first_part():
        if epilogue == "conv_silu":
            _conv_silu_tiles(a_ref, w_ref, cw_ref, cb_ref, o_ref, tail_ref, pad_ref,
                             seq_tiles=seq_tiles, w_transposed=w_transposed)
        else:
            _write_epilogue(_tile_dot(a_ref[...], w_ref, slice(None), w_transposed), epilogue, r_ref, o_ref)

    def second_part():
        _write_epilogue(_tile_dot(a2_ref[...], w_ref, slice(None), w_transposed),
                        "none" if epilogue == "conv_silu" else epilogue, None, o2_ref)

    if second:
        pl.when(pl.program_id(1) < row_tiles)(first_part)
        pl.when(pl.program_id(1) == row_tiles)(second_part)
    else:
        first_part()


W_ROW_ALIGN = 64


def matmul(a, w, *, tm, tn, n=None, w_rows=None, out_dtype=F32, epilogue="none", residual=None,
           conv=None, seq=None, a2=None, name="matmul"):
    m, k = a.shape
    n = w.shape[1] if n is None else n
    tm = min(tm, m)
    tn = min(tn, n)
    assert m % tm == 0 and n % tn == 0
    row_tiles = m // tm
    second = a2 is not None
    assert not (second and epilogue == "residual"), "a second operand has no residual input"
    cast_w = w.dtype != BF16
    row = (lambda i: jnp.minimum(i, row_tiles - 1)) if second else (lambda i: i)
    if w_rows is None:
        w_spec = pl.BlockSpec((k, tn), lambda j, i: (0, j))
        w_tile = (k, tn)
    else:
        w_spec = pl.BlockSpec((pl.Element(tn), pl.Element(k)),
                              lambda j, i: (pl.multiple_of(w_rows(j), W_ROW_ALIGN), 0))
        w_tile = (tn, k)
    in_specs = [pl.BlockSpec((tm, k), lambda j, i: (row(i), 0)), w_spec]
    args = [a, w]
    scratch = [pltpu.VMEM(w_tile, BF16)] if cast_w else []
    heads = epilogue == "heads"

    def out_for(rows, tile_rows, row_index):
        if heads:
            return (pl.BlockSpec((tile_rows, tn // LANES, LANES), lambda j, i: (row_index(i), j, 0)),
                    jax.ShapeDtypeStruct((rows, n // LANES, LANES), out_dtype))
        return (pl.BlockSpec((tile_rows, tn), lambda j, i: (row_index(i), j)),
                jax.ShapeDtypeStruct((rows, n), out_dtype))

    spec, shape = out_for(m, tm, row)
    out_specs, out_shapes = [spec], [shape]
    seq_tiles = None
    if epilogue == "residual":
        in_specs.append(pl.BlockSpec((tm, tn), lambda j, i: (row(i), j)))
        args.append(residual)
    elif epilogue == "conv_silu":
        assert seq % tm == 0
        seq_tiles = seq // tm
        conv_w, conv_b = conv
        in_specs += [pl.BlockSpec((CONV_WIDTH, tn), lambda j, i: (0, j)),
                     pl.BlockSpec((1, tn), lambda j, i: (0, j))]
        args += [conv_w, conv_b.reshape(1, n)]
        out_specs.append(pl.BlockSpec((None, SUBLANES, tn), lambda j, i: (row(i) // seq_tiles, 0, j)))
        out_shapes.append(jax.ShapeDtypeStruct((m // seq, SUBLANES, n), F32))
        scratch.append(pltpu.VMEM((tm + SUBLANES, tn), F32))
    if second:
        m2 = a2.shape[0]
        in_specs.append(pl.BlockSpec((m2, k), lambda j, i: (0, 0)))
        args.append(a2)
        spec, shape = out_for(m2, m2, lambda i: 0)
        out_specs.append(spec)
        out_shapes.append(shape)
    outs = pl.pallas_call(
        functools.partial(_mm_kernel, epilogue=epilogue, cast_w=cast_w, seq_tiles=seq_tiles,
                          w_transposed=w_rows is not None, row_tiles=row_tiles, second=second),
        grid=(n // tn, row_tiles + (1 if second else 0)),
        in_specs=in_specs,
        out_specs=out_specs,
        out_shape=out_shapes,
        scratch_shapes=scratch,
        compiler_params=_params(2),
        name=name,
    )(*args)
    return outs[0] if len(outs) == 1 else tuple(outs)


def _merge_kernel(*refs, cast_w, row_tiles, second):
    refs = list(refs)
    wsbf_ref = wabf_ref = None
    if cast_w:
        wabf_ref = refs.pop()
        wsbf_ref = refs.pop()
    it = iter(refs)
    first_in = [next(it) for _ in range(4)]
    ws_ref, wa_ref = next(it), next(it)
    second_in = [next(it) for _ in range(4)] if second else None
    out_ref = next(it)
    out2_ref = next(it) if second else None
    ws_ref = _bf16_weights(ws_ref, wsbf_ref)
    wa_ref = _bf16_weights(wa_ref, wabf_ref)

    def part(y_ref, o_ref, gs_ref, ga_ref, dst_ref):
        ps = jnp.dot(y_ref[...], ws_ref[...], preferred_element_type=F32)
        pa = jnp.dot(o_ref[...], wa_ref[...], preferred_element_type=F32)
        dst_ref[...] = (_sigmoid(gs_ref[...]) * ps + _sigmoid(ga_ref[...]) * pa).astype(dst_ref.dtype)

    if second:
        pl.when(pl.program_id(1) < row_tiles)(lambda: part(*first_in, out_ref))
        pl.when(pl.program_id(1) == row_tiles)(lambda: part(*second_in, out2_ref))
    else:
        part(*first_in, out_ref)


def merge_proj(y_ssm, o_att, gates, w_ssm, w_att, *, tm, tn, second=None):
    m = y_ssm.shape[0]
    n = w_ssm.shape[1]
    tm = min(tm, m)
    n_col = n // tn
    row_tiles = m // tm
    row = (lambda i: jnp.minimum(i, row_tiles - 1)) if second else (lambda i: i)
    cast_w = w_ssm.dtype != BF16
    scratch = []
    if cast_w:
        scratch = [pltpu.VMEM((w_ssm.shape[0], tn), BF16), pltpu.VMEM((w_att.shape[0], tn), BF16)]

    def operand_specs(tile_rows, row_index):
        return [pl.BlockSpec((tile_rows, y_ssm.shape[1]), lambda j, i: (row_index(i), 0)),
                pl.BlockSpec((tile_rows, o_att.shape[1]), lambda j, i: (row_index(i), 0)),
                pl.BlockSpec((tile_rows, tn), lambda j, i: (row_index(i), j)),
                pl.BlockSpec((tile_rows, tn), lambda j, i: (row_index(i), j + n_col))]

    in_specs = operand_specs(tm, row) + [pl.BlockSpec((w_ssm.shape[0], tn), lambda j, i: (0, j)),
                                        pl.BlockSpec((w_att.shape[0], tn), lambda j, i: (0, j))]
    args = [y_ssm, o_att, gates, gates, w_ssm, w_att]
    out_specs = [pl.BlockSpec((tm, tn), lambda j, i: (row(i), j))]
    out_shapes = [jax.ShapeDtypeStruct((m, n), BF16)]
    if second:
        y2, o2, g2 = second
        m2 = y2.shape[0]
        in_specs += operand_specs(m2, lambda i: 0)
        args += [y2, o2, g2, g2]
        out_specs.append(pl.BlockSpec((m2, tn), lambda j, i: (0, j)))
        out_shapes.append(jax.ShapeDtypeStruct((m2, n), BF16))
    outs = pl.pallas_call(
        functools.partial(_merge_kernel, cast_w=cast_w, row_tiles=row_tiles, second=bool(second)),
        grid=(n_col, row_tiles + (1 if second else 0)),
        in_specs=in_specs,
        out_specs=out_specs,
        out_shape=out_shapes,
        scratch_shapes=scratch,
        compiler_params=_params(2),
        name="merge_proj",
    )(*args)
    return outs[0] if len(outs) == 1 else tuple(outs)


def _conv_sample_kernel(x0_ref, x1_ref, x2_ref, x3_ref, w_ref, b_ref, o_ref):
    acc = b_ref[...] + x0_ref[...] * w_ref[0:1, :]
    for i, x_ref in enumerate((x1_ref, x2_ref, x3_ref), start=1):
        acc = acc + x_ref[...] * w_ref[i:i + 1, :]
    o_ref[...] = _silu(acc)


def conv_sample(xpad, conv_w, conv_b, *, steps, tc=512):
    bsz = xpad.shape[0]
    c = conv_w.shape[1]
    nct = c // tc
    x_specs = [pl.BlockSpec((bsz, tc), functools.partial(lambda t, j, i: (0, (t + i) * nct + j), i=i))
               for i in range(CONV_WIDTH)]
    return pl.pallas_call(
        _conv_sample_kernel,
        grid=(steps, nct),
        in_specs=x_specs + [pl.BlockSpec((CONV_WIDTH, tc), lambda t, j: (0, j)),
                            pl.BlockSpec((1, tc), lambda t, j: (0, j))],
        out_specs=pl.BlockSpec((bsz, tc), lambda t, j: (0, t * nct + j)),
        out_shape=jax.ShapeDtypeStruct((bsz, steps * c), F32),
        compiler_params=_params(2),
        name="conv_sample",
    )(xpad, xpad, xpad, xpad, conv_w, conv_b.reshape(1, c))


def _split3(a):
    hi = a.astype(BF16)
    rest = a - hi.astype(F32)
    mid = rest.astype(BF16)
    return hi, mid, (rest - mid.astype(F32)).astype(BF16)


def _ssd_group(gl, x_ref, b_ref, c_ref, dtc_ref, dtr_ref, z_ref, pc_ref, pr_ref, nrm_ref, y_ref, h_ref,
               ht_ref, *, q_len, valid_len, last_chunk):
    nh, hd = HEADS_PER_SSM_GROUP, SSM_HEAD_DIM
    pair_w = 2 * hd
    gw = SSM_GROUP_WIDTH
    x_cols = slice(gl * gw, (gl + 1) * gw)
    n_cols = slice(gl * SSM_STATE, (gl + 1) * SSM_STATE)

    pc = pc_ref[gl]
    pr = pr_ref[gl]
    dt_c = _softplus(dtc_ref[gl] + pc[0:1, :])
    dt_r = _softplus(dtr_ref[gl] + pr[:, 0:1])
    if valid_len < q_len:
        dt_c = jnp.where(lax.broadcasted_iota(jnp.int32, dt_c.shape, 0) < valid_len, dt_c, 0.0)
        dt_r = jnp.where(lax.broadcasted_iota(jnp.int32, dt_r.shape, 1) < valid_len, dt_r, 0.0)
    a_c = dt_c * (-jnp.exp(pc[1:2, :]))
    a_r = dt_r * (-jnp.exp(pr[:, 1:2]))
    ti = lax.broadcasted_iota(jnp.int32, (q_len, q_len), 0)
    tj = lax.broadcasted_iota(jnp.int32, (q_len, q_len), 1)
    causal = ti >= tj
    tril = causal.astype(BF16)
    triu = (ti <= tj).astype(BF16)
    acum_c = sum(jnp.dot(tril, part, preferred_element_type=F32) for part in _split3(a_c))
    acum_r = sum(jnp.dot(part, triu, preferred_element_type=F32) for part in _split3(a_r))
    end_c = acum_c[q_len - 1:q_len, :]
    dd_c = jnp.exp(end_c - acum_c) * dt_c
    src_r = acum_r - jnp.log(dt_r)

    x = x_ref[:, x_cols]
    cmat = c_ref[:, n_cols].astype(BF16)
    if ht_ref is None:
        bmat = b_ref[:, n_cols].astype(BF16)
        cb = _dot_nt(cmat, bmat)
        h_prev = h_ref[gl * nh:(gl + 1) * nh]
        y_off_all = _dot_nt(cmat, h_prev.reshape(nh * hd, SSM_STATE).astype(BF16))
    else:
        b_t = jnp.transpose(b_ref[:, n_cols]).astype(BF16)
        cb = jnp.dot(cmat, b_t, preferred_element_type=F32)
        ht_prev = ht_ref[gl]
        y_off_all = jnp.dot(cmat, ht_prev.astype(BF16), preferred_element_type=F32)

    lane = lax.broadcasted_iota(jnp.int32, (q_len, pair_w), 1)
    first_half = lane < hd

    def pair_bcast(cols, j0):
        return jnp.where(first_half,
                         jnp.broadcast_to(cols[:, j0:j0 + 1], (q_len, pair_w)),
                         jnp.broadcast_to(cols[:, j0 + 1:j0 + 2], (q_len, pair_w)))

    yield

    y_diags = []
    for i in range(nh // 2):
        j0 = 2 * i
        x_pair = x[:, i * pair_w:(i + 1) * pair_w]
        y_diag = jnp.zeros((q_len, pair_w), F32)
        for j in (j0, j0 + 1):
            seg = jnp.broadcast_to(acum_c[:, j:j + 1], (q_len, q_len)) - src_r[j:j + 1, :]
            w = (cb * jnp.exp(jnp.where(causal, seg, NEG_INF))).astype(BF16)
            own = first_half if j == j0 else jnp.logical_not(first_half)
            y_diag = y_diag + jnp.dot(w, jnp.where(own, x_pair, 0.0).astype(BF16),
                                      preferred_element_type=F32)
        y_diags.append(y_diag)
    yield

    ys = []
    xws = []
    sumsq = jnp.zeros((q_len, 1), F32)
    for i in range(nh // 2):
        j0 = 2 * i
        x_pair = x[:, i * pair_w:(i + 1) * pair_w]
        e_pair = jnp.exp(pair_bcast(acum_c, j0))
        d_pair = jnp.where(first_half[0:1, :], pc[2:3, j0:j0 + 1], pc[2:3, j0 + 1:j0 + 2])
        y = y_diags[i] + y_off_all[:, i * pair_w:(i + 1) * pair_w] * e_pair + d_pair * x_pair
        y = y * _silu(z_ref[:, gl * gw + i * pair_w:gl * gw + (i + 1) * pair_w])
        sumsq = sumsq + jnp.sum(y * y, axis=-1, keepdims=True)
        ys.append(y)
        xws.append((x_pair * pair_bcast(dd_c, j0)).astype(BF16))

    if ht_ref is None:
        for i in range(nh // 2):
            s_pair = _dot_tn(xws[i], bmat)
            for jj, j in enumerate((2 * i, 2 * i + 1)):
                chunk_decay = jnp.exp(acum_r[j:j + 1, q_len - 1:q_len])
                h_ref[gl * nh + j] = h_prev[j] * chunk_decay + s_pair[jj * hd:(jj + 1) * hd, :]
    else:
        end_decay = jnp.exp(end_c)
        decay_row = jnp.concatenate(
            [jnp.where(first_half[0:1, :], end_decay[:, 2 * i:2 * i + 1], end_decay[:, 2 * i + 1:2 * i + 2])
             for i in range(nh // 2)], axis=1)
        ht_new = ht_prev * decay_row + jnp.dot(b_t, jnp.concatenate(xws, axis=1), preferred_element_type=F32)
        ht_ref[gl] = ht_new

        @pl.when(last_chunk)
        def _():
            for i in range(nh // 2):
                both = jnp.transpose(ht_new[:, i * pair_w:(i + 1) * pair_w])
                h_ref[gl * nh + 2 * i] = both[:hd, :]
                h_ref[gl * nh + 2 * i + 1] = both[hd:, :]

    inv = lax.rsqrt(sumsq * (1.0 / (nh * hd)) + EPS)
    for i in range(nh // 2):
        sl = slice(gl * gw + i * pair_w, gl * gw + (i + 1) * pair_w)
        y_ref[:, sl] = (ys[i] * inv * nrm_ref[:, sl]).astype(y_ref.dtype)


def _ssd_kernel(*refs, q_len, valid_len, has_h0, groups_per_step):
    if has_h0:
        h0_ref = refs[9]
        refs = refs[:9] + refs[10:] + (None,)
    h_ref, ht_ref = refs[-2:]
    chunk = pl.program_id(2)

    @pl.when(chunk == 0)
    def _():
        if has_h0:
            h_ref[...] = h0_ref[...]
        else:
            ht_ref[...] = jnp.zeros(ht_ref.shape, F32)

    groups = [_ssd_group(gl, *refs, q_len=q_len, valid_len=valid_len,
                         last_chunk=chunk == pl.num_programs(2) - 1) for gl in range(groups_per_step)]
    if has_h0:
        while groups:
            groups = [g for g in groups if next(g, StopIteration) is not StopIteration]
    else:
        for g in groups:
            for _ in g:
                pass


def ssd(u, dt_raw, z, dt_bias, a_log, d_skip, ssm_norm, h0, *, batch, seq, q_len, valid_len, groups_per_step):
    nh, g, gps = HEADS_PER_SSM_GROUP, SSM_GROUPS, groups_per_step
    nc = seq // q_len
    m = batch * seq
    dt4 = dt_raw[:, :SSM_HEADS].reshape(batch * nc, q_len, g, nh)
    dtc = jnp.transpose(dt4, (0, 2, 1, 3))
    dtr = jnp.transpose(dt4, (0, 2, 3, 1))
    pcol = jnp.stack([dt_bias, a_log, d_skip]).reshape(3, g, nh).transpose(1, 0, 2)
    prow = jnp.transpose(pcol, (0, 2, 1))
    b_off = D_INNER // (gps * SSM_STATE)
    c_off = b_off + g // gps
    row = lambda b, gi, c: b * nc + c
    in_specs = [
        pl.BlockSpec((q_len, gps * SSM_GROUP_WIDTH), lambda b, gi, c: (row(b, gi, c), gi)),
        pl.BlockSpec((q_len, gps * SSM_STATE), lambda b, gi, c: (row(b, gi, c), b_off + gi)),
        pl.BlockSpec((q_len, gps * SSM_STATE), lambda b, gi, c: (row(b, gi, c), c_off + gi)),
        pl.BlockSpec((None, gps, q_len, nh), lambda b, gi, c: (row(b, gi, c), gi, 0, 0)),
        pl.BlockSpec((None, gps, nh, q_len), lambda b, gi, c: (row(b, gi, c), gi, 0, 0)),
        pl.BlockSpec((q_len, gps * SSM_GROUP_WIDTH), lambda b, gi, c: (row(b, gi, c), gi)),
        pl.BlockSpec((gps, 3, nh), lambda b, gi, c: (gi, 0, 0)),
        pl.BlockSpec((gps, nh, 3), lambda b, gi, c: (gi, 0, 0)),
        pl.BlockSpec((1, gps * SSM_GROUP_WIDTH), lambda b, gi, c: (0, gi)),
    ]
    args = [u, u, u, dtc, dtr, z, pcol, prow, ssm_norm.reshape(1, D_INNER)]
    h_spec = pl.BlockSpec((None, gps * nh, SSM_HEAD_DIM, SSM_STATE), lambda b, gi, c: (b, gi, 0, 0))
    scratch = []
    if h0 is not None:
        in_specs.append(h_spec)
        args.append(h0)
    else:
        scratch.append(pltpu.VMEM((gps, SSM_STATE, SSM_GROUP_WIDTH), F32))
    return pl.pallas_call(
        functools.partial(_ssd_kernel, q_len=q_len, valid_len=valid_len, has_h0=h0 is not None,
                          groups_per_step=gps),
        grid=(batch, g // gps, nc),
        in_specs=in_specs,
        out_specs=[pl.BlockSpec((q_len, gps * SSM_GROUP_WIDTH), lambda b, gi, c: (row(b, gi, c), gi)), h_spec],
        out_shape=[jax.ShapeDtypeStruct((m, D_INNER), BF16),
                   jax.ShapeDtypeStruct((batch, SSM_HEADS, SSM_HEAD_DIM, SSM_STATE), F32)],
        scratch_shapes=scratch,
        compiler_params=_params(3),
        name="ssd",
    )(*args)


def _attn_prompt_kernel(*refs, kb, has_prev):
    if has_prev:
        q_ref, kc_ref, vc_ref, mb_ref, o_ref, l_ref, kp_ref, vp_ref = refs
    else:
        q_ref, kc_ref, vc_ref, mb_ref, o_ref, l_ref = refs
    not_first = pl.program_id(2) > 0
    scale = ATT_HEAD_DIM ** -0.5
    heads = range(ATT_HEADS_PER_GROUP)
    if has_prev:
        @pl.when(pl.program_id(2) == 0)
        def _():
            kp_ref[...] = jnp.zeros(kp_ref.shape, BF16)
            vp_ref[...] = jnp.zeros(vp_ref.shape, BF16)

    scores = []
    for h in heads:
        q = (_load_head(q_ref, h) * scale).astype(BF16)
        k_cur = _load_head(kc_ref, h).astype(BF16)
        s_c = _dot_nt(q, k_cur) + mb_ref[h, :, kb:]
        s_p = None
        if has_prev:
            s_p = _dot_nt(q, kp_ref[h]) + mb_ref[h, :, :kb]
            s_p = jnp.where(not_first, s_p, NEG_INF)
            kp_ref[h] = k_cur
        scores.append((s_c, s_p))
    probs = []
    for s_c, s_p in scores:
        m = jnp.max(s_c, axis=-1, keepdims=True)
        if has_prev:
            m = jnp.maximum(m, jnp.max(s_p, axis=-1, keepdims=True))
        p_c = jnp.exp(s_c - m)
        den = jnp.sum(p_c, axis=-1, keepdims=True)
        p_p = None
        if has_prev:
            p_p = jnp.exp(s_p - m)
            den = den + jnp.sum(p_p, axis=-1, keepdims=True)
            p_p = p_p.astype(BF16)
        probs.append((p_c.astype(BF16), p_p, den, m + jnp.log(den)))
    for h, (p_c, p_p, den, lse) in zip(heads, probs):
        v_cur = _load_head(vc_ref, h).astype(BF16)
        o = jnp.dot(p_c, v_cur, preferred_element_type=F32)
        if has_prev:
            o = o + jnp.dot(p_p, vp_ref[h], preferred_element_type=F32)
            vp_ref[h] = v_cur
        _store_head(o_ref, h, o / den)
        _store_head(l_ref, h, jnp.broadcast_to(lse, (kb, ATT_HEAD_DIM)))


def attn_prompt(q, kv, mbias, *, group, batch, seq, dil, kb):
    m_res = seq // dil
    nb = m_res // kb
    has_prev = nb > 1
    nh = ATT_HEADS_PER_GROUP
    q4 = q.reshape(batch * m_res, dil, q.shape[1], ATT_HEAD_DIM)
    kv4 = kv.reshape(batch * m_res, dil, 2 * nh, ATT_HEAD_DIM)

    def cur(head_block):
        return pl.BlockSpec((kb, None, nh, ATT_HEAD_DIM), lambda b, r, n: (b * nb + n, r, head_block, 0))

    carry = [pltpu.VMEM((nh, kb, ATT_HEAD_DIM), BF16)] * 2 if has_prev else []
    o, lse = pl.pallas_call(
        functools.partial(_attn_prompt_kernel, kb=kb, has_prev=has_prev),
        grid=(batch, dil, nb),
        in_specs=[cur(group), cur(0), cur(1), pl.BlockSpec(mbias.shape, lambda b, r, n: (0, 0, 0))],
        out_specs=[cur(0), cur(0)],
        out_shape=[jax.ShapeDtypeStruct((batch * m_res, dil, nh, ATT_HEAD_DIM), F32)] * 2,
        scratch_shapes=carry,
        compiler_params=_params(3),
        name=f"attn_prompt_g{group}",
    )(q4, kv4, kv4, mbias)
    return o.reshape(batch * seq, nh, ATT_HEAD_DIM), lse.reshape(batch * seq, nh, ATT_HEAD_DIM)


def _attn_sample_kernel(*refs, n_sets, steps, q_rows):
    q_ref, new_ref = refs[:2]
    cache_refs = refs[2:2 + n_sets]
    bias_ref, o_ref, l_ref = refs[2 + n_sets:]
    gw = ATT_GROUP_WIDTH
    nh = ATT_HEADS_PER_GROUP
    hd = ATT_HEAD_DIM
    scale = ATT_HEAD_DIM ** -0.5
    bb = q_ref.shape[0]
    cache_rows = cache_refs[0].shape[1]
    new_rows = new_ref.shape[1]
    row_stride = 2 * nh
    per_set = steps // n_sets
    sub = lax.broadcasted_iota(jnp.int32, (nh, gw), 0)
    own = sub == lax.broadcasted_iota(jnp.int32, (nh, gw), 1) // hd
    sets = [(bi, ci) for bi in range(bb) for ci in range(n_sets)]

    def keys_or_values(bi, ci, first_head):
        base = bi * cache_rows * row_stride
        return jnp.concatenate(
            [jnp.concatenate([_load_rows(cache_refs[ci], base + first_head + h, cache_rows, row_stride),
                              _load_rows(new_ref, bi * new_rows * 2 * nh + first_head + h, new_rows, 2 * nh)], axis=0)
             for h in range(nh)], axis=1).astype(BF16)

    scores = []
    for bi, ci in sets:
        q_blocks = []
        for tt in range(per_set):
            q_t = q_ref[bi, ci * per_set + tt] * scale
            q_blocks.append(jnp.where(own, jnp.concatenate([q_t] * nh, axis=1), 0.0))
        if q_rows > per_set * nh:
            q_blocks.append(jnp.zeros((q_rows - per_set * nh, gw), F32))
        qbd = jnp.concatenate(q_blocks, axis=0).astype(BF16)
        scores.append(_dot_nt(qbd, keys_or_values(bi, ci, 0)) + bias_ref[ci])
    probs = []
    for s in scores:
        m = jnp.max(s, axis=-1, keepdims=True)
        p = jnp.exp(s - m)
        den = jnp.sum(p, axis=-1, keepdims=True)
        probs.append((p.astype(BF16), den, m + jnp.log(den)))
    for (bi, ci), (p, den, lse) in zip(sets, probs):
        o_all = jnp.dot(p, keys_or_values(bi, ci, nh), preferred_element_type=F32) / den
        for tt in range(per_set):
            t = ci * per_set + tt
            rows = slice(tt * nh, (tt + 1) * nh)
            o_t = jnp.zeros((nh, hd), F32)
            for h in range(nh):
                o_t = o_t + jnp.where(sub[:, :hd] == h, o_all[rows, h * hd:(h + 1) * hd], 0.0)
            o_ref[bi, t] = o_t
            l_ref[bi, t] = jnp.broadcast_to(lse[rows, :], (nh, hd))


def attn_sample(q, kv_new_pad, cache, bias, *, group, batch, steps, dil, batch_block):
    nh, hd = ATT_HEADS_PER_GROUP, ATT_HEAD_DIM
    bb = batch_block
    n_sets, q_rows, n_keys_tot = bias.shape
    new_rows = kv_new_pad.shape[1]
    cache_rows = n_keys_tot - new_rows
    q4 = q.reshape(batch, steps, q.shape[1], hd)
    if n_sets == 1:
        cache_view = cache.reshape(batch, cache_rows, 2 * nh, hd)
        cache_specs = [pl.BlockSpec((bb, cache_rows, 2 * nh, hd), lambda b: (b, 0, 0, 0))]
    else:
        cache_view = cache.reshape(batch, cache_rows, dil, 2 * nh, hd)
        cache_specs = [pl.BlockSpec((bb, cache_rows, None, 2 * nh, hd),
                                    functools.partial(lambda b, t: (b, 0, t, 0, 0), t=t))
                       for t in range(n_sets)]
    out_spec = pl.BlockSpec((bb, steps, nh, hd), lambda b: (b, 0, 0, 0))
    o, lse = pl.pallas_call(
        functools.partial(_attn_sample_kernel, n_sets=n_sets, steps=steps, q_rows=q_rows),
        grid=(batch // bb,),
        in_specs=[pl.BlockSpec((bb, steps, nh, hd), lambda b: (b, 0, group, 0)),
                  pl.BlockSpec((bb, new_rows, 2 * nh, hd), lambda b: (b, 0, 0, 0))]
                 + cache_specs
                 + [pl.BlockSpec(bias.shape, lambda b: (0, 0, 0))],
        out_specs=[out_spec, out_spec],
        out_shape=[jax.ShapeDtypeStruct((batch, steps, nh, hd), F32)] * 2,
        compiler_params=_params(1),
        name=f"attn_sample_g{group}",
    )(q4, kv_new_pad, *([cache_view] * n_sets), bias)
    return o.reshape(batch * steps, nh, hd), lse.reshape(batch * steps, nh, hd)


def _attn_combine_kernel(o0, o1, o2, l0, l1, l2, out_ref, mix_ref):
    ls = (l0[...], l1[...], l2[...])
    m = jnp.maximum(jnp.maximum(ls[0], ls[1]), ls[2])
    ws = [jnp.exp(l - m) for l in ls]
    num = ws[0] * o0[...] + ws[1] * o1[...] + ws[2] * o2[...]
    mix_ref[...] = num / (ws[0] + ws[1] + ws[2])
    for h in range(ATT_HEADS_PER_GROUP):
        out_ref[:, h * ATT_HEAD_DIM:(h + 1) * ATT_HEAD_DIM] = _load_head(mix_ref, h).astype(out_ref.dtype)


def attn_combine(outs, lses, tm=512):
    m, nh, hd = outs[0].shape
    tm = min(tm, m)
    spec = pl.BlockSpec((tm, nh, hd), lambda i: (i, 0, 0))
    return pl.pallas_call(
        _attn_combine_kernel,
        grid=(m // tm,),
        in_specs=[spec] * 6,
        out_specs=pl.BlockSpec((tm, nh * hd), lambda i: (i, 0)),
        out_shape=jax.ShapeDtypeStruct((m, nh * hd), BF16),
        scratch_shapes=[pltpu.VMEM((tm, nh, hd), F32)],
        compiler_params=_params(1),
        name="attn_combine",
    )(*outs, *lses)


def _t5_bucket(dist):
    max_exact = N_BUCKETS // 2
    d = np.maximum(dist, max_exact).astype(np.float32)
    large = max_exact + (np.log(d / max_exact) / np.log(MAX_DISTANCE / max_exact)
                         * (N_BUCKETS - max_exact)).astype(np.int32)
    large = np.minimum(large, N_BUCKETS - 1)
    return np.where(dist < max_exact, dist, large).astype(np.int32)


def _group_bias(rel_bias, g):
    win, dil = ATT_PATTERNS[g]
    buckets = _t5_bucket(np.arange(win // dil + 1) * dil)
    return rel_bias[buckets][:, g * ATT_HEADS_PER_GROUP:(g + 1) * ATT_HEADS_PER_GROUP]


def _prompt_mask_bias(bias_g, n_keys, kb):
    period = 3 * kb
    offs = np.arange(period)
    offs = np.where(offs >= 2 * kb, offs - period, offs)
    j_of = np.clip(kb - offs, 0, n_keys)
    a = jnp.transpose(bias_g[j_of]).astype(F32)
    nh = a.shape[0]
    tiled = jnp.tile(a, (1, kb))[:, :kb * (period - 1)].reshape(nh, kb, period - 1)
    rel = np.arange(kb)[:, None] + kb - np.arange(2 * kb)[None, :]
    valid = (rel >= 0) & (rel <= n_keys)
    return jnp.where(valid[None], tiled[:, :, :2 * kb], NEG_INF)


def _sample_mask_bias(bias_g, *, cache_len, dil, n_keys, steps, new_rows, shared_cache, q_rows):
    nh = ATT_HEADS_PER_GROUP
    n_cache = 1 if shared_cache else steps
    per_cache = steps // n_cache
    rows_per_cache = cache_len if shared_cache else cache_len // dil
    blocks = []
    for ci in range(n_cache):
        if shared_cache:
            cache_pos = np.arange(rows_per_cache)
        else:
            cache_pos = ci + dil * np.arange(rows_per_cache)
        key_pos = np.concatenate([cache_pos, cache_len + np.arange(new_rows)])
        key_live = np.concatenate([np.ones(rows_per_cache, bool), np.arange(new_rows) < steps])
        rows = []
        for tt in range(per_cache):
            t = ci * per_cache + tt
            diff = cache_len + t - key_pos
            valid = key_live & (diff >= 0) & (diff % dil == 0) & (diff // dil <= n_keys)
            j = np.clip(diff // dil, 0, n_keys)
            rows.append(jnp.where(valid[None, :], bias_g[j].T.astype(F32), NEG_INF))
        blk = jnp.concatenate(rows, axis=0)
        pad = q_rows - per_cache * nh
        if pad:
            blk = jnp.concatenate([blk, jnp.zeros((pad, blk.shape[1]), F32)], axis=0)
        blocks.append(blk)
    return jnp.stack(blocks)


SAMPLE_NEW_ROWS = 16
SAMPLE_Q_ROWS_MIN = 16
SSD_SAMPLE_Q = 16
SAMPLE_SHARED_BATCH_BLOCK = 8
SAMPLE_DILATED_BATCH_BLOCK = 4
SSD_PROMPT_GROUPS_PER_STEP = 8


IN_SIZES = (D_INNER, CONV_DIM, SSM_HEADS, ATT_WIDTH, ATT_WIDTH, ATT_WIDTH, D_MODEL, D_MODEL)
IN_OFFSETS = tuple(int(o) for o in np.concatenate([[0], np.cumsum(IN_SIZES)]))


IN_TILE = 1024
assert all(o % W_ROW_ALIGN == 0 for o in IN_OFFSETS)


def _in_rows(seg, stride=IN_TILE, first=0):
    return lambda j: IN_OFFSETS[seg] + first + j * stride


PROMPT_ROW_TILE = 1024


def _ssd_branches(u_p, u_s, dt_p, dt_s, z_p, z_s, w, ssm_state, *, bp, sp, bs, ss):
    params = (w["dt_bias"], w["a_log"], w["d_skip"], w["ssm_norm"])
    y_p, ssm_p = ssd(u_p, dt_p, z_p, *params, None, batch=bp, seq=sp, q_len=SSD_CHUNK,
                     valid_len=SSD_CHUNK, groups_per_step=SSD_PROMPT_GROUPS_PER_STEP)
    qp = SSD_SAMPLE_Q
    pad_rows = lambda a: jnp.pad(a.reshape(bs, ss, a.shape[-1]),
                                 ((0, 0), (0, qp - ss), (0, 0))).reshape(bs * qp, a.shape[-1])
    y_pad, ssm_s = ssd(pad_rows(u_s), pad_rows(dt_s), pad_rows(z_s), *params, ssm_state,
                       batch=bs, seq=qp, q_len=qp, valid_len=ss, groups_per_step=SSM_GROUPS)
    y_s = y_pad.reshape(bs, qp, D_INNER)[:, :ss].reshape(bs * ss, D_INNER)
    return y_p, y_s, ssm_p, ssm_s


def _attention_branches(q_p, q_s, kvs_p, kvs_s, caches, rel_bias, *, bp, sp, bs, ss):
    nh, hd = ATT_HEADS_PER_GROUP, ATT_HEAD_DIM
    outs_p, lses_p, outs_s, lses_s = [], [], [], []
    for g, (win, dil) in enumerate(ATT_PATTERNS):
        n_keys = win // dil
        bias_g = _group_bias(rel_bias, g)
        mb = _prompt_mask_bias(bias_g, n_keys, n_keys)
        o, lse = attn_prompt(q_p, kvs_p[g], mb, group=g, batch=bp, seq=sp, dil=dil, kb=n_keys)
        outs_p.append(o)
        lses_p.append(lse)

        cache = caches[g]
        cache_len = cache.shape[1]
        shared = dil == 1
        q_rows = max(SAMPLE_Q_ROWS_MIN, (ss if shared else 1) * nh)
        sb = _sample_mask_bias(bias_g, cache_len=cache_len, dil=dil, n_keys=n_keys, steps=ss,
                               new_rows=SAMPLE_NEW_ROWS, shared_cache=shared, q_rows=q_rows)
        kv_new = jnp.pad(kvs_s[g].reshape(bs, ss, 2 * nh, hd),
                         ((0, 0), (0, SAMPLE_NEW_ROWS - ss), (0, 0), (0, 0)))
        o, lse = attn_sample(q_s, kv_new, cache, sb, group=g, batch=bs, steps=ss, dil=dil,
                             batch_block=SAMPLE_SHARED_BATCH_BLOCK if shared else SAMPLE_DILATED_BATCH_BLOCK)
        outs_s.append(o)
        lses_s.append(lse)
    return attn_combine(outs_p, lses_p), attn_combine(outs_s, lses_s)


def _forward(x_p, x_s, w, caches, conv_state, ssm_state, rel_bias, *, bp, sp, bs, ss):
    nh, hd = ATT_HEADS_PER_GROUP, ATT_HEAD_DIM
    tm = PROMPT_ROW_TILE
    h_p = rmsnorm(x_p, w["norm_mix"], BF16)
    h_s = rmsnorm(x_s, w["norm_mix"], BF16)
    proj = lambda name, **kw: matmul(h_p, w["in_t"], a2=h_s, tm=tm, tn=IN_TILE, name=name, **kw)
    z_p, z_s = proj("in_z", n=IN_SIZES[0], w_rows=_in_rows(0))
    dt_p, dt_s = matmul(h_p, w["in_t"], a2=h_s, tm=tm, tn=LANES, n=LANES, w_rows=_in_rows(2), name="in_dt")
    q_p, q_s = proj("in_q", n=IN_SIZES[3], w_rows=_in_rows(3), epilogue="heads")
    kvs = [proj(f"in_kv{g}", n=2 * ATT_GROUP_WIDTH, epilogue="heads",
                w_rows=_in_rows(4, stride=IN_OFFSETS[5] - IN_OFFSETS[4], first=g * ATT_GROUP_WIDTH))
           for g in range(ATT_N_GROUPS)]
    kvs_p, kvs_s = [kv[0] for kv in kvs], [kv[1] for kv in kvs]
    gates_p, gates_s = proj("in_gates", n=IN_SIZES[6] + IN_SIZES[7], w_rows=_in_rows(6))
    u_p, xbc_tail = matmul(h_p, w["in_t"], tm=tm, tn=IN_TILE, n=IN_SIZES[1], w_rows=_in_rows(1), name="in_xbc_conv",
                           epilogue="conv_silu", conv=(w["conv_w"], w["conv_b"]), seq=sp)
    xbc_s = matmul(h_s, w["in_t"], tm=tm, tn=IN_TILE, n=IN_SIZES[1], w_rows=_in_rows(1), name="in_xbc")
    conv_p = xbc_tail[:, SUBLANES - (CONV_WIDTH - 1):]
    xpad = jnp.concatenate([conv_state.reshape(bs, (CONV_WIDTH - 1) * CONV_DIM),
                            xbc_s.reshape(bs, ss * CONV_DIM)], axis=1)
    u_s = conv_sample(xpad, w["conv_w"], w["conv_b"], steps=ss).reshape(bs * ss, CONV_DIM)
    conv_s = xpad.reshape(bs, CONV_WIDTH - 1 + ss, CONV_DIM)[:, ss:]

    y_p, y_s, ssm_p, ssm_s = _ssd_branches(u_p, u_s, dt_p, dt_s, z_p, z_s, w, ssm_state,
                                           bp=bp, sp=sp, bs=bs, ss=ss)
    o_p, o_s = _attention_branches(q_p, q_s, kvs_p, kvs_s, caches, rel_bias, bp=bp, sp=sp, bs=bs, ss=ss)

    merged_p = merge_proj(y_p, o_p, gates_p, w["ssm_proj"], w["att_proj"], tm=tm, tn=512)
    merged_s = merge_proj(y_s, o_s, gates_s, w["ssm_proj"], w["att_proj"], tm=tm, tn=512)
    out_proj = lambda a, res: matmul(a, w["out"], tm=tm, tn=1024, epilogue="residual", residual=res, name="out_proj")
    x1_p, x1_s = out_proj(merged_p, x_p), out_proj(merged_s, x_s)
    h2_p = rmsnorm(x1_p, w["norm_mlp"], BF16)
    h2_s = rmsnorm(x1_s, w["norm_mlp"], BF16)
    mlp_up = lambda a: matmul(a, w["up"], tm=tm, tn=1024, out_dtype=BF16, epilogue="relu2", name="mlp_up")
    up_p, up_s = mlp_up(h2_p), mlp_up(h2_s)
    mlp_down = lambda a, res: matmul(a, w["down"], tm=512, tn=512, epilogue="residual", residual=res,
                                     name="mlp_down")
    x2_p, x2_s = mlp_down(up_p, x1_p), mlp_down(up_s, x1_s)
    y_out_p = rmsnorm(x2_p, w["norm_final"], F32)
    y_out_s = rmsnorm(x2_s, w["norm_final"], F32)

    rows_p = [kv.reshape(bp, sp, 2, nh, hd)[:, sp - min(win, sp):] for kv, (win, _) in zip(kvs_p, ATT_PATTERNS)]
    rows_s = [kv.reshape(bs, ss, 2, nh, hd) for kv in kvs_s]
    return (y_out_p, rows_p, conv_p, ssm_p), (y_out_s, rows_s, conv_s, ssm_s)


def kernel(x_prompt, x_sample, cache_win128, cache_win512, cache_win2048, state_conv, state_ssm, w_in, conv_w, conv_b, dt_bias, a_log, d_skip, ssm_norm, w_ssm_proj, w_att_proj, w_out, norm_mix, w_up, w_down, norm_mlp, rel_bias, norm_final):
    assert w_in.shape[0] == 1, "single-layer model"
    bp, sp, _ = x_prompt.shape
    bs, ss, _ = x_sample.shape
    w = dict(
        in_t=jnp.swapaxes(w_in[0], 0, 1),
        conv_w=conv_w[0], conv_b=conv_b[0], dt_bias=dt_bias[0], a_log=a_log[0], d_skip=d_skip[0],
        ssm_norm=ssm_norm[0], ssm_proj=w_ssm_proj[0].astype(BF16), att_proj=w_att_proj[0].astype(BF16),
        out=w_out[0], norm_mix=norm_mix[0], up=w_up[0],
        down=w_down[0].astype(BF16), norm_mlp=norm_mlp[0], norm_final=norm_final)

    (yp, rows_p, conv_p, ssm_p), (ys, rows_s, conv_s, ssm_s) = _forward(
        x_prompt.reshape(bp * sp, D_MODEL), x_sample.reshape(bs * ss, D_MODEL), w,
        (cache_win128[0], cache_win512[0], cache_win2048[0]), state_conv[0], state_ssm[0], rel_bias,
        bp=bp, sp=sp, bs=bs, ss=ss)

    return (yp.reshape(bp, sp, D_MODEL), ys.reshape(bs, ss, D_MODEL),
            rows_p[0][None], rows_p[1][None], rows_p[2][None], conv_p[None], ssm_p[None],
            rows_s[0][None], rows_s[1][None], rows_s[2][None], conv_s[None], ssm_s[None])
```

```python
import functools
import math

import numpy as np
import jax
import jax.numpy as jnp
from jax import lax
from jax.experimental import pallas as pl
from jax.experimental.pallas import tpu as pltpu

D_MODEL = 2048
D_INNER = 2 * D_MODEL
SSM_HEAD_DIM = 64
SSM_HEADS = D_INNER // SSM_HEAD_DIM
SSM_GROUPS = 8
HEADS_PER_SSM_GROUP = SSM_HEADS // SSM_GROUPS
SSM_STATE = 128
SSM_GROUP_WIDTH = D_INNER // SSM_GROUPS
CONV_WIDTH = 4
CONV_DIM = D_INNER + 2 * SSM_GROUPS * SSM_STATE
SSD_CHUNK = 128
ATT_HEAD_DIM = 128
ATT_HEADS_PER_GROUP = 8
ATT_PATTERNS = ((128, 1), (512, 4), (2048, 16))
ATT_N_GROUPS = len(ATT_PATTERNS)
ATT_GROUP_WIDTH = ATT_HEADS_PER_GROUP * ATT_HEAD_DIM
ATT_WIDTH = ATT_N_GROUPS * ATT_GROUP_WIDTH
N_BUCKETS = 32
MAX_DISTANCE = max(w for w, _ in ATT_PATTERNS)
EPS = 1e-6

LANES = 128
SUBLANES = 8
VMEM_LIMIT_BYTES = 56 * 1024 * 1024

F32 = jnp.float32
BF16 = jnp.bfloat16
NEG_INF = float("-inf")


def _params(n_grid_dims):
    return pltpu.CompilerParams(
        dimension_semantics=("arbitrary",) * n_grid_dims,
        vmem_limit_bytes=VMEM_LIMIT_BYTES)


def _sigmoid(x):
    return 0.5 * jnp.tanh(0.5 * x) + 0.5


def _silu(x):
    return x * _sigmoid(x)


def _softplus(x):
    return jnp.maximum(x, 0.0) + jnp.log1p(jnp.exp(-jnp.abs(x)))


def _dot_nt(a, b):
    return lax.dot_general(a, b, (((1,), (1,)), ((), ())), preferred_element_type=F32)


def _dot_tn(a, b):
    return lax.dot_general(a, b, (((0,), (0,)), ((), ())), preferred_element_type=F32)


def _load_rows(ref, offset, rows, stride):
    flat = ref.reshape(math.prod(ref.shape[:-1]), ref.shape[-1])
    return flat[pl.ds(offset, rows, stride=stride), :]


def _store_rows(ref, offset, rows, stride, value):
    flat = ref.reshape(math.prod(ref.shape[:-1]), ref.shape[-1])
    flat[pl.ds(offset, rows, stride=stride), :] = value


def _load_head(ref, h):
    return _load_rows(ref, h, ref.shape[0], ref.shape[1])


def _store_head(ref, h, value):
    _store_rows(ref, h, ref.shape[0], ref.shape[1], value)


def _rmsnorm_kernel(x_ref, g_ref, o_ref):
    x = x_ref[...]
    ms = jnp.mean(x * x, axis=-1, keepdims=True)
    o_ref[...] = (x * lax.rsqrt(ms + EPS) * g_ref[...]).astype(o_ref.dtype)


def rmsnorm(x, g, out_dtype, tm=512):
    m, d = x.shape
    tm = min(tm, m)
    return pl.pallas_call(
        _rmsnorm_kernel,
        grid=(m // tm,),
        in_specs=[pl.BlockSpec((tm, d), lambda i: (i, 0)),
                  pl.BlockSpec((1, d), lambda i: (0, 0))],
        out_specs=pl.BlockSpec((tm, d), lambda i: (i, 0)),
        out_shape=jax.ShapeDtypeStruct((m, d), out_dtype),
        compiler_params=_params(1),
        name="rmsnorm",
    )(x, g.reshape(1, d))


def _bf16_weights(w_ref, wbf_ref):
    if wbf_ref is None:
        return w_ref

    @pl.when(pl.program_id(1) == 0)
    def _():
        wbf_ref[...] = w_ref[...].astype(BF16)

    return wbf_ref


CONV_COL_CHUNK = 256


def _tile_dot(a, w_ref, cols, w_transposed):
    if w_transposed:
        return _dot_nt(a, w_ref[cols, :])
    return jnp.dot(a, w_ref[:, cols], preferred_element_type=F32)


def _conv_silu_tiles(a_ref, w_ref, cw_ref, cb_ref, o_ref, tail_ref, pad_ref, *, seq_tiles, w_transposed):
    tm = a_ref.shape[0]
    first = SUBLANES - (CONV_WIDTH - 1)

    @pl.when(pl.program_id(1) % seq_tiles == 0)
    def _():
        pad_ref[0:SUBLANES, :] = jnp.zeros((SUBLANES, pad_ref.shape[1]), F32)

    a = a_ref[...]
    for c0 in range(0, o_ref.shape[1], CONV_COL_CHUNK):
        cols = slice(c0, c0 + CONV_COL_CHUNK)
        acc = _tile_dot(a, w_ref, cols, w_transposed)
        pad_ref[SUBLANES:SUBLANES + tm, cols] = acc
        conv = cb_ref[:, cols] + pad_ref[first:first + tm, cols] * cw_ref[0:1, cols]
        for i in range(1, CONV_WIDTH):
            conv = conv + pad_ref[first + i:first + i + tm, cols] * cw_ref[i:i + 1, cols]
        o_ref[:, cols] = _silu(conv)
        last_rows = acc[tm - SUBLANES:, :]
        pad_ref[0:SUBLANES, cols] = last_rows
        tail_ref[:, cols] = last_rows


def _write_epilogue(acc, epilogue, r_ref, o_ref):
    if epilogue == "residual":
        o_ref[...] = r_ref[...] + acc
    elif epilogue == "relu2":
        u = jnp.maximum(acc, 0.0)
        o_ref[...] = (u * u).astype(o_ref.dtype)
    elif epilogue == "heads":
        for j in range(o_ref.shape[1]):
            _store_head(o_ref, j, acc[:, j * LANES:(j + 1) * LANES])
    else:
        o_ref[...] = acc.astype(o_ref.dtype)


def _mm_kernel(*refs, epilogue, cast_w, seq_tiles, w_transposed, row_tiles, second):
    refs = list(refs)
    wbf_ref = pad_ref = None
    if epilogue == "conv_silu":
        pad_ref = refs.pop()
    if cast_w:
        wbf_ref = refs.pop()
    it = iter(refs)
    a_ref, w_ref = next(it), next(it)
    r_ref = next(it) if epilogue == "residual" else None
    cw_ref, cb_ref = (next(it), next(it)) if epilogue == "conv_silu" else (None, None)
    a2_ref = next(it) if second else None
    o_ref = next(it)
    tail_ref = next(it) if epilogue == "conv_silu" else None
    o2_ref = next(it) if second else None
    w_ref = _bf16_weights(w_ref, wbf_ref)

    def first_part():
        if epilogue == "conv_silu":
            _conv_silu_tiles(a_ref, w_ref, cw_ref, cb_ref, o_ref, tail_ref, pad_ref,
                             seq_tiles=seq_tiles, w_transposed=w_transposed)
        else:
            _write_epilogue(_tile_dot(a_ref[...], w_ref, slice(None), w_transposed), epilogue, r_ref, o_ref)

    def second_part():
        _write_epilogue(_tile_dot(a2_ref[...], w_ref, slice(None), w_transposed),
                        "none" if epilogue == "conv_silu" else epilogue, None, o2_ref)

    if second:
        pl.when(pl.program_id(1) < row_tiles)(first_part)
        pl.when(pl.program_id(1) == row_tiles)(second_part)
    else:
        first_part()


W_ROW_ALIGN = 64


def matmul(a, w, *, tm, tn, n=None, w_rows=None, out_dtype=F32, epilogue="none", residual=None,
           conv=None, seq=None, a2=None, name="matmul"):
    m, k = a.shape
    n = w.shape[1] if n is None else n
    tm = min(tm, m)
    tn = min(tn, n)
    assert m % tm == 0 and n % tn == 0
    row_tiles = m // tm
    second = a2 is not None
    assert not (second and epilogue == "residual"), "a second operand has no residual input"
    cast_w = w.dtype != BF16
    row = (lambda i: jnp.minimum(i, row_tiles - 1)) if second else (lambda i: i)
    if w_rows is None:
        w_spec = pl.BlockSpec((k, tn), lambda j, i: (0, j))
        w_tile = (k, tn)
    else:
        w_spec = pl.BlockSpec((pl.Element(tn), pl.Element(k)),
                              lambda j, i: (pl.multiple_of(w_rows(j), W_ROW_ALIGN), 0))
        w_tile = (tn, k)
    in_specs = [pl.BlockSpec((tm, k), lambda j, i: (row(i), 0)), w_spec]
    args = [a, w]
    scratch = [pltpu.VMEM(w_tile, BF16)] if cast_w else []
    heads = epilogue == "heads"

    def out_for(rows, tile_rows, row_index):
        if heads:
            return (pl.BlockSpec((tile_rows, tn // LANES, LANES), lambda j, i: (row_index(i), j, 0)),
                    jax.ShapeDtypeStruct((rows, n // LANES, LANES), out_dtype))
        return (pl.BlockSpec((tile_rows, tn), lambda j, i: (row_index(i), j)),
                jax.ShapeDtypeStruct((rows, n), out_dtype))

    spec, shape = out_for(m, tm, row)
    out_specs, out_shapes = [spec], [shape]
    seq_tiles = None
    if epilogue == "residual":
        in_specs.append(pl.BlockSpec((tm, tn), lambda j, i: (row(i), j)))
        args.append(residual)
    elif epilogue == "conv_silu":
        assert seq % tm == 0
        seq_tiles = seq // tm
        conv_w, conv_b = conv
        in_specs += [pl.BlockSpec((CONV_WIDTH, tn), lambda j, i: (0, j)),
                     pl.BlockSpec((1, tn), lambda j, i: (0, j))]
        args += [conv_w, conv_b.reshape(1, n)]
        out_specs.append(pl.BlockSpec((None, SUBLANES, tn), lambda j, i: (row(i) // seq_tiles, 0, j)))
        out_shapes.append(jax.ShapeDtypeStruct((m // seq, SUBLANES, n), F32))
        scratch.append(pltpu.VMEM((tm + SUBLANES, tn), F32))
    if second:
        m2 = a2.shape[0]
        in_specs.append(pl.BlockSpec((m2, k), lambda j, i: (0, 0)))
        args.append(a2)
        spec, shape = out_for(m2, m2, lambda i: 0)
        out_specs.append(spec)
        out_shapes.append(shape)
    outs = pl.pallas_call(
        functools.partial(_mm_kernel, epilogue=epilogue, cast_w=cast_w, seq_tiles=seq_tiles,
                          w_transposed=w_rows is not None, row_tiles=row_tiles, second=second),
        grid=(n // tn, row_tiles + (1 if second else 0)),
        in_specs=in_specs,
        out_specs=out_specs,
        out_shape=out_shapes,
        scratch_shapes=scratch,
        compiler_params=_params(2),
        name=name,
    )(*args)
    return outs[0] if len(outs) == 1 else tuple(outs)


def _merge_kernel(*refs, cast_w, row_tiles, second):
    refs = list(refs)
    wsbf_ref = wabf_ref = None
    if cast_w:
        wabf_ref = refs.pop()
        wsbf_ref = refs.pop()
    it = iter(refs)
    first_in = [next(it) for _ in range(4)]
    ws_ref, wa_ref = next(it), next(it)
    second_in = [next(it) for _ in range(4)] if second else None
    out_ref = next(it)
    out2_ref = next(it) if second else None
    ws_ref = _bf16_weights(ws_ref, wsbf_ref)
    wa_ref = _bf16_weights(wa_ref, wabf_ref)

    def part(y_ref, o_ref, gs_ref, ga_ref, dst_ref):
        ps = jnp.dot(y_ref[...], ws_ref[...], preferred_element_type=F32)
        pa = jnp.dot(o_ref[...], wa_ref[...], preferred_element_type=F32)
        dst_ref[...] = (_sigmoid(gs_ref[...]) * ps + _sigmoid(ga_ref[...]) * pa).astype(dst_ref.dtype)

    if second:
        pl.when(pl.program_id(1) < row_tiles)(lambda: part(*first_in, out_ref))
        pl.when(pl.program_id(1) == row_tiles)(lambda: part(*second_in, out2_ref))
    else:
        part(*first_in, out_ref)


def merge_proj(y_ssm, o_att, gates, w_ssm, w_att, *, tm, tn, second=None):
    m = y_ssm.shape[0]
    n = w_ssm.shape[1]
    tm = min(tm, m)
    n_col = n // tn
    row_tiles = m // tm
    row = (lambda i: jnp.minimum(i, row_tiles - 1)) if second else (lambda i: i)
    cast_w = w_ssm.dtype != BF16
    scratch = []
    if cast_w:
        scratch = [pltpu.VMEM((w_ssm.shape[0], tn), BF16), pltpu.VMEM((w_att.shape[0], tn), BF16)]

    def operand_specs(tile_rows, row_index):
        return [pl.BlockSpec((tile_rows, y_ssm.shape[1]), lambda j, i: (row_index(i), 0)),
                pl.BlockSpec((tile_rows, o_att.shape[1]), lambda j, i: (row_index(i), 0)),
                pl.BlockSpec((tile_rows, tn), lambda j, i: (row_index(i), j)),
                pl.BlockSpec((tile_rows, tn), lambda j, i: (row_index(i), j + n_col))]

    in_specs = operand_specs(tm, row) + [pl.BlockSpec((w_ssm.shape[0], tn), lambda j, i: (0, j)),
                                        pl.BlockSpec((w_att.shape[0], tn), lambda j, i: (0, j))]
    args = [y_ssm, o_att, gates, gates, w_ssm, w_att]
    out_specs = [pl.BlockSpec((tm, tn), lambda j, i: (row(i), j))]
    out_shapes = [jax.ShapeDtypeStruct((m, n), BF16)]
    if second:
        y2, o2, g2 = second
        m2 = y2.shape[0]
        in_specs += operand_specs(m2, lambda i: 0)
        args += [y2, o2, g2, g2]
        out_specs.append(pl.BlockSpec((m2, tn), lambda j, i: (0, j)))
        out_shapes.append(jax.ShapeDtypeStruct((m2, n), BF16))
    outs = pl.pallas_call(
        functools.partial(_merge_kernel, cast_w=cast_w, row_tiles=row_tiles, second=bool(second)),
        grid=(n_col, row_tiles + (1 if second else 0)),
        in_specs=in_specs,
        out_specs=out_specs,
        out_shape=out_shapes,
        scratch_shapes=scratch,
        compiler_params=_params(2),
        name="merge_proj",
    )(*args)
    return outs[0] if len(outs) == 1 else tuple(outs)


def _conv_sample_kernel(x0_ref, x1_ref, x2_ref, x3_ref, w_ref, b_ref, o_ref):
    acc = b_ref[...] + x0_ref[...] * w_ref[0:1, :]
    for i, x_ref in enumerate((x1_ref, x2_ref, x3_ref), start=1):
        acc = acc + x_ref[...] * w_ref[i:i + 1, :]
    o_ref[...] = _silu(acc)


def conv_sample(xpad, conv_w, conv_b, *, steps, tc=512):
    bsz = xpad.shape[0]
    c = conv_w.shape[1]
    nct = c // tc
    x_specs = [pl.BlockSpec((bsz, tc), functools.partial(lambda t, j, i: (0, (t + i) * nct + j), i=i))
               for i in range(CONV_WIDTH)]
    return pl.pallas_call(
        _conv_sample_kernel,
        grid=(steps, nct),
        in_specs=x_specs + [pl.BlockSpec((CONV_WIDTH, tc), lambda t, j: (0, j)),
                            pl.BlockSpec((1, tc), lambda t, j: (0, j))],
        out_specs=pl.BlockSpec((bsz, tc), lambda t, j: (0, t * nct + j)),
        out_shape=jax.ShapeDtypeStruct((bsz, steps * c), F32),
        compiler_params=_params(2),
        name="conv_sample",
    )(xpad, xpad, xpad, xpad, conv_w, conv_b.reshape(1, c))


def _split3(a):
    hi = a.astype(BF16)
    rest = a - hi.astype(F32)
    mid = rest.astype(BF16)
    return hi, mid, (rest - mid.astype(F32)).astype(BF16)


def _ssd_group(gl, x_ref, b_ref, c_ref, dtc_ref, dtr_ref, z_ref, pc_ref, pr_ref, nrm_ref, y_ref, h_ref,
               ht_ref, *, q_len, valid_len, last_chunk):
    nh, hd = HEADS_PER_SSM_GROUP, SSM_HEAD_DIM
    pair_w = 2 * hd
    gw = SSM_GROUP_WIDTH
    x_cols = slice(gl * gw, (gl + 1) * gw)
    n_cols = slice(gl * SSM_STATE, (gl + 1) * SSM_STATE)

    pc = pc_ref[gl]
    pr = pr_ref[gl]
    dt_c = _softplus(dtc_ref[gl] + pc[0:1, :])
    dt_r = _softplus(dtr_ref[gl] + pr[:, 0:1])
    if valid_len < q_len:
        dt_c = jnp.where(lax.broadcasted_iota(jnp.int32, dt_c.shape, 0) < valid_len, dt_c, 0.0)
        dt_r = jnp.where(lax.broadcasted_iota(jnp.int32, dt_r.shape, 1) < valid_len, dt_r, 0.0)
    a_c = dt_c * (-jnp.exp(pc[1:2, :]))
    a_r = dt_r * (-jnp.exp(pr[:, 1:2]))
    ti = lax.broadcasted_iota(jnp.int32, (q_len, q_len), 0)
    tj = lax.broadcasted_iota(jnp.int32, (q_len, q_len), 1)
    causal = ti >= tj
    tril = causal.astype(BF16)
    triu = (ti <= tj).astype(BF16)
    acum_c = sum(jnp.dot(tril, part, preferred_element_type=F32) for part in _split3(a_c))
    acum_r = sum(jnp.dot(part, triu, preferred_element_type=F32) for part in _split3(a_r))
    end_c = acum_c[q_len - 1:q_len, :]
    dd_c = jnp.exp(end_c - acum_c) * dt_c
    src_r = acum_r - jnp.log(dt_r)

    x = x_ref[:, x_cols]
    cmat = c_ref[:, n_cols].astype(BF16)
    if ht_ref is None:
        bmat = b_ref[:, n_cols].astype(BF16)
        cb = _dot_nt(cmat, bmat)
        h_prev = h_ref[gl * nh:(gl + 1) * nh]
        y_off_all = _dot_nt(cmat, h_prev.reshape(nh * hd, SSM_STATE).astype(BF16))
    else:
        b_t = jnp.transpose(b_ref[:, n_cols]).astype(BF16)
        cb = jnp.dot(cmat, b_t, preferred_element_type=F32)
        ht_prev = ht_ref[gl]
        y_off_all = jnp.dot(cmat, ht_prev.astype(BF16), preferred_element_type=F32)

    lane = lax.broadcasted_iota(jnp.int32, (q_len, pair_w), 1)
    first_half = lane < hd

    def pair_bcast(cols, j0):
        return jnp.where(first_half,
                         jnp.broadcast_to(cols[:, j0:j0 + 1], (q_len, pair_w)),
                         jnp.broadcast_to(cols[:, j0 + 1:j0 + 2], (q_len, pair_w)))

    yield

    y_diags = []
    for i in range(nh // 2):
        j0 = 2 * i
        x_pair = x[:, i * pair_w:(i + 1) * pair_w]
        y_diag = jnp.zeros((q_len, pair_w), F32)
        for j in (j0, j0 + 1):
            seg = jnp.broadcast_to(acum_c[:, j:j + 1], (q_len, q_len)) - src_r[j:j + 1, :]
            w = (cb * jnp.exp(jnp.where(causal, seg, NEG_INF))).astype(BF16)
            own = first_half if j == j0 else jnp.logical_not(first_half)
            y_diag = y_diag + jnp.dot(w, jnp.where(own, x_pair, 0.0).astype(BF16),
                                      preferred_element_type=F32)
        y_diags.append(y_diag)
    yield

    ys = []
    xws = []
    sumsq = jnp.zeros((q_len, 1), F32)
    for i in range(nh // 2):
        j0 = 2 * i
        x_pair = x[:, i * pair_w:(i + 1) * pair_w]
        e_pair = jnp.exp(pair_bcast(acum_c, j0))
        d_pair = jnp.where(first_half[0:1, :], pc[2:3, j0:j0 + 1], pc[2:3, j0 + 1:j0 + 2])
        y = y_diags[i] + y_off_all[:, i * pair_w:(i + 1) * pair_w] * e_pair + d_pair * x_pair
        y = y * _silu(z_ref[:, gl * gw + i * pair_w:gl * gw + (i + 1) * pair_w])
        sumsq = sumsq + jnp.sum(y * y, axis=-1, keepdims=True)
        ys.append(y)
        xws.append((x_pair * pair_bcast(dd_c, j0)).astype(BF16))

    if ht_ref is None:
        for i in range(nh // 2):
            s_pair = _dot_tn(xws[i], bmat)
            for jj, j in enumerate((2 * i, 2 * i + 1)):
                chunk_decay = jnp.exp(acum_r[j:j + 1, q_len - 1:q_len])
                h_ref[gl * nh + j] = h_prev[j] * chunk_decay + s_pair[jj * hd:(jj + 1) * hd, :]
    else:
        end_decay = jnp.exp(end_c)
        decay_row = jnp.concatenate(
            [jnp.where(first_half[0:1, :], end_decay[:, 2 * i:2 * i + 1], end_decay[:, 2 * i + 1:2 * i + 2])
             for i in range(nh // 2)], axis=1)
        ht_new = ht_prev * decay_row + jnp.dot(b_t, jnp.concatenate(xws, axis=1), preferred_element_type=F32)
        ht_ref[gl] = ht_new

        @pl.when(last_chunk)
        def _():
            for i in range(nh // 2):
                both = jnp.transpose(ht_new[:, i * pair_w:(i + 1) * pair_w])
                h_ref[gl * nh + 2 * i] = both[:hd, :]
                h_ref[gl * nh + 2 * i + 1] = both[hd:, :]

    inv = lax.rsqrt(sumsq * (1.0 / (nh * hd)) + EPS)
    for i in range(nh // 2):
        sl = slice(gl * gw + i * pair_w, gl * gw + (i + 1) * pair_w)
        y_ref[:, sl] = (ys[i] * inv * nrm_ref[:, sl]).astype(y_ref.dtype)


def _ssd_kernel(*refs, q_len, valid_len, has_h0, groups_per_step):
    if has_h0:
        h0_ref = refs[9]
        refs = refs[:9] + refs[10:] + (None,)
    h_ref, ht_ref = refs[-2:]
    chunk = pl.program_id(2)

    @pl.when(chunk == 0)
    def _():
        if has_h0:
            h_ref[...] = h0_ref[...]
        else:
            ht_ref[...] = jnp.zeros(ht_ref.shape, F32)

    groups = [_ssd_group(gl, *refs, q_len=q_len, valid_len=valid_len,
                         last_chunk=chunk == pl.num_programs(2) - 1) for gl in range(groups_per_step)]
    if has_h0:
        while groups:
            groups = [g for g in groups if next(g, StopIteration) is not StopIteration]
    else:
        for g in groups:
            for _ in g:
                pass


def ssd(u, dt_raw, z, dt_bias, a_log, d_skip, ssm_norm, h0, *, batch, seq, q_len, valid_len, groups_per_step):
    nh, g, gps = HEADS_PER_SSM_GROUP, SSM_GROUPS, groups_per_step
    nc = seq // q_len
    m = batch * seq
    dt4 = dt_raw[:, :SSM_HEADS].reshape(batch * nc, q_len, g, nh)
    dtc = jnp.transpose(dt4, (0, 2, 1, 3))
    dtr = jnp.transpose(dt4, (0, 2, 3, 1))
    pcol = jnp.stack([dt_bias, a_log, d_skip]).reshape(3, g, nh).transpose(1, 0, 2)
    prow = jnp.transpose(pcol, (0, 2, 1))
    b_off = D_INNER // (gps * SSM_STATE)
    c_off = b_off + g // gps
    row = lambda b, gi, c: b * nc + c
    in_specs = [
        pl.BlockSpec((q_len, gps * SSM_GROUP_WIDTH), lambda b, gi, c: (row(b, gi, c), gi)),
        pl.BlockSpec((q_len, gps * SSM_STATE), lambda b, gi, c: (row(b, gi, c), b_off + gi)),
        pl.BlockSpec((q_len, gps * SSM_STATE), lambda b, gi, c: (row(b, gi, c), c_off + gi)),
        pl.BlockSpec((None, gps, q_len, nh), lambda b, gi, c: (row(b, gi, c), gi, 0, 0)),
        pl.BlockSpec((None, gps, nh, q_len), lambda b, gi, c: (row(b, gi, c), gi, 0, 0)),
        pl.BlockSpec((q_len, gps * SSM_GROUP_WIDTH), lambda b, gi, c: (row(b, gi, c), gi)),
        pl.BlockSpec((gps, 3, nh), lambda b, gi, c: (gi, 0, 0)),
        pl.BlockSpec((gps, nh, 3), lambda b, gi, c: (gi, 0, 0)),
        pl.BlockSpec((1, gps * SSM_GROUP_WIDTH), lambda b, gi, c: (0, gi)),
    ]
    args = [u, u, u, dtc, dtr, z, pcol, prow, ssm_norm.reshape(1, D_INNER)]
    h_spec = pl.BlockSpec((None, gps * nh, SSM_HEAD_DIM, SSM_STATE), lambda b, gi, c: (b, gi, 0, 0))
    scratch = []
    if h0 is not None:
        in_specs.append(h_spec)
        args.append(h0)
    else:
        scratch.append(pltpu.VMEM((gps, SSM_STATE, SSM_GROUP_WIDTH), F32))
    return pl.pallas_call(
        functools.partial(_ssd_kernel, q_len=q_len, valid_len=valid_len, has_h0=h0 is not None,
                          groups_per_step=gps),
        grid=(batch, g // gps, nc),
        in_specs=in_specs,
        out_specs=[pl.BlockSpec((q_len, gps * SSM_GROUP_WIDTH), lambda b, gi, c: (row(b, gi, c), gi)), h_spec],
        out_shape=[jax.ShapeDtypeStruct((m, D_INNER), BF16),
                   jax.ShapeDtypeStruct((batch, SSM_HEADS, SSM_HEAD_DIM, SSM_STATE), F32)],
        scratch_shapes=scratch,
        compiler_params=_params(3),
        name="ssd",
    )(*args)


def _attn_prompt_kernel(*refs, kb, has_prev):
    if has_prev:
        q_ref, kc_ref, vc_ref, mb_ref, o_ref, l_ref, kp_ref, vp_ref = refs
    else:
        q_ref, kc_ref, vc_ref, mb_ref, o_ref, l_ref = refs
    not_first = pl.program_id(2) > 0
    scale = ATT_HEAD_DIM ** -0.5
    heads = range(ATT_HEADS_PER_GROUP)
    if has_prev:
        @pl.when(pl.program_id(2) == 0)
        def _():
            kp_ref[...] = jnp.zeros(kp_ref.shape, BF16)
            vp_ref[...] = jnp.zeros(vp_ref.shape, BF16)

    scores = []
    for h in heads:
        q = (_load_head(q_ref, h) * scale).astype(BF16)
        k_cur = _load_head(kc_ref, h).astype(BF16)
        s_c = _dot_nt(q, k_cur) + mb_ref[h, :, kb:]
        s_p = None
        if has_prev:
            s_p = _dot_nt(q, kp_ref[h]) + mb_ref[h, :, :kb]
            s_p = jnp.where(not_first, s_p, NEG_INF)
            kp_ref[h] = k_cur
        scores.append((s_c, s_p))
    probs = []
    for s_c, s_p in scores:
        m = jnp.max(s_c, axis=-1, keepdims=True)
        if has_prev:
            m = jnp.maximum(m, jnp.max(s_p, axis=-1, keepdims=True))
        p_c = jnp.exp(s_c - m)
        den = jnp.sum(p_c, axis=-1, keepdims=True)
        p_p = None
        if has_prev:
            p_p = jnp.exp(s_p - m)
            den = den + jnp.sum(p_p, axis=-1, keepdims=True)
            p_p = p_p.astype(BF16)
        probs.append((p_c.astype(BF16), p_p, den, m + jnp.log(den)))
    for h, (p_c, p_p, den, lse) in zip(heads, probs):
        v_cur = _load_head(vc_ref, h).astype(BF16)
        o = jnp.dot(p_c, v_cur, preferred_element_type=F32)
        if has_prev:
            o = o + jnp.dot(p_p, vp_ref[h], preferred_element_type=F32)
            vp_ref[h] = v_cur
        _store_head(o_ref, h, o / den)
        _store_head(l_ref, h, jnp.broadcast_to(lse, (kb, ATT_HEAD_DIM)))


def attn_prompt(q, kv, mbias, *, group, batch, seq, dil, kb):
    m_res = seq // dil
    nb = m_res // kb
    has_prev = nb > 1
    nh = ATT_HEADS_PER_GROUP
    q4 = q.reshape(batch * m_res, dil, q.shape[1], ATT_HEAD_DIM)
    kv4 = kv.reshape(batch * m_res, dil, 2 * nh, ATT_HEAD_DIM)

    def cur(head_block):
        return pl.BlockSpec((kb, None, nh, ATT_HEAD_DIM), lambda b, r, n: (b * nb + n, r, head_block, 0))

    carry = [pltpu.VMEM((nh, kb, ATT_HEAD_DIM), BF16)] * 2 if has_prev else []
    o, lse = pl.pallas_call(
        functools.partial(_attn_prompt_kernel, kb=kb, has_prev=has_prev),
        grid=(batch, dil, nb),
        in_specs=[cur(group), cur(0), cur(1), pl.BlockSpec(mbias.shape, lambda b, r, n: (0, 0, 0))],
        out_specs=[cur(0), cur(0)],
        out_shape=[jax.ShapeDtypeStruct((batch * m_res, dil, nh, ATT_HEAD_DIM), F32)] * 2,
        scratch_shapes=carry,
        compiler_params=_params(3),
        name=f"attn_prompt_g{group}",
    )(q4, kv4, kv4, mbias)
    return o.reshape(batch * seq, nh, ATT_HEAD_DIM), lse.reshape(batch * seq, nh, ATT_HEAD_DIM)


def _attn_sample_kernel(*refs, n_sets, steps, q_rows):
    q_ref, new_ref = refs[:2]
    cache_refs = refs[2:2 + n_sets]
    bias_ref, o_ref, l_ref = refs[2 + n_sets:]
    gw = ATT_GROUP_WIDTH
    nh = ATT_HEADS_PER_GROUP
    hd = ATT_HEAD_DIM
    scale = ATT_HEAD_DIM ** -0.5
    bb = q_ref.shape[0]
    cache_rows = cache_refs[0].shape[1]
    new_rows = new_ref.shape[1]
    row_stride = 2 * nh
    per_set = steps // n_sets
    sub = lax.broadcasted_iota(jnp.int32, (nh, gw), 0)
    own = sub == lax.broadcasted_iota(jnp.int32, (nh, gw), 1) // hd
    sets = [(bi, ci) for bi in range(bb) for ci in range(n_sets)]

    def keys_or_values(bi, ci, first_head):
        base = bi * cache_rows * row_stride
        return jnp.concatenate(
            [jnp.concatenate([_load_rows(cache_refs[ci], base + first_head + h, cache_rows, row_stride),
                              _load_rows(new_ref, bi * new_rows * 2 * nh + first_head + h, new_rows, 2 * nh)], axis=0)
             for h in range(nh)], axis=1).astype(BF16)

    scores = []
    for bi, ci in sets:
        q_blocks = []
        for tt in range(per_set):
            q_t = q_ref[bi, ci * per_set + tt] * scale
            q_blocks.append(jnp.where(own, jnp.concatenate([q_t] * nh, axis=1), 0.0))
        if q_rows > per_set * nh:
            q_blocks.append(jnp.zeros((q_rows - per_set * nh, gw), F32))
        qbd = jnp.concatenate(q_blocks, axis=0).astype(BF16)
        scores.append(_dot_nt(qbd, keys_or_values(bi, ci, 0)) + bias_ref[ci])
    probs = []
    for s in scores:
        m = jnp.max(s, axis=-1, keepdims=True)
        p = jnp.exp(s - m)
        den = jnp.sum(p, axis=-1, keepdims=True)
        probs.append((p.astype(BF16), den, m + jnp.log(den)))
    for (bi, ci), (p, den, lse) in zip(sets, probs):
        o_all = jnp.dot(p, keys_or_values(bi, ci, nh), preferred_element_type=F32) / den
        for tt in range(per_set):
            t = ci * per_set + tt
            rows = slice(tt * nh, (tt + 1) * nh)
            o_t = jnp.zeros((nh, hd), F32)
            for h in range(nh):
                o_t = o_t + jnp.where(sub[:, :hd] == h, o_all[rows, h * hd:(h + 1) * hd], 0.0)
            o_ref[bi, t] = o_t
            l_ref[bi, t] = jnp.broadcast_to(lse[rows, :], (nh, hd))


def attn_sample(q, kv_new_pad, cache, bias, *, group, batch, steps, dil, batch_block):
    nh, hd = ATT_HEADS_PER_GROUP, ATT_HEAD_DIM
    bb = batch_block
    n_sets, q_rows, n_keys_tot = bias.shape
    new_rows = kv_new_pad.shape[1]
    cache_rows = n_keys_tot - new_rows
    q4 = q.reshape(batch, steps, q.shape[1], hd)
    if n_sets == 1:
        cache_view = cache.reshape(batch, cache_rows, 2 * nh, hd)
        cache_specs = [pl.BlockSpec((bb, cache_rows, 2 * nh, hd), lambda b: (b, 0, 0, 0))]
    else:
        cache_view = cache.reshape(batch, cache_rows, dil, 2 * nh, hd)
        cache_specs = [pl.BlockSpec((bb, cache_rows, None, 2 * nh, hd),
                                    functools.partial(lambda b, t: (b, 0, t, 0, 0), t=t))
                       for t in range(n_sets)]
    out_spec = pl.BlockSpec((bb, steps, nh, hd), lambda b: (b, 0, 0, 0))
    o, lse = pl.pallas_call(
        functools.partial(_attn_sample_kernel, n_sets=n_sets, steps=steps, q_rows=q_rows),
        grid=(batch // bb,),
        in_specs=[pl.BlockSpec((bb, steps, nh, hd), lambda b: (b, 0, group, 0)),
                  pl.BlockSpec((bb, new_rows, 2 * nh, hd), lambda b: (b, 0, 0, 0))]
                 + cache_specs
                 + [pl.BlockSpec(bias.shape, lambda b: (0, 0, 0))],
        out_specs=[out_spec, out_spec],
        out_shape=[jax.ShapeDtypeStruct((batch, steps, nh, hd), F32)] * 2,
        compiler_params=_params(1),
        name=f"attn_sample_g{group}",
    )(q4, kv_new_pad, *([cache_view] * n_sets), bias)
    return o.reshape(batch * steps, nh, hd), lse.reshape(batch * steps, nh, hd)


def _attn_combine_kernel(o0, o1, o2, l0, l1, l2, out_ref, mix_ref):
    ls = (l0[...], l1[...], l2[...])
    m = jnp.maximum(jnp.maximum(ls[0], ls[1]), ls[2])
    ws = [jnp.exp(l - m) for l in ls]
    num = ws[0] * o0[...] + ws[1] * o1[...] + ws[2] * o2[...]
    mix_ref[...] = num / (ws[0] + ws[1] + ws[2])
    for h in range(ATT_HEADS_PER_GROUP):
        out_ref[:, h * ATT_HEAD_DIM:(h + 1) * ATT_HEAD_DIM] = _load_head(mix_ref, h).astype(out_ref.dtype)


def attn_combine(outs, lses, tm=512):
    m, nh, hd = outs[0].shape
    tm = min(tm, m)
    spec = pl.BlockSpec((tm, nh, hd), lambda i: (i, 0, 0))
    return pl.pallas_call(
        _attn_combine_kernel,
        grid=(m // tm,),
        in_specs=[spec] * 6,
        out_specs=pl.BlockSpec((tm, nh * hd), lambda i: (i, 0)),
        out_shape=jax.ShapeDtypeStruct((m, nh * hd), BF16),
        scratch_shapes=[pltpu.VMEM((tm, nh, hd), F32)],
        compiler_params=_params(1),
        name="attn_combine",
    )(*outs, *lses)


def _t5_bucket(dist):
    max_exact = N_BUCKETS // 2
    d = np.maximum(dist, max_exact).astype(np.float32)
    large = max_exact + (np.log(d / max_exact) / np.log(MAX_DISTANCE / max_exact)
                         * (N_BUCKETS - max_exact)).astype(np.int32)
    large = np.minimum(large, N_BUCKETS - 1)
    return np.where(dist < max_exact, dist, large).astype(np.int32)


def _group_bias(rel_bias, g):
    win, dil = ATT_PATTERNS[g]
    buckets = _t5_bucket(np.arange(win // dil + 1) * dil)
    return rel_bias[buckets][:, g * ATT_HEADS_PER_GROUP:(g + 1) * ATT_HEADS_PER_GROUP]


def _prompt_mask_bias(bias_g, n_keys, kb):
    period = 3 * kb
    offs = np.arange(period)
    offs = np.where(offs >= 2 * kb, offs - period, offs)
    j_of = np.clip(kb - offs, 0, n_keys)
    a = jnp.transpose(bias_g[j_of]).astype(F32)
    nh = a.shape[0]
    tiled = jnp.tile(a, (1, kb))[:, :kb * (period - 1)].reshape(nh, kb, period - 1)
    rel = np.arange(kb)[:, None] + kb - np.arange(2 * kb)[None, :]
    valid = (rel >= 0) & (rel <= n_keys)
    return jnp.where(valid[None], tiled[:, :, :2 * kb], NEG_INF)


def _sample_mask_bias(bias_g, *, cache_len, dil, n_keys, steps, new_rows, shared_cache, q_rows):
    nh = ATT_HEADS_PER_GROUP
    n_cache = 1 if shared_cache else steps
    per_cache = steps // n_cache
    rows_per_cache = cache_len if shared_cache else cache_len // dil
    blocks = []
    for ci in range(n_cache):
        if shared_cache:
            cache_pos = np.arange(rows_per_cache)
        else:
            cache_pos = ci + dil * np.arange(rows_per_cache)
        key_pos = np.concatenate([cache_pos, cache_len + np.arange(new_rows)])
        key_live = np.concatenate([np.ones(rows_per_cache, bool), np.arange(new_rows) < steps])
        rows = []
        for tt in range(per_cache):
            t = ci * per_cache + tt
            diff = cache_len + t - key_pos
            valid = key_live & (diff >= 0) & (diff % dil == 0) & (diff // dil <= n_keys)
            j = np.clip(diff // dil, 0, n_keys)
            rows.append(jnp.where(valid[None, :], bias_g[j].T.astype(F32), NEG_INF))
        blk = jnp.concatenate(rows, axis=0)
        pad = q_rows - per_cache * nh
        if pad:
            blk = jnp.concatenate([blk, jnp.zeros((pad, blk.shape[1]), F32)], axis=0)
        blocks.append(blk)
    return jnp.stack(blocks)


SAMPLE_NEW_ROWS = 16
SAMPLE_Q_ROWS_MIN = 16
SSD_SAMPLE_Q = 8
SAMPLE_SHARED_BATCH_BLOCK = 8
SAMPLE_DILATED_BATCH_BLOCK = 4
SSD_PROMPT_GROUPS_PER_STEP = 8


IN_SIZES = (D_INNER, CONV_DIM, SSM_HEADS, ATT_WIDTH, ATT_WIDTH, ATT_WIDTH, D_MODEL, D_MODEL)
IN_OFFSETS = tuple(int(o) for o in np.concatenate([[0], np.cumsum(IN_SIZES)]))


IN_TILE = 1024
assert all(o % W_ROW_ALIGN == 0 for o in IN_OFFSETS)


def _in_rows(seg, stride=IN_TILE, first=0):
    return lambda j: IN_OFFSETS[seg] + first + j * stride


PROMPT_ROW_TILE = 1024


def _ssd_branches(u_p, u_s, dt_p, dt_s, z_p, z_s, w, ssm_state, *, bp, sp, bs, ss):
    params = (w["dt_bias"], w["a_log"], w["d_skip"], w["ssm_norm"])
    y_p, ssm_p = ssd(u_p, dt_p, z_p, *params, None, batch=bp, seq=sp, q_len=SSD_CHUNK,
                     valid_len=SSD_CHUNK, groups_per_step=SSD_PROMPT_GROUPS_PER_STEP)
    qp = SSD_SAMPLE_Q
    pad_rows = lambda a: jnp.pad(a.reshape(bs, ss, a.shape[-1]),
                                 ((0, 0), (0, qp - ss), (0, 0))).reshape(bs * qp, a.shape[-1])
    y_pad, ssm_s = ssd(pad_rows(u_s), pad_rows(dt_s), pad_rows(z_s), *params, ssm_state,
                       batch=bs, seq=qp, q_len=qp, valid_len=ss, groups_per_step=SSM_GROUPS)
    y_s = y_pad.reshape(bs, qp, D_INNER)[:, :ss].reshape(bs * ss, D_INNER)
    return y_p, y_s, ssm_p, ssm_s


def _attention_branches(q_p, q_s, kvs_p, kvs_s, caches, rel_bias, *, bp, sp, bs, ss):
    nh, hd = ATT_HEADS_PER_GROUP, ATT_HEAD_DIM
    outs_p, lses_p, outs_s, lses_s = [], [], [], []
    for g, (win, dil) in enumerate(ATT_PATTERNS):
        n_keys = win // dil
        bias_g = _group_bias(rel_bias, g)
        mb = _prompt_mask_bias(bias_g, n_keys, n_keys)
        o, lse = attn_prompt(q_p, kvs_p[g], mb, group=g, batch=bp, seq=sp, dil=dil, kb=n_keys)
        outs_p.append(o)
        lses_p.append(lse)

        cache = caches[g]
        cache_len = cache.shape[1]
        shared = dil == 1
        q_rows = max(SAMPLE_Q_ROWS_MIN, (ss if shared else 1) * nh)
        sb = _sample_mask_bias(bias_g, cache_len=cache_len, dil=dil, n_keys=n_keys, steps=ss,
                               new_rows=SAMPLE_NEW_ROWS, shared_cache=shared, q_rows=q_rows)
        kv_new = jnp.pad(kvs_s[g].reshape(bs, ss, 2 * nh, hd),
                         ((0, 0), (0, SAMPLE_NEW_ROWS - ss), (0, 0), (0, 0)))
        o, lse = attn_sample(q_s, kv_new, cache, sb, group=g, batch=bs, steps=ss, dil=dil,
                             batch_block=SAMPLE_SHARED_BATCH_BLOCK if shared else SAMPLE_DILATED_BATCH_BLOCK)
        outs_s.append(o)
        lses_s.append(lse)
    return attn_combine(outs_p, lses_p), attn_combine(outs_s, lses_s)


def _forward(x_p, x_s, w, caches, conv_state, ssm_state, rel_bias, *, bp, sp, bs, ss):
    nh, hd = ATT_HEADS_PER_GROUP, ATT_HEAD_DIM
    tm = PROMPT_ROW_TILE
    h_p = rmsnorm(x_p, w["norm_mix"], BF16)
    h_s = rmsnorm(x_s, w["norm_mix"], BF16)
    proj = lambda name, **kw: matmul(h_p, w["in_t"], a2=h_s, tm=tm, tn=IN_TILE, name=name, **kw)
    z_p, z_s = proj("in_z", n=IN_SIZES[0], w_rows=_in_rows(0))
    dt_p, dt_s = matmul(h_p, w["in_t"], a2=h_s, tm=tm, tn=LANES, n=LANES, w_rows=_in_rows(2), name="in_dt")
    q_p, q_s = proj("in_q", n=IN_SIZES[3], w_rows=_in_rows(3), epilogue="heads")
    kvs = [proj(f"in_kv{g}", n=2 * ATT_GROUP_WIDTH, epilogue="heads",
                w_rows=_in_rows(4, stride=IN_OFFSETS[5] - IN_OFFSETS[4], first=g * ATT_GROUP_WIDTH))
           for g in range(ATT_N_GROUPS)]
    kvs_p, kvs_s = [kv[0] for kv in kvs], [kv[1] for kv in kvs]
    gates_p, gates_s = proj("in_gates", n=IN_SIZES[6] + IN_SIZES[7], w_rows=_in_rows(6))
    u_p, xbc_tail = matmul(h_p, w["in_t"], tm=tm, tn=IN_TILE, n=IN_SIZES[1], w_rows=_in_rows(1), name="in_xbc_conv",
                           epilogue="conv_silu", conv=(w["conv_w"], w["conv_b"]), seq=sp)
    xbc_s = matmul(h_s, w["in_t"], tm=tm, tn=IN_TILE, n=IN_SIZES[1], w_rows=_in_rows(1), name="in_xbc")
    conv_p = xbc_tail[:, SUBLANES - (CONV_WIDTH - 1):]
    xpad = jnp.concatenate([conv_state.reshape(bs, (CONV_WIDTH - 1) * CONV_DIM),
                            xbc_s.reshape(bs, ss * CONV_DIM)], axis=1)
    u_s = conv_sample(xpad, w["conv_w"], w["conv_b"], steps=ss).reshape(bs * ss, CONV_DIM)
    conv_s = xpad.reshape(bs, CONV_WIDTH - 1 + ss, CONV_DIM)[:, ss:]

    y_p, y_s, ssm_p, ssm_s = _ssd_branches(u_p, u_s, dt_p, dt_s, z_p, z_s, w, ssm_state,
                                           bp=bp, sp=sp, bs=bs, ss=ss)
    o_p, o_s = _attention_branches(q_p, q_s, kvs_p, kvs_s, caches, rel_bias, bp=bp, sp=sp, bs=bs, ss=ss)

    merged_p = merge_proj(y_p, o_p, gates_p, w["ssm_proj"], w["att_proj"], tm=tm, tn=512)
    merged_s = merge_proj(y_s, o_s, gates_s, w["ssm_proj"], w["att_proj"], tm=tm, tn=512)
    out_proj = lambda a, res: matmul(a, w["out"], tm=tm, tn=1024, epilogue="residual", residual=res, name="out_proj")
    x1_p, x1_s = out_proj(merged_p, x_p), out_proj(merged_s, x_s)
    h2_p = rmsnorm(x1_p, w["norm_mlp"], BF16)
    h2_s = rmsnorm(x1_s, w["norm_mlp"], BF16)
    mlp_up = lambda a: matmul(a, w["up"], tm=tm, tn=1024, out_dtype=BF16, epilogue="relu2", name="mlp_up")
    up_p, up_s = mlp_up(h2_p), mlp_up(h2_s)
    mlp_down = lambda a, res: matmul(a, w["down"], tm=512, tn=512, epilogue="residual", residual=res,
                                     name="mlp_down")
    x2_p, x2_s = mlp_down(up_p, x1_p), mlp_down(up_s, x1_s)
    y_out_p = rmsnorm(x2_p, w["norm_final"], F32)
    y_out_s = rmsnorm(x2_s, w["norm_final"], F32)

    rows_p = [kv.reshape(bp, sp, 2, nh, hd)[:, sp - min(win, sp):] for kv, (win, _) in zip(kvs_p, ATT_PATTERNS)]
    rows_s = [kv.reshape(bs, ss, 2, nh, hd) for kv in kvs_s]
    return (y_out_p, rows_p, conv_p, ssm_p), (y_out_s, rows_s, conv_s, ssm_s)


def kernel(x_prompt, x_sample, cache_win128, cache_win512, cache_win2048, state_conv, state_ssm, w_in, conv_w, conv_b, dt_bias, a_log, d_skip, ssm_norm, w_ssm_proj, w_att_proj, w_out, norm_mix, w_up, w_down, norm_mlp, rel_bias, norm_final):
    assert w_in.shape[0] == 1, "single-layer model"
    bp, sp, _ = x_prompt.shape
    bs, ss, _ = x_sample.shape
    w = dict(
        in_t=jnp.swapaxes(w_in[0], 0, 1),
        conv_w=conv_w[0], conv_b=conv_b[0], dt_bias=dt_bias[0], a_log=a_log[0], d_skip=d_skip[0],
        ssm_norm=ssm_norm[0], ssm_proj=w_ssm_proj[0].astype(BF16), att_proj=w_att_proj[0].astype(BF16),
        out=w_out[0], norm_mix=norm_mix[0], up=w_up[0],
        down=w_down[0].astype(BF16), norm_mlp=norm_mlp[0], norm_final=norm_final)

    (yp, rows_p, conv_p, ssm_p), (ys, rows_s, conv_s, ssm_s) = _forward(
        x_prompt.reshape(bp * sp, D_MODEL), x_sample.reshape(bs * ss, D_MODEL), w,
        (cache_win128[0], cache_win512[0], cache_win2048[0]), state_conv[0], state_ssm[0], rel_bias,
        bp=bp, sp=sp, bs=bs, ss=ss)

    return (yp.reshape(bp, sp, D_MODEL), ys.reshape(bs, ss, D_MODEL),
            rows_p[0][None], rows_p[1][None], rows_p[2][None], conv_p[None], ssm_p[None],
            rows_s[0][None], rows_s[1][None], rows_s[2][None], conv_s[None], ssm_s[None])
```
